```python
import math
import jax
import jax.numpy as jnp
from jax import lax
import numpy as np

D_MODEL = 1024
BATCH = 32
SEQ = 256
DEPTH = 2
DEC_BATCH = 4
DEC_SEQ = 2048
PAST_LEN = 512

GRID_W = 64
CHUNK = 128
EPS = 1e-6
N_DIR = 2
SSD_D_INNER = 1024
SSD_HEADDIM = 64
SSD_HEADS = SSD_D_INNER // SSD_HEADDIM
SSD_GROUPS = 4
SSD_STATE = 64
SSD_CONV = 3
SSD_CONV_CH = SSD_D_INNER + 2 * SSD_GROUPS * SSD_STATE
ML_HEADS = 8
ML_DV = 128
ML_DK = 64
ML_V_WIDTH = ML_HEADS * ML_DV
ML_QK_WIDTH = ML_HEADS * ML_DK
D_FF = 2816
FFN_CONV = 3
IN_SPLITS = (SSD_D_INNER, SSD_CONV_CH, N_DIR * SSD_HEADS, ML_QK_WIDTH, ML_QK_WIDTH, ML_V_WIDTH,
             ML_V_WIDTH, N_DIR * ML_HEADS, N_DIR * ML_HEADS, 2 * D_MODEL)
IN_COLS = sum(IN_SPLITS)

kernel_name = 'hybrid_ssd_mlstm_diffusion_step'


def _split(x, sizes):
    offsets = [int(o) for o in np.cumsum(sizes)[:-1]]
    return jnp.split(x, offsets, axis=-1)


def _flip(t):
    return jnp.flip(t, axis=1)


def rmsnorm(x, w):
    xf = x.astype(jnp.float32)
    y = xf * lax.rsqrt(jnp.mean(xf * xf, axis=-1, keepdims=True) + EPS)
    return y.astype(x.dtype) * w


def dwconv1d(x, w, b):
    y = lax.conv_general_dilated(x, w[:, None, :], window_strides=(1,), padding='SAME',
                                 dimension_numbers=('NWC', 'WIO', 'NWC'),
                                 feature_group_count=x.shape[-1])
    return y + b


def dwconv2d(x, w, b):
    y = lax.conv_general_dilated(x, w[:, :, None, :], window_strides=(1, 1), padding='SAME',
                                 dimension_numbers=('NHWC', 'HWIO', 'NHWC'),
                                 feature_group_count=x.shape[-1])
    return y + b


def ssd_chunked(x, dt, A, Bm, Cm, s0):
    f32 = jnp.float32
    b, l, h, p = x.shape
    n = Bm.shape[-1]
    c = l // CHUNK
    xc = x.astype(f32).reshape(b, c, CHUNK, h, p)
    dtc = dt.astype(f32).reshape(b, c, CHUNK, h)
    Bc = Bm.astype(f32).reshape(b, c, CHUNK, h, n)
    Cc = Cm.astype(f32).reshape(b, c, CHUNK, h, n)
    acum = jnp.cumsum(dtc * A.astype(f32), axis=2)
    acum_h = jnp.swapaxes(acum, 2, 3)
    mask = jnp.tril(jnp.ones((CHUNK, CHUNK), dtype=bool))
    decay = jnp.exp(jnp.where(mask, acum_h[..., :, None] - acum_h[..., None, :], -jnp.inf))
    xdt = xc * dtc[..., None]
    scores = jnp.einsum('bcthn,bcshn->bchts', Cc, Bc) * decay
    y_diag = jnp.einsum('bchts,bcshp->bcthp', scores, xdt)
    to_end = jnp.exp(acum[:, :, -1:, :] - acum)
    chunk_states = jnp.einsum('bcshn,bcshp->bchpn', Bc * to_end[..., None], xdt)
    chunk_decay = jnp.exp(acum[:, :, -1, :])

    def step(s, inp):
        dec, st = inp
        return dec[..., None, None] * s + st, s

    s_fin, starts = lax.scan(step, s0.astype(f32),
                             (jnp.moveaxis(chunk_decay, 1, 0), jnp.moveaxis(chunk_states, 1, 0)))
    starts = jnp.moveaxis(starts, 0, 1)
    y_off = jnp.einsum('bcthn,bchpn->bcthp', Cc * jnp.exp(acum)[..., None], starts)
    y = (y_diag + y_off).reshape(b, l, h, p)
    return y.astype(x.dtype), s_fin


def mlstm_chunked(q, k, v, i_pre, f_pre, C0, n0, m0):
    f32 = jnp.float32
    b, l, h, dk = q.shape
    dv = v.shape[-1]
    c = l // CHUNK
    qc = q.astype(f32).reshape(b, c, CHUNK, h, dk)
    kc = k.astype(f32).reshape(b, c, CHUNK, h, dk)
    vc = v.astype(f32).reshape(b, c, CHUNK, h, dv)
    ig = jnp.moveaxis(i_pre.astype(f32).reshape(b, c, CHUNK, h), 2, 3)
    logf = jax.nn.log_sigmoid(jnp.moveaxis(f_pre.astype(f32).reshape(b, c, CHUNK, h), 2, 3))
    bcum = jnp.cumsum(logf, axis=-1)
    mask = jnp.tril(jnp.ones((CHUNK, CHUNK), dtype=bool))
    log_d = jnp.where(mask, bcum[..., :, None] - bcum[..., None, :] + ig[..., None, :], -jnp.inf)
    log_end = bcum[..., -1:] - bcum + ig
    m_loc = jnp.max(log_end, axis=-1)
    w_end = jnp.exp(log_end - m_loc[..., None])
    C_loc = jnp.einsum('bchs,bcshv,bcshk->bchvk', w_end, vc, kc)
    n_loc = jnp.einsum('bchs,bcshk->bchk', w_end, kc)
    b_tot = bcum[..., -1]

    def step(carry, inp):
        Cp, np_, mp = carry
        bt, ml, Cl, nl = inp
        m_new = jnp.maximum(bt + mp, ml)
        a_old = jnp.exp(bt + mp - m_new)
        a_loc = jnp.exp(ml - m_new)
        C_new = a_old[..., None, None] * Cp + a_loc[..., None, None] * Cl
        n_new = a_old[..., None] * np_ + a_loc[..., None] * nl
        return (C_new, n_new, m_new), (Cp, np_, mp)

    (C_fin, n_fin, m_fin), (Cs, ns, ms) = lax.scan(
        step, (C0.astype(f32), n0.astype(f32), m0.astype(f32)),
        (jnp.moveaxis(b_tot, 1, 0), jnp.moveaxis(m_loc, 1, 0),
         jnp.moveaxis(C_loc, 1, 0), jnp.moveaxis(n_loc, 1, 0)))
    Cs = jnp.moveaxis(Cs, 0, 1)
    ns = jnp.moveaxis(ns, 0, 1)
    ms = jnp.moveaxis(ms, 0, 1)
    log_inter = bcum + ms[..., None]
    m_t = jnp.maximum(log_inter, jnp.max(log_d, axis=-1))
    w_intra = jnp.exp(log_d - m_t[..., None])
    w_inter = jnp.exp(log_inter - m_t)
    s_qk = jnp.einsum('bcthk,bcshk->bchts', qc, kc) * w_intra
    num = (jnp.einsum('bchts,bcshv->bchtv', s_qk, vc)
           + w_inter[..., None] * jnp.einsum('bcthk,bchvk->bchtv', qc, Cs))
    den = jnp.sum(s_qk, axis=-1) + w_inter * jnp.einsum('bcthk,bchk->bcht', qc, ns)
    den = jnp.maximum(jnp.abs(den), jnp.exp(-m_t))
    hout = num / den[..., None]
    hout = jnp.moveaxis(hout, 2, 3).reshape(b, l, h, dv)
    return hout.astype(v.dtype), C_fin, n_fin, m_fin


def mixer(u, lp, ssd_s0, ml_C0, ml_n0, ml_m0):
    b, l, _ = u.shape
    proj = u @ lp['w_in']
    z, xbc, dt_raw, q, k, v, o, i_raw, f_raw, gates = _split(proj, IN_SPLITS)
    xbc = jax.nn.silu(dwconv1d(xbc, lp['ssd_conv_w'], lp['ssd_conv_b']))
    xs, Bm, Cm = _split(xbc, (SSD_D_INNER, SSD_GROUPS * SSD_STATE, SSD_GROUPS * SSD_STATE))
    xs = xs.reshape(b, l, SSD_HEADS, SSD_HEADDIM)
    rep = SSD_HEADS // SSD_GROUPS
    Bm = jnp.repeat(Bm.reshape(b, l, SSD_GROUPS, SSD_STATE), rep, axis=2)
    Cm = jnp.repeat(Cm.reshape(b, l, SSD_GROUPS, SSD_STATE), rep, axis=2)
    dt = jax.nn.softplus(dt_raw.reshape(b, l, N_DIR, SSD_HEADS) + lp['ssd_dt_bias'])
    A = -jnp.exp(lp['ssd_a_log'])
    y_f, s_f = ssd_chunked(xs, dt[:, :, 0], A[0], Bm, Cm, ssd_s0[:, 0])
    y_b, s_b = ssd_chunked(_flip(xs), _flip(dt[:, :, 1]), A[1], _flip(Bm), _flip(Cm), ssd_s0[:, 1])
    y = y_f + _flip(y_b) + lp['ssd_d'][:, None] * xs
    y = rmsnorm(y.reshape(b, l, SSD_D_INNER) * jax.nn.silu(z), lp['ssd_norm'])
    y_ssd = y @ lp['w_o_ssd']
    q = q.reshape(b, l, ML_HEADS, ML_DK) * (ML_DK ** -0.5)
    k = k.reshape(b, l, ML_HEADS, ML_DK)
    v = v.reshape(b, l, ML_HEADS, ML_DV)
    ig = i_raw.reshape(b, l, N_DIR, ML_HEADS) + lp['ml_i_bias']
    fg = f_raw.reshape(b, l, N_DIR, ML_HEADS) + lp['ml_f_bias']
    h_f, C_f, n_f, m_f = mlstm_chunked(q, k, v, ig[:, :, 0], fg[:, :, 0],
                                       ml_C0[:, 0], ml_n0[:, 0], ml_m0[:, 0])
    h_b, C_b, n_b, m_b = mlstm_chunked(_flip(q), _flip(k), _flip(v), _flip(ig[:, :, 1]), _flip(fg[:, :, 1]),
                                       ml_C0[:, 1], ml_n0[:, 1], ml_m0[:, 1])
    hm = rmsnorm(h_f + _flip(h_b), lp['ml_norm'].reshape(ML_HEADS, ML_DV))
    hm = hm.reshape(b, l, ML_V_WIDTH) * jax.nn.sigmoid(o)
    y_ml = hm @ lp['w_o_ml']
    g_ssd, g_ml = _split(gates, (D_MODEL, D_MODEL))
    out = (jax.nn.sigmoid(g_ssd) * y_ssd + jax.nn.sigmoid(g_ml) * y_ml) @ lp['w_out']
    new_states = (jnp.stack([s_f, s_b], axis=1), jnp.stack([C_f, C_b], axis=1),
                  jnp.stack([n_f, n_b], axis=1), jnp.stack([m_f, m_b], axis=1))
    return out, new_states


def conv_ffn(u, lp, on_grid):
    g, val = _split(u @ lp['w_up'], (D_FF, D_FF))
    if on_grid:
        b, l, _ = g.shape
        rows = l // GRID_W
        g = dwconv2d(g.reshape(b, rows, GRID_W, D_FF), lp['ffn_conv_w'], lp['ffn_conv_b']).reshape(b, l, D_FF)
    else:
        g = dwconv1d(g, lp['ffn_conv_w'][1], lp['ffn_conv_b'])
    return (jax.nn.silu(g) * val) @ lp['w_down']


def trunk_layer(h, mod, lp, states, on_grid):
    shift1, scale1, gate1, shift2, scale2, gate2 = jnp.split(mod, 6, axis=-1)
    u = rmsnorm(h, lp['norm_mix']) * (1.0 + scale1) + shift1
    mix, new_states = mixer(u, lp, states[0], states[1], states[2], states[3])
    h = h + gate1 * mix
    u = rmsnorm(h, lp['norm_ffn']) * (1.0 + scale2) + shift2
    h = h + gate2 * conv_ffn(u, lp, on_grid)
    return h, new_states


def setup_inputs(seed: int = 0) -> dict:
    key = jax.random.key(seed)
    ks = jax.random.split(key, 32)

    def nrm(k, shape, scale):
        return jax.random.normal(k, shape, jnp.float32) * scale

    def gain(k, shape):
        return 1.0 + 0.05 * jax.random.normal(k, shape, jnp.float32)

    dt0 = jnp.exp(jax.random.uniform(ks[13], (DEPTH, N_DIR, SSD_HEADS), jnp.float32,
                                     math.log(1e-3), math.log(1e-1)))
    return {
        'x_prompt': nrm(ks[0], (BATCH, SEQ, D_MODEL), 1.0),
        'x_sample': nrm(ks[1], (DEC_BATCH, DEC_SEQ, D_MODEL), 1.0),
        'state_ssd': nrm(ks[2], (DEC_BATCH, DEPTH, N_DIR, SSD_HEADS, SSD_HEADDIM, SSD_STATE), 0.5),
        'state_mlstm_C': nrm(ks[3], (DEC_BATCH, DEPTH, N_DIR, ML_HEADS, ML_DV, ML_DK), 0.1),
        'state_mlstm_n': nrm(ks[4], (DEC_BATCH, DEPTH, N_DIR, ML_HEADS, ML_DK), 0.1),
        'state_mlstm_m': nrm(ks[5], (DEC_BATCH, DEPTH, N_DIR, ML_HEADS), 1.0),
        'c': nrm(ks[6], (DEC_BATCH, D_MODEL), 1.0),
        'c_ctx': nrm(ks[7], (D_MODEL,), 1.0),
        'w_ada': nrm(ks[8], (DEPTH, D_MODEL, 6 * D_MODEL), D_MODEL ** -0.5),
        'b_ada': nrm(ks[9], (DEPTH, 6 * D_MODEL), 0.02),
        'norm_mix': gain(ks[10], (DEPTH, D_MODEL)),
        'w_in': nrm(ks[11], (DEPTH, D_MODEL, IN_COLS), D_MODEL ** -0.5),
        'ssd_conv_w': nrm(ks[12], (DEPTH, SSD_CONV, SSD_CONV_CH), SSD_CONV ** -0.5),
        'ssd_conv_b': nrm(ks[15], (DEPTH, SSD_CONV_CH), 0.02),
        'ssd_dt_bias': dt0 + jnp.log(-jnp.expm1(-dt0)),
        'ssd_a_log': jnp.log(jax.random.uniform(ks[14], (DEPTH, N_DIR, SSD_HEADS), jnp.float32, 1.0, 16.0)),
        'ssd_d': gain(ks[16], (DEPTH, SSD_HEADS)),
        'ssd_norm': gain(ks[17], (DEPTH, SSD_D_INNER)),
        'w_o_ssd': nrm(ks[18], (DEPTH, SSD_D_INNER, D_MODEL), SSD_D_INNER ** -0.5),
        'ml_i_bias': nrm(ks[19], (DEPTH, N_DIR, ML_HEADS), 0.1),
        'ml_f_bias': 3.0 + nrm(ks[20], (DEPTH, N_DIR, ML_HEADS), 0.5),
        'ml_norm': gain(ks[21], (DEPTH, ML_V_WIDTH)),
        'w_o_ml': nrm(ks[22], (DEPTH, ML_V_WIDTH, D_MODEL), ML_V_WIDTH ** -0.5),
        'w_out': nrm(ks[23], (DEPTH, D_MODEL, D_MODEL), D_MODEL ** -0.5),
        'norm_ffn': gain(ks[24], (DEPTH, D_MODEL)),
        'w_up': nrm(ks[25], (DEPTH, D_MODEL, 2 * D_FF), D_MODEL ** -0.5),
        'ffn_conv_w': nrm(ks[26], (DEPTH, FFN_CONV, FFN_CONV, D_FF), 1.0 / FFN_CONV),
        'ffn_conv_b': nrm(ks[27], (DEPTH, D_FF), 0.02),
        'w_down': nrm(ks[28], (DEPTH, D_FF, D_MODEL), D_FF ** -0.5),
        'final_norm': gain(ks[29], (D_MODEL,)),
    }


def reference(x_prompt, x_sample, state_ssd, state_mlstm_C, state_mlstm_n, state_mlstm_m, c, c_ctx,
              w_ada, b_ada, norm_mix, w_in, ssd_conv_w, ssd_conv_b, ssd_dt_bias, ssd_a_log, ssd_d,
              ssd_norm, w_o_ssd, ml_i_bias, ml_f_bias, ml_norm, w_o_ml, w_out, norm_ffn, w_up,
              ffn_conv_w, ffn_conv_b, w_down, final_norm):
    dtype = x_prompt.dtype
    nb = x_prompt.shape[0]
    zero_states = (jnp.zeros((nb, N_DIR, SSD_HEADS, SSD_HEADDIM, SSD_STATE), dtype),
                   jnp.zeros((nb, N_DIR, ML_HEADS, ML_DV, ML_DK), dtype),
                   jnp.zeros((nb, N_DIR, ML_HEADS, ML_DK), dtype),
                   jnp.zeros((nb, N_DIR, ML_HEADS), dtype))
    ctx_cond = jax.nn.silu(c_ctx)[None, None, :]
    lat_cond = jax.nn.silu(c)[:, None, :]
    h_ctx = x_prompt
    h_lat = x_sample
    ssd_list, C_list, n_list, m_list = [], [], [], []
    for layer in range(DEPTH):
        lp = {
            'norm_mix': norm_mix[layer], 'w_in': w_in[layer],
            'ssd_conv_w': ssd_conv_w[layer], 'ssd_conv_b': ssd_conv_b[layer],
            'ssd_dt_bias': ssd_dt_bias[layer], 'ssd_a_log': ssd_a_log[layer], 'ssd_d': ssd_d[layer],
            'ssd_norm': ssd_norm[layer], 'w_o_ssd': w_o_ssd[layer],
            'ml_i_bias': ml_i_bias[layer], 'ml_f_bias': ml_f_bias[layer], 'ml_norm': ml_norm[layer],
            'w_o_ml': w_o_ml[layer], 'w_out': w_out[layer], 'norm_ffn': norm_ffn[layer],
            'w_up': w_up[layer], 'ffn_conv_w': ffn_conv_w[layer], 'ffn_conv_b': ffn_conv_b[layer],
            'w_down': w_down[layer],
        }
        mod_ctx = ctx_cond @ w_ada[layer] + b_ada[layer]
        mod_lat = lat_cond @ w_ada[layer] + b_ada[layer]
        h_ctx, st = trunk_layer(h_ctx, mod_ctx, lp, zero_states, False)
        ssd_list.append(st[0])
        C_list.append(st[1])
        n_list.append(st[2])
        m_list.append(st[3])
        cached = (state_ssd[:, layer], state_mlstm_C[:, layer], state_mlstm_n[:, layer], state_mlstm_m[:, layer])
        h_lat, _ = trunk_layer(h_lat, mod_lat, lp, cached, True)
    y_prompt = rmsnorm(h_ctx, final_norm)
    y_sample = rmsnorm(h_lat, final_norm)
    new_state_ssd = jnp.stack(ssd_list, axis=1).astype(dtype)
    new_state_mlstm_C = jnp.stack(C_list, axis=1).astype(dtype)
    new_state_mlstm_n = jnp.stack(n_list, axis=1).astype(dtype)
    new_state_mlstm_m = jnp.stack(m_list, axis=1).astype(dtype)
    return (y_prompt, y_sample, new_state_ssd, new_state_mlstm_C, new_state_mlstm_n, new_state_mlstm_m)
```

```python
import functools

import jax
import jax.numpy as jnp
import numpy as np
from jax import lax
from jax.experimental import pallas as pl
from jax.experimental.pallas import tpu as pltpu

F32 = jnp.float32
BF16 = jnp.bfloat16

D_MODEL = 1024
CHUNK = 128
EPS = 1e-6
GRID_W = 64
SSD_HEADS = 16
SSD_HEADDIM = 64
SSD_STATE = 64
SSD_GROUPS = 4
ML_HEADS = 8
ML_DV = 128
ML_DK = 64
D_FF = 2816

LANES = 128
BF16_SUBLANES = 16
VMEM_LIMIT = 56 * 1024 * 1024

COL_Z, COL_O, COL_GS, COL_GM = 0, 1024, 2048, 3072
COL_X, COL_B, COL_C = 4096, 5120, 5376
COL_Q, COL_K, COL_V = 5632, 6144, 6656
PROJ_COLS = 7680
GATE_COLS = 256
LANE_I = 16
LANE_F = 24


def _cparams(sem):
    return pltpu.CompilerParams(dimension_semantics=sem, vmem_limit_bytes=VMEM_LIMIT)


def _silu(x):
    return x * jax.nn.sigmoid(x)


def _softplus(x):
    return jnp.maximum(x, 0.0) + jnp.log1p(jnp.exp(-jnp.abs(x)))


def _cumsum_rows(x, reverse):
    n = x.shape[0]
    row = lax.broadcasted_iota(jnp.int32, x.shape, 0)
    k = 1
    while k < n:
        if reverse:
            x = x + jnp.where(row < n - k, pltpu.roll(x, n - k, 0), 0.0)
        else:
            x = x + jnp.where(row >= k, pltpu.roll(x, k, 0), 0.0)
        k *= 2
    return x


def _dot(a, b):
    return jnp.dot(a, b, preferred_element_type=F32)


def _dot_nt(a, b):
    return lax.dot_general(a, b, (((1,), (1,)), ((), ())), preferred_element_type=F32)


def _pair(lane_lo, col0, col1):
    return jnp.where(lane_lo, col0, col1)


def _ada_kernel(c_ref, w_ref, b_ref, o_ref):
    cond = _silu(c_ref[...]).astype(BF16)
    o_ref[0] = _dot(cond, w_ref[0].astype(BF16)) + b_ref[0]


def _ada(cond8, w_ada, b_ada):
    depth, d, n = w_ada.shape
    tn = 512
    return pl.pallas_call(
        _ada_kernel,
        grid=(depth, n // tn),
        in_specs=[pl.BlockSpec((8, d), lambda l, j: (0, 0)),
                  pl.BlockSpec((1, d, tn), lambda l, j: (l, 0, j)),
                  pl.BlockSpec((1, 1, tn), lambda l, j: (l, 0, j))],
        out_specs=pl.BlockSpec((1, 8, tn), lambda l, j: (l, 0, j)),
        out_shape=jax.ShapeDtypeStruct((depth, 8, n), F32),
        compiler_params=_cparams(("arbitrary", "arbitrary")),
        name="ada_mod",
    )(cond8, w_ada, b_ada.reshape(depth, 1, n))


def _mod_index(i, tm, n_ctx_rows, lat_len, which):
    n_ctx_tiles = n_ctx_rows // tm
    tiles_per_seq = lat_len // tm
    row = jnp.where(i < n_ctx_tiles, 0, 1 + (i - n_ctx_tiles) // tiles_per_seq)
    return row * 6 + which


def _rms(x, w):
    return (x * lax.rsqrt(jnp.mean(x * x, axis=-1, keepdims=True) + EPS)) * w


def _norm_mod_kernel(x_ref, w_ref, sc_ref, sh_ref, o_ref):
    y = _rms(x_ref[...], w_ref[...])
    o_ref[...] = (y * (1.0 + sc_ref[0]) + sh_ref[0]).astype(o_ref.dtype)


def _norm_mod(h, w, mod48, n_ctx_rows, lat_len, which_scale, which_shift, tm=512):
    t, d = h.shape
    mi = functools.partial(_mod_index, tm=tm, n_ctx_rows=n_ctx_rows, lat_len=lat_len)
    return pl.pallas_call(
        _norm_mod_kernel,
        grid=(t // tm,),
        in_specs=[pl.BlockSpec((tm, d), lambda i: (i, 0)),
                  pl.BlockSpec((1, d), lambda i: (0, 0)),
                  pl.BlockSpec((1, 1, d), lambda i: (mi(i, which=which_scale), 0, 0)),
                  pl.BlockSpec((1, 1, d), lambda i: (mi(i, which=which_shift), 0, 0))],
        out_specs=pl.BlockSpec((tm, d), lambda i: (i, 0)),
        out_shape=jax.ShapeDtypeStruct((t, d), BF16),
        compiler_params=_cparams(("parallel",)),
        name="norm_mod",
    )(h, w.reshape(1, d), mod48, mod48)


def _norm_kernel(x_ref, w_ref, o_ref):
    o_ref[...] = _rms(x_ref[...], w_ref[...])


def _final_norm(h, w, row0, rows, tm=512):
    d = h.shape[1]
    off = row0 // tm
    return pl.pallas_call(
        _norm_kernel,
        grid=(rows // tm,),
        in_specs=[pl.BlockSpec((tm, d), lambda i: (i + off, 0)),
                  pl.BlockSpec((1, d), lambda i: (0, 0))],
        out_specs=pl.BlockSpec((tm, d), lambda i: (i, 0)),
        out_shape=jax.ShapeDtypeStruct((rows, d), F32),
        compiler_params=_cparams(("parallel",)),
        name="final_norm",
    )(h, w.reshape(1, d))


def _matmul_kernel(x_ref, w_ref, o_ref):
    o_ref[...] = _dot(x_ref[...], w_ref[...]).astype(o_ref.dtype)


def _matmul(x, w, out_dtype, tm, tn, name):
    t, k = x.shape
    n = w.shape[1]
    return pl.pallas_call(
        _matmul_kernel,
        grid=(t // tm, n // tn),
        in_specs=[pl.BlockSpec((tm, k), lambda i, j: (i, 0)),
                  pl.BlockSpec((k, tn), lambda i, j: (0, j))],
        out_specs=pl.BlockSpec((tm, tn), lambda i, j: (i, j)),
        out_shape=jax.ShapeDtypeStruct((t, n), out_dtype),
        compiler_params=_cparams(("parallel", "arbitrary")),
        name=name,
    )(x, w)


def _conv_silu_chunk(src_ref, cw_ref, c, n_chunks, seq_len):
    r0 = pl.multiple_of(c * CHUNK, CHUNK)
    cur = src_ref[pl.ds(r0, CHUNK), :].astype(F32)
    p0 = pl.multiple_of(jnp.maximum(r0 - BF16_SUBLANES, 0), BF16_SUBLANES)
    n0 = pl.multiple_of(jnp.minimum(r0 + CHUNK, seq_len - BF16_SUBLANES), BF16_SUBLANES)
    prv = src_ref[pl.ds(p0, BF16_SUBLANES), :].astype(F32)[BF16_SUBLANES - 1:BF16_SUBLANES, :]
    nxt = src_ref[pl.ds(n0, BF16_SUBLANES), :].astype(F32)[0:1, :]
    prv = prv * (c > 0).astype(F32)
    nxt = nxt * (c < n_chunks - 1).astype(F32)
    row = lax.broadcasted_iota(jnp.int32, cur.shape, 0)
    up = jnp.where(row == 0, prv, pltpu.roll(cur, 1, 0))
    dn = jnp.where(row == CHUNK - 1, nxt, pltpu.roll(cur, CHUNK - 1, 0))
    y = up * cw_ref[0, 0:1, :] + cur * cw_ref[0, 1:2, :] + dn * cw_ref[0, 2:3, :] + cw_ref[0, 3:4, :]
    return _silu(y)


def _ssd_kernel(*refs, seq_len, zero_init, emit_state):
    it = iter(refs)
    x_ref, b_ref, c_ref, g_ref = next(it), next(it), next(it), next(it)
    par_ref, d_ref, cwx_ref, cwb_ref, cwc_ref = next(it), next(it), next(it), next(it), next(it)
    s0_ref = None if zero_init else next(it)
    y_ref = next(it)
    sfin_ref = next(it) if emit_state else None
    xc_s, bc_s, cc_s, yacc_s, st_s = next(it), next(it), next(it), next(it), next(it)

    n_chunks = seq_len // CHUNK

    def conv_body(c, carry):
        r0 = pl.multiple_of(c * CHUNK, CHUNK)
        xc_s[pl.ds(r0, CHUNK), :] = _conv_silu_chunk(x_ref, cwx_ref, c, n_chunks, seq_len)
        bc_s[pl.ds(r0, CHUNK), :] = _conv_silu_chunk(b_ref, cwb_ref, c, n_chunks, seq_len)
        cc_s[pl.ds(r0, CHUNK), :] = _conv_silu_chunk(c_ref, cwc_ref, c, n_chunks, seq_len)
        return carry

    lax.fori_loop(0, n_chunks, conv_body, 0)

    lane = lax.broadcasted_iota(jnp.int32, (CHUNK, LANES), 1)
    lane_lo = lane < 64
    trow = lax.broadcasted_iota(jnp.int32, (CHUNK, CHUNK), 0)
    scol = lax.broadcasted_iota(jnp.int32, (CHUNK, CHUNK), 1)
    lane_row256 = lax.broadcasted_iota(jnp.int32, (1, 256), 1)
    bias_row = par_ref[0, 0:1, :]
    a_row = -jnp.exp(par_ref[0, 1:2, :])

    for d in (0, 1):
        mask = (scol <= trow) if d == 0 else (scol >= trow)
        if zero_init:
            st_s[...] = jnp.zeros(st_s.shape, F32)
        else:
            st_s[...] = s0_ref[0, 0, d]

        def body(ci, carry, d=d, mask=mask):
            c = ci if d == 0 else n_chunks - 1 - ci
            r0 = pl.multiple_of(c * CHUNK, CHUNK)
            bck = bc_s[pl.ds(r0, CHUNK), :]
            cck = cc_s[pl.ds(r0, CHUNK), :]
            dt = _softplus(g_ref[pl.ds(r0, CHUNK), :] + bias_row)
            cum = _cumsum_rows(dt * a_row, reverse=(d == 1))
            cum_t = cum.T
            tot = cum[CHUNK - 1:CHUNK, :] if d == 0 else cum[0:1, :]
            dte = dt * jnp.exp(tot - cum)
            ecum = jnp.exp(cum)
            cd = jnp.exp(tot)
            st = st_s[...]
            st_bf = st.astype(BF16)
            bbf = bck.astype(BF16)
            bt_bf = bck.T.astype(BF16)
            new_states = []
            for g2 in (0, 1):
                in_g = lane_lo if g2 == 0 else jnp.logical_not(lane_lo)
                cm = jnp.where(in_g, cck, 0.0).astype(BF16)
                gmat = _dot_nt(cm, bbf)
                yoff = _dot(cm, st_bf)
                xte_parts = []
                cd_row = jnp.zeros((1, 256), F32)
                for pr in (0, 1):
                    h0 = g2 * 4 + pr * 2
                    j0 = d * 8 + h0
                    j1 = j0 + 1
                    c0 = (g2 * 2 + pr) * LANES
                    xp = xc_s[pl.ds(r0, CHUNK), c0:c0 + LANES]
                    dec0 = jnp.exp(jnp.where(mask, cum[:, j0:j0 + 1] - cum_t[j0:j0 + 1, :], -jnp.inf))
                    dec1 = jnp.exp(jnp.where(mask, cum[:, j1:j1 + 1] - cum_t[j1:j1 + 1, :], -jnp.inf))
                    mcat = jnp.concatenate([(gmat * dec0).astype(BF16), (gmat * dec1).astype(BF16)], axis=1)
                    xdt = xp * _pair(lane_lo, dt[:, j0:j0 + 1], dt[:, j1:j1 + 1])
                    rhs = jnp.concatenate([jnp.where(lane_lo, xdt, 0.0), jnp.where(lane_lo, 0.0, xdt)],
                                          axis=0).astype(BF16)
                    yp = _dot(mcat, rhs) + (_pair(lane_lo, ecum[:, j0:j0 + 1], ecum[:, j1:j1 + 1])
                                            * yoff[:, pr * LANES:(pr + 1) * LANES])
                    if d == 0:
                        yacc_s[pl.ds(r0, CHUNK), c0:c0 + LANES] = yp + d_ref[0, :, c0:c0 + LANES] * xp
                    else:
                        y_ref[pl.ds(r0, CHUNK), c0:c0 + LANES] = (
                            yacc_s[pl.ds(r0, CHUNK), c0:c0 + LANES] + yp).astype(y_ref.dtype)
                    xte_parts.append((xp * _pair(lane_lo, dte[:, j0:j0 + 1], dte[:, j1:j1 + 1])).astype(BF16))
                    for hh, j in ((0, j0), (1, j1)):
                        lo = (pr * 2 + hh) * 64
                        cd_row = jnp.where((lane_row256 >= lo) & (lane_row256 < lo + 64), cd[:, j:j + 1], cd_row)
                xte = jnp.concatenate(xte_parts, axis=1)
                snew = _dot(bt_bf[g2 * 64:(g2 + 1) * 64, :], xte)
                new_states.append(cd_row * st[g2 * 64:(g2 + 1) * 64, :] + snew)
            st_s[...] = jnp.concatenate(new_states, axis=0)
            return carry

        lax.fori_loop(0, n_chunks, body, 0)
        if emit_state:
            sfin_ref[0, 0, d] = st_s[...]


def _ssd(proj, gates, par, dvec, cwx, cwb, cwc, s0, *, row0, n_seq, seq_len, emit_state):
    zero_init = s0 is None
    rb = row0 // seq_len
    in_specs = [
        pl.BlockSpec((seq_len, 512), lambda b, s: (rb + b, COL_X // 512 + s)),
        pl.BlockSpec((seq_len, LANES), lambda b, s: (rb + b, COL_B // LANES + s)),
        pl.BlockSpec((seq_len, LANES), lambda b, s: (rb + b, COL_C // LANES + s)),
        pl.BlockSpec((seq_len, LANES), lambda b, s: (rb + b, s)),
        pl.BlockSpec((1, 8, LANES), lambda b, s: (s, 0, 0)),
        pl.BlockSpec((1, 1, 512), lambda b, s: (s, 0, 0)),
        pl.BlockSpec((1, 8, 512), lambda b, s: (s, 0, 0)),
        pl.BlockSpec((1, 8, LANES), lambda b, s: (s, 0, 0)),
        pl.BlockSpec((1, 8, LANES), lambda b, s: (s, 0, 0)),
    ]
    args = [proj, proj, proj, gates, par, dvec, cwx, cwb, cwc]
    if not zero_init:
        in_specs.append(pl.BlockSpec((1, 1, 2, 128, 256), lambda b, s: (b, s, 0, 0, 0)))
        args.append(s0)
    out_specs = [pl.BlockSpec((seq_len, 512), lambda b, s: (b, s))]
    out_shape = [jax.ShapeDtypeStruct((n_seq * seq_len, 1024), BF16)]
    if emit_state:
        out_specs.append(pl.BlockSpec((1, 1, 2, 128, 256), lambda b, s: (b, s, 0, 0, 0)))
        out_shape.append(jax.ShapeDtypeStruct((n_seq, 2, 2, 128, 256), F32))
    res = pl.pallas_call(
        functools.partial(_ssd_kernel, seq_len=seq_len, zero_init=zero_init, emit_state=emit_state),
        grid=(n_seq, 2),
        in_specs=in_specs,
        out_specs=out_specs,
        out_shape=out_shape,
        scratch_shapes=[pltpu.VMEM((seq_len, 512), F32), pltpu.VMEM((seq_len, LANES), F32),
                        pltpu.VMEM((seq_len, LANES), F32), pltpu.VMEM((seq_len, 512), F32),
                        pltpu.VMEM((128, 256), F32)],
        compiler_params=_cparams(("parallel", "arbitrary")),
        name="ssd_scan",
    )(*args)
    return res if emit_state else (res[0], None)


def _mlstm_kernel(*refs, seq_len, zero_init, emit_state):
    it = iter(refs)
    q_ref, k_ref, v_ref, g_ref, par_ref, nw_ref = (next(it) for _ in range(6))
    s0_ref, m0_ref = (None, None) if zero_init else (next(it), next(it))
    hm_ref = next(it)
    sfin_ref, mfin_ref = (next(it), next(it)) if emit_state else (None, None)
    hacc_s, st_s, m_s = next(it), next(it), next(it)

    n_chunks = seq_len // CHUNK
    lane = lax.broadcasted_iota(jnp.int32, (CHUNK, LANES), 1)
    lane_lo = lane < 64
    trow = lax.broadcasted_iota(jnp.int32, (CHUNK, CHUNK), 0)
    scol = lax.broadcasted_iota(jnp.int32, (CHUNK, CHUNK), 1)
    ones_col = jnp.where(lane == 0, 1.0, 0.0).astype(BF16)
    bias_row = par_ref[0, 0:1, :]
    qscale = ML_DK ** -0.5

    for d in (0, 1):
        mask = (scol <= trow) if d == 0 else (scol >= trow)
        if zero_init:
            st_s[...] = jnp.zeros(st_s.shape, F32)
            m_s[...] = jnp.zeros(m_s.shape, F32)
        else:
            st_s[...] = s0_ref[0, 0, d]
            m_s[...] = m0_ref[0, 0]

        def body(ci, carry, d=d, mask=mask):
            c = ci if d == 0 else n_chunks - 1 - ci
            r0 = pl.multiple_of(c * CHUNK, CHUNK)
            gk = g_ref[pl.ds(r0, CHUNK), :] + bias_row
            logf = pltpu.roll(-_softplus(-gk), LANES - (LANE_F - LANE_I), 1)
            cum = _cumsum_rows(logf, reverse=(d == 1))
            tot = cum[CHUNK - 1:CHUNK, :] if d == 0 else cum[0:1, :]
            e = gk - cum
            e_t = e.T
            m_prev = m_s[...]
            mloc = jnp.max(e, axis=0, keepdims=True) + tot
            wend = jnp.exp(e + tot - mloc)
            m_new = jnp.maximum(tot + m_prev, mloc)
            a_old = jnp.exp(tot + m_prev - m_new)
            a_loc = jnp.exp(mloc - m_new)
            linter = cum + m_prev
            for pr in (0, 1):
                qp = q_ref[pl.ds(r0, CHUNK), pr * LANES:(pr + 1) * LANES]
                kp = k_ref[pl.ds(r0, CHUNK), pr * LANES:(pr + 1) * LANES]
                kt_bf = kp.astype(F32).T.astype(BF16)
                stp = st_s[pr]
                stp_bf = stp.astype(BF16)
                news = []
                for hh in (0, 1):
                    hl = pr * 2 + hh
                    j = LANE_I + d * 4 + hl
                    in_h = lane_lo if hh == 0 else jnp.logical_not(lane_lo)
                    qm = jnp.where(in_h, qp.astype(F32), 0.0).astype(BF16)
                    smat = _dot_nt(qm, kp) * qscale
                    lmat = jnp.where(mask, cum[:, j:j + 1] + e_t[j:j + 1, :], -jnp.inf)
                    mt = jnp.maximum(linter[:, j:j + 1], jnp.max(lmat, axis=1, keepdims=True))
                    sqk = (smat * jnp.exp(lmat - mt)).astype(BF16)
                    vh = v_ref[pl.ds(r0, CHUNK), hl * LANES:(hl + 1) * LANES]
                    vaug = jnp.concatenate([vh, ones_col], axis=1)
                    winter = jnp.exp(linter[:, j:j + 1] - mt) * qscale
                    nd = _dot(sqk, vaug) + winter * _dot(qm, stp_bf)
                    den = jnp.maximum(jnp.abs(nd[:, LANES:LANES + 1]), jnp.exp(-mt))
                    hout = nd[:, 0:LANES] / den
                    if d == 0:
                        hacc_s[pl.ds(r0, CHUNK), hl * LANES:(hl + 1) * LANES] = hout
                    else:
                        hsum = hacc_s[pl.ds(r0, CHUNK), hl * LANES:(hl + 1) * LANES] + hout
                        hm_ref[pl.ds(r0, CHUNK), hl * LANES:(hl + 1) * LANES] = _rms(
                            hsum, nw_ref[0, :, hl * LANES:(hl + 1) * LANES]).astype(hm_ref.dtype)
                    wv = (vaug.astype(F32) * wend[:, j:j + 1]).astype(BF16)
                    snew = _dot(kt_bf[hh * 64:(hh + 1) * 64, :], wv)
                    news.append(a_old[:, j:j + 1] * stp[hh * 64:(hh + 1) * 64, :] + a_loc[:, j:j + 1] * snew)
                st_s[pr] = jnp.concatenate(news, axis=0)
            m_s[...] = m_new
            return carry

        lax.fori_loop(0, n_chunks, body, 0)
        if emit_state:
            sfin_ref[0, 0, d] = st_s[...]
            mfin_ref[0, 0, d:d + 1, :] = m_s[...]


def _mlstm(proj, gates, par, normw, s0, m0, *, row0, n_seq, seq_len, emit_state):
    zero_init = s0 is None
    rb = row0 // seq_len
    in_specs = [
        pl.BlockSpec((seq_len, 256), lambda b, s: (rb + b, COL_Q // 256 + s)),
        pl.BlockSpec((seq_len, 256), lambda b, s: (rb + b, COL_K // 256 + s)),
        pl.BlockSpec((seq_len, 512), lambda b, s: (rb + b, COL_V // 512 + s)),
        pl.BlockSpec((seq_len, LANES), lambda b, s: (rb + b, s)),
        pl.BlockSpec((1, 8, LANES), lambda b, s: (s, 0, 0)),
        pl.BlockSpec((1, 1, 512), lambda b, s: (s, 0, 0)),
    ]
    args = [proj, proj, proj, gates, par, normw]
    if not zero_init:
        in_specs.append(pl.BlockSpec((1, 1, 2, 2, 128, 256), lambda b, s: (b, s, 0, 0, 0, 0)))
        in_specs.append(pl.BlockSpec((1, 1, 1, LANES), lambda b, s: (b, s, 0, 0)))
        args += [s0, m0]
    out_specs = [pl.BlockSpec((seq_len, 512), lambda b, s: (b, s))]
    out_shape = [jax.ShapeDtypeStruct((n_seq * seq_len, 1024), BF16)]
    if emit_state:
        out_specs.append(pl.BlockSpec((1, 1, 2, 2, 128, 256), lambda b, s: (b, s, 0, 0, 0, 0)))
        out_shape.append(jax.ShapeDtypeStruct((n_seq, 2, 2, 2, 128, 256), F32))
        out_specs.append(pl.BlockSpec((1, 1, 2, LANES), lambda b, s: (b, s, 0, 0)))
        out_shape.append(jax.ShapeDtypeStruct((n_seq, 2, 2, LANES), F32))
    res = pl.pallas_call(
        functools.partial(_mlstm_kernel, seq_len=seq_len, zero_init=zero_init, emit_state=emit_state),
        grid=(n_seq, 2),
        in_specs=in_specs,
        out_specs=out_specs,
        out_shape=out_shape,
        scratch_shapes=[pltpu.VMEM((seq_len, 512), F32), pltpu.VMEM((2, 128, 256), F32),
                        pltpu.VMEM((1, LANES), F32)],
        compiler_params=_cparams(("parallel", "arbitrary")),
        name="mlstm_scan",
    )(*args)
    return res if emit_state else (res[0], None, None)


def _outproj_kernel(y_ref, z_ref, hm_ref, o_ref, gs_ref, gm_ref, h_ref, wos_ref, wom_ref, wout_ref,
                    sn_ref, nf_ref, g1_ref, sc2_ref, sh2_ref, hout_ref, u2_ref):
    y = y_ref[...].astype(F32) * _silu(z_ref[...].astype(F32))
    y = _rms(y, sn_ref[...]).astype(BF16)
    y_ssd = _dot(y, wos_ref[...])
    hm = (hm_ref[...].astype(F32) * jax.nn.sigmoid(o_ref[...].astype(F32))).astype(BF16)
    y_ml = _dot(hm, wom_ref[...])
    mix = (jax.nn.sigmoid(gs_ref[...].astype(F32)) * y_ssd
           + jax.nn.sigmoid(gm_ref[...].astype(F32)) * y_ml).astype(BF16)
    h = h_ref[...] + g1_ref[0] * _dot(mix, wout_ref[...])
    hout_ref[...] = h
    u2_ref[...] = (_rms(h, nf_ref[...]) * (1.0 + sc2_ref[0]) + sh2_ref[0]).astype(u2_ref.dtype)


def _outproj(y_raw, hm, proj, h, wos, wom, wout, ssd_norm, norm_ffn, mod48, n_ctx_rows, lat_len, tm=512):
    t, d = h.shape
    mi = functools.partial(_mod_index, tm=tm, n_ctx_rows=n_ctx_rows, lat_len=lat_len)
    row = lambda i: (i, 0)
    const = lambda i: (0, 0)
    col = lambda k: (lambda i: (i, k))
    modspec = lambda which: pl.BlockSpec((1, 1, d), lambda i: (mi(i, which=which), 0, 0))
    return pl.pallas_call(
        _outproj_kernel,
        grid=(t // tm,),
        in_specs=[pl.BlockSpec((tm, d), row), pl.BlockSpec((tm, d), col(COL_Z // d)),
                  pl.BlockSpec((tm, d), row), pl.BlockSpec((tm, d), col(COL_O // d)),
                  pl.BlockSpec((tm, d), col(COL_GS // d)), pl.BlockSpec((tm, d), col(COL_GM // d)),
                  pl.BlockSpec((tm, d), row),
                  pl.BlockSpec((d, d), const), pl.BlockSpec((d, d), const), pl.BlockSpec((d, d), const),
                  pl.BlockSpec((1, d), const), pl.BlockSpec((1, d), const),
                  modspec(2), modspec(4), modspec(3)],
        out_specs=[pl.BlockSpec((tm, d), row), pl.BlockSpec((tm, d), row)],
        out_shape=[jax.ShapeDtypeStruct((t, d), F32), jax.ShapeDtypeStruct((t, d), BF16)],
        compiler_params=_cparams(("parallel",)),
        name="merge_outproj",
    )(y_raw, proj, hm, proj, proj, proj, h, wos, wom, wout, ssd_norm.reshape(1, d), norm_ffn.reshape(1, d),
      mod48, mod48, mod48)


def _ffn_kernel(g_ref, v_ref, cw_ref, wd_ref, h_ref, g2_ref, o_ref, *, n_ctx_tiles, ctx_len, n_grid_rows):
    i = pl.program_id(0)
    j = pl.program_id(1)
    tm = g_ref.shape[0]
    is_lat = i >= n_ctx_tiles
    lat_f = is_lat.astype(F32)
    g = g_ref[...].astype(F32)
    row = lax.broadcasted_iota(jnp.int32, (tm, 1), 0)
    width = jnp.where(is_lat, GRID_W, ctx_len)
    colpos = row & (width - 1)
    m_left = (colpos != 0).astype(F32)
    m_right = (colpos != width - 1).astype(F32)
    gl = pltpu.roll(g, 1, 0) * m_left
    gr = pltpu.roll(g, tm - 1, 0) * m_right

    def taps(r):
        return gl * cw_ref[3 * r:3 * r + 1, :] + g * cw_ref[3 * r + 1:3 * r + 2, :] + gr * cw_ref[3 * r + 2:3 * r + 3, :]

    m_top = (row >= GRID_W).astype(F32) * lat_f
    m_bot = (row < (n_grid_rows - 1) * GRID_W).astype(F32) * lat_f
    conv = (taps(1) + pltpu.roll(taps(0), GRID_W, 0) * m_top + pltpu.roll(taps(2), tm - GRID_W, 0) * m_bot
            + cw_ref[9:10, :])
    act = (_silu(conv) * v_ref[...].astype(F32)).astype(BF16)
    part = _dot(act, wd_ref[...])

    @pl.when(j == 0)
    def _():
        o_ref[...] = part

    @pl.when(j > 0)
    def _():
        o_ref[...] += part

    @pl.when(j == pl.num_programs(1) - 1)
    def _():
        o_ref[...] = h_ref[...] + g2_ref[0] * o_ref[...]


def _ffn(gv, cw16, wd, h, mod48, n_ctx_rows, ctx_len, lat_len, tf=256):
    t, d = h.shape
    tm = lat_len
    nj = D_FF // tf
    mi = functools.partial(_mod_index, tm=tm, n_ctx_rows=n_ctx_rows, lat_len=lat_len)
    return pl.pallas_call(
        functools.partial(_ffn_kernel, n_ctx_tiles=n_ctx_rows // tm, ctx_len=ctx_len,
                          n_grid_rows=lat_len // GRID_W),
        grid=(t // tm, nj),
        in_specs=[pl.BlockSpec((tm, tf), lambda i, j: (i, j)),
                  pl.BlockSpec((tm, tf), lambda i, j: (i, nj + j)),
                  pl.BlockSpec((16, tf), lambda i, j: (0, j)),
                  pl.BlockSpec((tf, d), lambda i, j: (j, 0)),
                  pl.BlockSpec((tm, d), lambda i, j: (i, 0)),
                  pl.BlockSpec((1, 1, d), lambda i, j: (mi(i, which=5), 0, 0))],
        out_specs=pl.BlockSpec((tm, d), lambda i, j: (i, 0)),
        out_shape=jax.ShapeDtypeStruct((t, d), F32),
        compiler_params=_cparams(("parallel", "arbitrary")),
        name="convffn_down",
    )(gv, gv, cw16, wd, h, mod48)


def _arrange_w_in(w):
    z, xbc, dt, q, k, v, o, ig, fg, gates = (
        w[:, 0:1024], w[:, 1024:2560], w[:, 2560:2592], w[:, 2592:3104], w[:, 3104:3616],
        w[:, 3616:4640], w[:, 4640:5664], w[:, 5664:5680], w[:, 5680:5696], w[:, 5696:7744])
    big = jnp.concatenate([z, o, gates, xbc, q, k, v], axis=1).astype(BF16)
    blocks = []
    for s in (0, 1):
        blocks += [dt[:, s * 8:s * 8 + 8], dt[:, 16 + s * 8:16 + s * 8 + 8],
                   ig[:, s * 4:s * 4 + 4], ig[:, 8 + s * 4:8 + s * 4 + 4],
                   fg[:, s * 4:s * 4 + 4], fg[:, 8 + s * 4:8 + s * 4 + 4],
                   jnp.zeros((w.shape[0], LANES - 32), w.dtype)]
    small = jnp.concatenate(blocks, axis=1).astype(BF16)
    return big, small


def _slab_rows(parts, n_rows=8):
    out = jnp.zeros((2, n_rows, LANES), F32)
    for r, lane0, val in parts:
        out = out.at[:, r, lane0:lane0 + val.shape[1]].set(val)
    return out


def _ssd_params(dt_bias, a_log, d_skip, conv_w, conv_b):
    par = _slab_rows([(0, 0, dt_bias[0].reshape(2, 8)), (0, 8, dt_bias[1].reshape(2, 8)),
                      (1, 0, a_log[0].reshape(2, 8)), (1, 8, a_log[1].reshape(2, 8))])
    dvec = jnp.repeat(d_skip, SSD_HEADDIM).reshape(2, 1, 512)
    cw = jnp.concatenate([conv_w, conv_b[None], jnp.zeros((4, conv_w.shape[1]), F32)], axis=0)
    cwx = cw[:, 0:1024].reshape(8, 2, 512).transpose(1, 0, 2)
    cwb = cw[:, 1024:1280].reshape(8, 2, LANES).transpose(1, 0, 2)
    cwc = cw[:, 1280:1536].reshape(8, 2, LANES).transpose(1, 0, 2)
    return par, dvec, cwx, cwb, cwc


def _ml_params(i_bias, f_bias, ml_norm):
    par = _slab_rows([(0, LANE_I, i_bias[0].reshape(2, 4)), (0, LANE_I + 4, i_bias[1].reshape(2, 4)),
                      (0, LANE_F, f_bias[0].reshape(2, 4)), (0, LANE_F + 4, f_bias[1].reshape(2, 4))])
    return par, ml_norm.reshape(2, 1, 512)


def _ssd_state_to_kernel(s):
    b = s.shape[0]
    s = s.reshape(b, 2, 2, 2, 4, 64, 64).transpose(0, 2, 1, 3, 6, 4, 5)
    return s.reshape(b, 2, 2, 128, 256)


def _ssd_state_from_kernel(s):
    b = s.shape[0]
    s = s.reshape(b, 2, 2, 2, 64, 4, 64).transpose(0, 2, 1, 3, 5, 6, 4)
    return s.reshape(b, 2, SSD_HEADS, SSD_HEADDIM, SSD_STATE)


def _ml_state_to_kernel(c0, n0, m0):
    b = c0.shape[0]
    aug = jnp.concatenate([jnp.swapaxes(c0, -1, -2), n0[..., None],
                           jnp.zeros(n0.shape + (LANES - 1,), F32)], axis=-1)
    aug = aug.reshape(b, 2, 2, 2, 2, 64, 256).transpose(0, 2, 1, 3, 4, 5, 6).reshape(b, 2, 2, 2, 128, 256)
    m = m0.reshape(b, 2, 2, 4).transpose(0, 2, 1, 3).reshape(b, 2, 1, 8)
    mk = jnp.zeros((b, 2, 1, LANES), F32).at[..., LANE_I:LANE_I + 8].set(m)
    return aug, mk


def _ml_state_from_kernel(st, mfin):
    b = st.shape[0]
    st = st.reshape(b, 2, 2, 2, 2, 64, 256).transpose(0, 2, 1, 3, 4, 5, 6).reshape(b, 2, ML_HEADS, 64, 256)
    c = jnp.swapaxes(st[..., 0:ML_DV], -1, -2)
    n = st[..., ML_DV]
    m = jnp.stack([mfin[:, :, 0, LANE_I:LANE_I + 4], mfin[:, :, 1, LANE_I + 4:LANE_I + 8]], axis=1)
    return c, n, m.reshape(b, 2, ML_HEADS)


def kernel(x_prompt, x_sample, state_ssd, state_mlstm_C, state_mlstm_n, state_mlstm_m, c, c_ctx, w_ada, b_ada,
           norm_mix, w_in, ssd_conv_w, ssd_conv_b, ssd_dt_bias, ssd_a_log, ssd_d, ssd_norm, w_o_ssd, ml_i_bias,
           ml_f_bias, ml_norm, w_o_ml, w_out, norm_ffn, w_up, ffn_conv_w, ffn_conv_b, w_down, final_norm):
    nb, ctx_len, d = x_prompt.shape
    nlat, lat_len, _ = x_sample.shape
    depth = w_in.shape[0]
    n_ctx_rows = nb * ctx_len
    n_lat_rows = nlat * lat_len

    h = jnp.concatenate([x_prompt.reshape(n_ctx_rows, d), x_sample.reshape(n_lat_rows, d)], axis=0)
    cond8 = jnp.concatenate([c_ctx[None], c, jnp.zeros((8 - 1 - nlat, d), F32)], axis=0)
    mod = _ada(cond8, w_ada, b_ada)

    ssd_l, c_l, n_l, m_l = [], [], [], []
    for l in range(depth):
        mod48 = mod[l].reshape(48, 1, d)
        w_big, w_small = _arrange_w_in(w_in[l])
        par_s, dvec, cwx, cwb, cwc = _ssd_params(ssd_dt_bias[l], ssd_a_log[l], ssd_d[l], ssd_conv_w[l],
                                                 ssd_conv_b[l])
        par_m, normw = _ml_params(ml_i_bias[l], ml_f_bias[l], ml_norm[l])

        u = _norm_mod(h, norm_mix[l], mod48, n_ctx_rows, lat_len, which_scale=1, which_shift=0)
        proj = _matmul(u, w_big, BF16, tm=2048, tn=512, name="in_proj")
        gates = _matmul(u, w_small, F32, tm=2048, tn=GATE_COLS, name="gate_proj")

        y_ctx, s_ctx = _ssd(proj, gates, par_s, dvec, cwx, cwb, cwc, None,
                            row0=0, n_seq=nb, seq_len=ctx_len, emit_state=True)
        y_lat, _ = _ssd(proj, gates, par_s, dvec, cwx, cwb, cwc, _ssd_state_to_kernel(state_ssd[:, l]),
                        row0=n_ctx_rows, n_seq=nlat, seq_len=lat_len, emit_state=False)
        hm_ctx, st_ctx, m_ctx = _mlstm(proj, gates, par_m, normw, None, None,
                                       row0=0, n_seq=nb, seq_len=ctx_len, emit_state=True)
        s0m, m0m = _ml_state_to_kernel(state_mlstm_C[:, l], state_mlstm_n[:, l], state_mlstm_m[:, l])
        hm_lat, _, _ = _mlstm(proj, gates, par_m, normw, s0m, m0m,
                              row0=n_ctx_rows, n_seq=nlat, seq_len=lat_len, emit_state=False)
        y_raw = jnp.concatenate([y_ctx, y_lat], axis=0)
        hm = jnp.concatenate([hm_ctx, hm_lat], axis=0)

        h, u2 = _outproj(y_raw, hm, proj, h, w_o_ssd[l].astype(BF16), w_o_ml[l].astype(BF16),
                         w_out[l].astype(BF16), ssd_norm[l], norm_ffn[l], mod48, n_ctx_rows, lat_len)

        gv = _matmul(u2, w_up[l].astype(BF16), BF16, tm=2048, tn=512, name="up_proj")
        cw16 = jnp.concatenate([ffn_conv_w[l].reshape(9, D_FF), ffn_conv_b[l][None],
                                jnp.zeros((6, D_FF), F32)], axis=0)
        h = _ffn(gv, cw16, w_down[l].astype(BF16), h, mod48, n_ctx_rows, ctx_len, lat_len)

        ssd_l.append(_ssd_state_from_kernel(s_ctx))
        c_new, n_new, m_new = _ml_state_from_kernel(st_ctx, m_ctx)
        c_l.append(c_new)
        n_l.append(n_new)
        m_l.append(m_new)

    y_prompt = _final_norm(h, final_norm, 0, n_ctx_rows).reshape(nb, ctx_len, d)
    y_sample = _final_norm(h, final_norm, n_ctx_rows, n_lat_rows).reshape(nlat, lat_len, d)
    return (y_prompt, y_sample, jnp.stack(ssd_l, axis=1), jnp.stack(c_l, axis=1), jnp.stack(n_l, axis=1),
            jnp.stack(m_l, axis=1))
```

```python
import functools

import jax
import jax.numpy as jnp
import numpy as np
from jax import lax
from jax.experimental import pallas as pl
from jax.experimental.pallas import tpu as pltpu

F32 = jnp.float32
BF16 = jnp.bfloat16

D_MODEL = 1024
CHUNK = 128
EPS = 1e-6
GRID_W = 64
SSD_HEADS = 16
SSD_HEADDIM = 64
SSD_STATE = 64
SSD_GROUPS = 4
ML_HEADS = 8
ML_DV = 128
ML_DK = 64
D_FF = 2816

LANES = 128
BF16_SUBLANES = 16
VMEM_LIMIT = 56 * 1024 * 1024

COL_Z, COL_O, COL_GS, COL_GM = 0, 1024, 2048, 3072
COL_X, COL_B, COL_C = 4096, 5120, 5376
COL_Q, COL_K, COL_V = 5632, 6144, 6656
PROJ_COLS = 7680
GATE_COLS = 256
LANE_I = 16
LANE_F = 24
GROWS = 16


def _cparams(sem):
    return pltpu.CompilerParams(dimension_semantics=sem, vmem_limit_bytes=VMEM_LIMIT)


def _silu(x):
    return x * jax.nn.sigmoid(x)


def _softplus(x):
    return jnp.maximum(x, 0.0) + jnp.log1p(jnp.exp(-jnp.abs(x)))


def _dot(a, b):
    return jnp.dot(a, b, preferred_element_type=F32)


def _dot_nt(a, b):
    return lax.dot_general(a, b, (((1,), (1,)), ((), ())), preferred_element_type=F32)


def _pair(lane_lo, col0, col1):
    return jnp.where(lane_lo, col0, col1)


def _tri2():
    s = lax.broadcasted_iota(jnp.int32, (CHUNK, 2 * CHUNK), 0)
    t = lax.broadcasted_iota(jnp.int32, (CHUNK, 2 * CHUNK), 1)
    keep = ((t < CHUNK) & (s <= t)) | ((t >= CHUNK) & (s >= t - CHUNK))
    return jnp.where(keep, 1.0, 0.0).astype(BF16)


def _lane_cumsums(x, tri2):
    hi = x.astype(BF16)
    r1 = x - hi.astype(F32)
    mid = r1.astype(BF16)
    lo = (r1 - mid.astype(F32)).astype(BF16)
    cs = _dot(hi, tri2) + _dot(mid, tri2) + _dot(lo, tri2)
    return cs[:, 0:CHUNK], cs[:, CHUNK:2 * CHUNK]


def _pad_rows_t(x):
    return jnp.concatenate([x, jnp.zeros((CHUNK - x.shape[0], LANES), F32)], axis=0).T


def _ada_kernel(c_ref, w_ref, b_ref, o_ref):
    cond = _silu(c_ref[...]).astype(BF16)
    o_ref[0] = _dot(cond, w_ref[0].astype(BF16)) + b_ref[0]


def _ada(cond8, w_ada, b_ada):
    depth, d, n = w_ada.shape
    tn = 512
    return pl.pallas_call(
        _ada_kernel,
        grid=(depth, n // tn),
        in_specs=[pl.BlockSpec((8, d), lambda l, j: (0, 0)),
                  pl.BlockSpec((1, d, tn), lambda l, j: (l, 0, j)),
                  pl.BlockSpec((1, 1, tn), lambda l, j: (l, 0, j))],
        out_specs=pl.BlockSpec((1, 8, tn), lambda l, j: (l, 0, j)),
        out_shape=jax.ShapeDtypeStruct((depth, 8, n), F32),
        compiler_params=_cparams(("arbitrary", "arbitrary")),
        name="ada_mod",
    )(cond8, w_ada, b_ada.reshape(depth, 1, n))


def _mod_index(i, tm, n_ctx_rows, lat_len, which):
    n_ctx_tiles = n_ctx_rows // tm
    tiles_per_seq = lat_len // tm
    row = jnp.where(i < n_ctx_tiles, 0, 1 + (i - n_ctx_tiles) // tiles_per_seq)
    return row * 6 + which


def _rms(x, w):
    return (x * lax.rsqrt(jnp.mean(x * x, axis=-1, keepdims=True) + EPS)) * w


def _norm_mod_kernel(x_ref, w_ref, sc_ref, sh_ref, o_ref):
    y = _rms(x_ref[...], w_ref[...])
    o_ref[...] = (y * (1.0 + sc_ref[0]) + sh_ref[0]).astype(o_ref.dtype)


def _norm_mod(h, w, mod48, n_ctx_rows, lat_len, which_scale, which_shift, tm=512):
    t, d = h.shape
    mi = functools.partial(_mod_index, tm=tm, n_ctx_rows=n_ctx_rows, lat_len=lat_len)
    return pl.pallas_call(
        _norm_mod_kernel,
        grid=(t // tm,),
        in_specs=[pl.BlockSpec((tm, d), lambda i: (i, 0)),
                  pl.BlockSpec((1, d), lambda i: (0, 0)),
                  pl.BlockSpec((1, 1, d), lambda i: (mi(i, which=which_scale), 0, 0)),
                  pl.BlockSpec((1, 1, d), lambda i: (mi(i, which=which_shift), 0, 0))],
        out_specs=pl.BlockSpec((tm, d), lambda i: (i, 0)),
        out_shape=jax.ShapeDtypeStruct((t, d), BF16),
        compiler_params=_cparams(("parallel",)),
        name="norm_mod",
    )(h, w.reshape(1, d), mod48, mod48)


def _norm_kernel(x_ref, w_ref, o_ref):
    o_ref[...] = _rms(x_ref[...], w_ref[...])


def _final_norm(h, w, row0, rows, tm=512):
    d = h.shape[1]
    off = row0 // tm
    return pl.pallas_call(
        _norm_kernel,
        grid=(rows // tm,),
        in_specs=[pl.BlockSpec((tm, d), lambda i: (i + off, 0)),
                  pl.BlockSpec((1, d), lambda i: (0, 0))],
        out_specs=pl.BlockSpec((tm, d), lambda i: (i, 0)),
        out_shape=jax.ShapeDtypeStruct((rows, d), F32),
        compiler_params=_cparams(("parallel",)),
        name="final_norm",
    )(h, w.reshape(1, d))


def _matmul_kernel(x_ref, w_ref, o_ref):
    o_ref[...] = _dot(x_ref[...], w_ref[...]).astype(o_ref.dtype)


def _matmul(x, w, out_dtype, tm, tn, name):
    t, k = x.shape
    n = w.shape[1]
    return pl.pallas_call(
        _matmul_kernel,
        grid=(t // tm, n // tn),
        in_specs=[pl.BlockSpec((tm, k), lambda i, j: (i, 0)),
                  pl.BlockSpec((k, tn), lambda i, j: (0, j))],
        out_specs=pl.BlockSpec((tm, tn), lambda i, j: (i, j)),
        out_shape=jax.ShapeDtypeStruct((t, n), out_dtype),
        compiler_params=_cparams(("parallel", "arbitrary")),
        name=name,
    )(x, w)


def _conv_silu_chunk(src_ref, cw_ref, c, n_chunks, seq_len):
    r0 = pl.multiple_of(c * CHUNK, CHUNK)
    cur = src_ref[pl.ds(r0, CHUNK), :].astype(F32)
    p0 = pl.multiple_of(jnp.maximum(r0 - BF16_SUBLANES, 0), BF16_SUBLANES)
    n0 = pl.multiple_of(jnp.minimum(r0 + CHUNK, seq_len - BF16_SUBLANES), BF16_SUBLANES)
    prv = src_ref[pl.ds(p0, BF16_SUBLANES), :].astype(F32)[BF16_SUBLANES - 1:BF16_SUBLANES, :]
    nxt = src_ref[pl.ds(n0, BF16_SUBLANES), :].astype(F32)[0:1, :]
    prv = prv * (c > 0).astype(F32)
    nxt = nxt * (c < n_chunks - 1).astype(F32)
    row = lax.broadcasted_iota(jnp.int32, cur.shape, 0)
    up = jnp.where(row == 0, prv, pltpu.roll(cur, 1, 0))
    dn = jnp.where(row == CHUNK - 1, nxt, pltpu.roll(cur, CHUNK - 1, 0))
    y = up * cw_ref[0, 0:1, :] + cur * cw_ref[0, 1:2, :] + dn * cw_ref[0, 2:3, :] + cw_ref[0, 3:4, :]
    return _silu(y)


def _ssd_kernel(*refs, seq_len, zero_init, emit_state):
    it = iter(refs)
    x_ref, b_ref, c_ref, g_ref = next(it), next(it), next(it), next(it)
    par_ref, d_ref, cwx_ref, cwb_ref, cwc_ref = next(it), next(it), next(it), next(it), next(it)
    s0_ref = None if zero_init else next(it)
    y_ref = next(it)
    sfin_ref = next(it) if emit_state else None
    (xbd_s, cm_s, g_s, bt_s, yacc_s, cumc_s, at_s, dtt_s, cumt_s, dtet_s, cdt_s, st_s) = (
        next(it) for _ in range(12))

    n_chunks = seq_len // CHUNK
    lane = lax.broadcasted_iota(jnp.int32, (CHUNK, LANES), 1)
    lane_lo = lane < 64
    lane_row_lo = lax.broadcasted_iota(jnp.int32, (1, LANES), 1) < 64
    trow = lax.broadcasted_iota(jnp.int32, (CHUNK, CHUNK), 0)
    scol = lax.broadcasted_iota(jnp.int32, (CHUNK, CHUNK), 1)
    bias_row = par_ref[0, 0:1, :]
    a_row = -jnp.exp(par_ref[0, 1:2, :])

    def phase_a(c, carry):
        r0 = pl.multiple_of(c * CHUNK, CHUNK)
        xck = _conv_silu_chunk(x_ref, cwx_ref, c, n_chunks, seq_len)
        bck = _conv_silu_chunk(b_ref, cwb_ref, c, n_chunks, seq_len)
        cck = _conv_silu_chunk(c_ref, cwc_ref, c, n_chunks, seq_len)
        yacc_s[pl.ds(r0, CHUNK), :] = d_ref[0] * xck
        for p in range(4):
            xp = xck[:, p * LANES:(p + 1) * LANES]
            xbd_s[c * 4 + p] = jnp.concatenate(
                [jnp.where(lane_lo, xp, 0.0), jnp.where(lane_lo, 0.0, xp)], axis=0).astype(BF16)
        bbf = bck.astype(BF16)
        for g2 in (0, 1):
            in_g = lane_lo if g2 == 0 else jnp.logical_not(lane_lo)
            cm = jnp.where(in_g, cck, 0.0).astype(BF16)
            cm_s[c * 2 + g2] = cm
            g_s[c * 2 + g2] = _dot_nt(cm, bbf)
        bt_s[c] = bck.T
        dt = _softplus(g_ref[pl.ds(r0, CHUNK), :] + bias_row)
        g0 = pl.multiple_of(c * GROWS, GROWS)
        at_s[pl.ds(g0, GROWS), :] = (dt * a_row).T[0:GROWS, :]
        dtt_s[pl.ds(g0, GROWS), :] = dt.T[0:GROWS, :]
        return carry

    lax.fori_loop(0, n_chunks, phase_a, 0)

    cf, cr = _lane_cumsums(at_s[...], _tri2())
    fwd_row = (lax.broadcasted_iota(jnp.int32, (n_chunks * GROWS, 1), 0) & (GROWS - 1)) < 8
    cum_t = jnp.where(fwd_row, cf, cr)
    tot_t = jnp.where(fwd_row, cum_t[:, CHUNK - 1:CHUNK], cum_t[:, 0:1])
    cumt_s[...] = cum_t
    dtet_s[...] = dtt_s[...] * jnp.exp(tot_t - cum_t)
    cdt_s[...] = jnp.broadcast_to(jnp.exp(tot_t), cum_t.shape)

    def phase_c(c, carry):
        g0 = pl.multiple_of(c * GROWS, GROWS)
        cumc_s[c] = _pad_rows_t(cumt_s[pl.ds(g0, GROWS), :])
        return carry

    lax.fori_loop(0, n_chunks, phase_c, 0)

    for d in (0, 1):
        mask = (scol <= trow) if d == 0 else (scol >= trow)
        if zero_init:
            st_s[...] = jnp.zeros(st_s.shape, F32)
        else:
            st_s[...] = s0_ref[0, 0, d]

        def body(ci, carry, d=d, mask=mask):
            c = ci if d == 0 else n_chunks - 1 - ci
            r0 = pl.multiple_of(c * CHUNK, CHUNK)
            cumc = cumc_s[c]
            st = st_s[...]
            st_bf = st.astype(BF16)

            def grow(ref, j):
                return ref[pl.ds(c * GROWS + j, 1), :]

            for g2 in (0, 1):
                yoff = _dot(cm_s[c * 2 + g2], st_bf)
                gmat = g_s[c * 2 + g2]
                bt_g = bt_s[c, g2 * 64:(g2 + 1) * 64, :]
                for pr in (0, 1):
                    p = g2 * 2 + pr
                    j0 = d * 8 + g2 * 4 + pr * 2
                    j1 = j0 + 1
                    cb0 = jnp.broadcast_to(cumc[:, j0:j0 + 1], (CHUNK, LANES))
                    cb1 = jnp.broadcast_to(cumc[:, j1:j1 + 1], (CHUNK, LANES))
                    m0 = gmat * jnp.exp(jnp.where(mask, cb0 - grow(cumt_s, j0), -jnp.inf))
                    m1 = gmat * jnp.exp(jnp.where(mask, cb1 - grow(cumt_s, j1), -jnp.inf))
                    mcat = jnp.concatenate([(m0 * grow(dtt_s, j0)).astype(BF16),
                                            (m1 * grow(dtt_s, j1)).astype(BF16)], axis=1)
                    xbd = xbd_s[c * 4 + p]
                    yp = _dot(mcat, xbd) + (jnp.exp(jnp.where(lane_lo, cb0, cb1))
                                            * yoff[:, pr * LANES:(pr + 1) * LANES])
                    c0 = p * LANES
                    if d == 0:
                        yacc_s[pl.ds(r0, CHUNK), c0:c0 + LANES] += yp
                    else:
                        y_ref[pl.ds(r0, CHUNK), c0:c0 + LANES] = (
                            yacc_s[pl.ds(r0, CHUNK), c0:c0 + LANES] + yp).astype(y_ref.dtype)
                    lhs = jnp.concatenate([(bt_g * grow(dtet_s, j0)).astype(BF16),
                                           (bt_g * grow(dtet_s, j1)).astype(BF16)], axis=1)
                    cd = jnp.where(lane_row_lo, grow(cdt_s, j0), grow(cdt_s, j1))
                    st_s[g2 * 64:(g2 + 1) * 64, pr * LANES:(pr + 1) * LANES] = (
                        cd * st[g2 * 64:(g2 + 1) * 64, pr * LANES:(pr + 1) * LANES] + _dot(lhs, xbd))
            return carry

        lax.fori_loop(0, n_chunks, body, 0)
        if emit_state:
            sfin_ref[0, 0, d] = st_s[...]


def _ssd(proj, gates, par, dvec, cwx, cwb, cwc, s0, *, row0, n_seq, seq_len, emit_state):
    zero_init = s0 is None
    rb = row0 // seq_len
    nc = seq_len // CHUNK
    in_specs = [
        pl.BlockSpec((seq_len, 512), lambda b, s: (rb + b, COL_X // 512 + s)),
        pl.BlockSpec((seq_len, LANES), lambda b, s: (rb + b, COL_B // LANES + s)),
        pl.BlockSpec((seq_len, LANES), lambda b, s: (rb + b, COL_C // LANES + s)),
        pl.BlockSpec((seq_len, LANES), lambda b, s: (rb + b, s)),
        pl.BlockSpec((1, 8, LANES), lambda b, s: (s, 0, 0)),
        pl.BlockSpec((1, 1, 512), lambda b, s: (s, 0, 0)),
        pl.BlockSpec((1, 8, 512), lambda b, s: (s, 0, 0)),
        pl.BlockSpec((1, 8, LANES), lambda b, s: (s, 0, 0)),
        pl.BlockSpec((1, 8, LANES), lambda b, s: (s, 0, 0)),
    ]
    args = [proj, proj, proj, gates, par, dvec, cwx, cwb, cwc]
    if not zero_init:
        in_specs.append(pl.BlockSpec((1, 1, 2, 128, 256), lambda b, s: (b, s, 0, 0, 0)))
        args.append(s0)
    out_specs = [pl.BlockSpec((seq_len, 512), lambda b, s: (b, s))]
    out_shape = [jax.ShapeDtypeStruct((n_seq * seq_len, 1024), BF16)]
    if emit_state:
        out_specs.append(pl.BlockSpec((1, 1, 2, 128, 256), lambda b, s: (b, s, 0, 0, 0)))
        out_shape.append(jax.ShapeDtypeStruct((n_seq, 2, 2, 128, 256), F32))
    gate_rows = pltpu.VMEM((nc * GROWS, LANES), F32)
    res = pl.pallas_call(
        functools.partial(_ssd_kernel, seq_len=seq_len, zero_init=zero_init, emit_state=emit_state),
        grid=(n_seq, 2),
        in_specs=in_specs,
        out_specs=out_specs,
        out_shape=out_shape,
        scratch_shapes=[pltpu.VMEM((nc * 4, 2 * CHUNK, LANES), BF16),
                        pltpu.VMEM((nc * 2, CHUNK, LANES), BF16),
                        pltpu.VMEM((nc * 2, CHUNK, CHUNK), F32),
                        pltpu.VMEM((nc, LANES, CHUNK), F32),
                        pltpu.VMEM((seq_len, 512), F32),
                        pltpu.VMEM((nc, CHUNK, LANES), F32),
                        gate_rows, gate_rows, gate_rows, gate_rows, gate_rows,
                        pltpu.VMEM((128, 256), F32)],
        compiler_params=_cparams(("parallel", "arbitrary")),
        name="ssd_scan",
    )(*args)
    return res if emit_state else (res[0], None)


def _mlstm_kernel(*refs, seq_len, zero_init, emit_state):
    it = iter(refs)
    q_ref, k_ref, v_ref, g_ref, par_ref, nw_ref = (next(it) for _ in range(6))
    s0_ref, m0_ref = (None, None) if zero_init else (next(it), next(it))
    hm_ref = next(it)
    sfin_ref, mfin_ref = (next(it), next(it)) if emit_state else (None, None)
    (qm_s, s_s, kt_s, hacc_s, gt_s, cumt_s, et_s, wendt_s, pmaxt_s, mloct_s, cumc_s, pmaxc_s, st_s, m_s) = (
        next(it) for _ in range(14))

    n_chunks = seq_len // CHUNK
    n_rows = n_chunks * GROWS
    lane = lax.broadcasted_iota(jnp.int32, (CHUNK, LANES), 1)
    lane_lo = lane < 64
    trow = lax.broadcasted_iota(jnp.int32, (CHUNK, CHUNK), 0)
    scol = lax.broadcasted_iota(jnp.int32, (CHUNK, CHUNK), 1)
    ones_blk = jnp.ones((CHUNK, LANES), BF16)
    bias_row = par_ref[0, 0:1, :]
    qscale = ML_DK ** -0.5

    def phase_a(c, carry):
        r0 = pl.multiple_of(c * CHUNK, CHUNK)
        for pr in (0, 1):
            qp = q_ref[pl.ds(r0, CHUNK), pr * LANES:(pr + 1) * LANES].astype(F32)
            kp = k_ref[pl.ds(r0, CHUNK), pr * LANES:(pr + 1) * LANES]
            kt_s[c * 2 + pr] = kp.astype(F32).T
            for hh in (0, 1):
                in_h = lane_lo if hh == 0 else jnp.logical_not(lane_lo)
                qm = jnp.where(in_h, qp, 0.0).astype(BF16)
                qm_s[c * 4 + pr * 2 + hh] = qm
                s_s[c * 4 + pr * 2 + hh] = _dot_nt(qm, kp) * qscale
        gk = g_ref[pl.ds(r0, CHUNK), :] + bias_row
        comb = jnp.where(lane < LANE_F, gk, -_softplus(-gk))
        g0 = pl.multiple_of(c * GROWS, GROWS)
        gt_s[pl.ds(g0, GROWS), :] = comb.T[LANE_I:LANE_I + GROWS, :]
        return carry

    lax.fori_loop(0, n_chunks, phase_a, 0)

    g_all = gt_s[...]
    cf, cr = _lane_cumsums(g_all, _tri2())
    rr = lax.broadcasted_iota(jnp.int32, (n_rows, 1), 0) & (GROWS - 1)
    fwd_row = rr < 12
    cum_t = jnp.where(fwd_row, cf, cr)
    tot_t = jnp.where(fwd_row, cum_t[:, CHUNK - 1:CHUNK], cum_t[:, 0:1])
    e_t = pltpu.roll(g_all, 8, 0) - cum_t
    mloc_t = jnp.max(e_t, axis=1, keepdims=True) + tot_t
    lane_b = lax.broadcasted_iota(jnp.int32, (n_rows, LANES), 1)
    pf, pb = e_t, e_t
    k = 1
    while k < CHUNK:
        pf = jnp.maximum(pf, jnp.where(lane_b >= k, pltpu.roll(pf, k, 1), -jnp.inf))
        pb = jnp.maximum(pb, jnp.where(lane_b < CHUNK - k, pltpu.roll(pb, CHUNK - k, 1), -jnp.inf))
        k *= 2
    cumt_s[...] = cum_t
    et_s[...] = e_t
    wendt_s[...] = jnp.exp(e_t + tot_t - mloc_t)
    pmaxt_s[...] = jnp.where(fwd_row, pf, pb)
    mloct_s[...] = jnp.broadcast_to(mloc_t, e_t.shape)

    def phase_c(c, carry):
        g8 = pl.multiple_of(c * GROWS + 8, 8)
        cumc_s[c] = _pad_rows_t(jnp.concatenate([cumt_s[pl.ds(g8, 8), :], mloct_s[pl.ds(g8, 8), :]], axis=0))
        pmaxc_s[c] = _pad_rows_t(pmaxt_s[pl.ds(g8, 8), :])
        return carry

    lax.fori_loop(0, n_chunks, phase_c, 0)

    for d in (0, 1):
        mask = (scol <= trow) if d == 0 else (scol >= trow)
        if zero_init:
            st_s[...] = jnp.zeros(st_s.shape, F32)
            m_s[...] = jnp.zeros(m_s.shape, F32)
        else:
            st_s[...] = s0_ref[0, 0, d]
            m_s[...] = m0_ref[0, 0]

        def body(ci, carry, d=d, mask=mask):
            c = ci if d == 0 else n_chunks - 1 - ci
            r0 = pl.multiple_of(c * CHUNK, CHUNK)
            cumc = cumc_s[c]
            m_prev = m_s[...]
            mx = jnp.maximum(m_prev, pmaxc_s[c])
            negmt = -(cumc + mx)
            tot =cumc[CHUNK - 1:CHUNK, :] if d == 0 else cumc[0:1, :]
            mloc = pltpu.roll(cumc[0:1, :], LANES - 8, 1)
            m_new = jnp.maximum(tot + m_prev, mloc)
            a_old = jnp.exp(tot + m_prev - m_new)
            a_loc = jnp.exp(mloc - m_new)
            for pr in (0, 1):
                stp = st_s[pr]
                stp_bf = stp.astype(BF16)
                kt = kt_s[c * 2 + pr]
                for hh in (0, 1):
                    hl = pr * 2 + hh
                    j = d * 4 + hl
                    c0 = hl * LANES
                    e_row = et_s[pl.ds(c * GROWS + 8 + j, 1), :]
                    wend_row = wendt_s[pl.ds(c * GROWS + 8 + j, 1), :]
                    mxb = jnp.broadcast_to(mx[:, j:j + 1], (CHUNK, LANES))
                    sqk = (s_s[c * 4 + hl] * jnp.exp(jnp.where(mask, e_row - mxb, -jnp.inf))).astype(BF16)
                    winter = jnp.exp(m_prev[:, j:j + 1] - mxb) * qscale
                    qw = (qm_s[c * 4 + hl].astype(F32) * winter).astype(BF16)
                    vaug = jnp.concatenate([v_ref[pl.ds(r0, CHUNK), c0:c0 + LANES], ones_blk], axis=1)
                    nd = _dot(jnp.concatenate([sqk, qw], axis=1), jnp.concatenate([vaug, stp_bf], axis=0))
                    emt = jnp.exp(jnp.broadcast_to(negmt[:, j:j + 1], (CHUNK, LANES)))
                    den = jnp.maximum(jnp.abs(nd[:, LANES:2 * LANES]), emt)
                    hout = nd[:, 0:LANES] / den
                    if d == 0:
                        hacc_s[pl.ds(r0, CHUNK), c0:c0 + LANES] = hout
                    else:
                        hsum = hacc_s[pl.ds(r0, CHUNK), c0:c0 + LANES] + hout
                        hm_ref[pl.ds(r0, CHUNK), c0:c0 + LANES] = _rms(
                            hsum, nw_ref[0, :, c0:c0 + LANES]).astype(hm_ref.dtype)
                    ktw = (kt[hh * 64:(hh + 1) * 64, :] * wend_row).astype(BF16)
                    st_s[pr, hh * 64:(hh + 1) * 64, :] = (a_old[:, j:j + 1] * stp[hh * 64:(hh + 1) * 64, :]
                                                          + a_loc[:, j:j + 1] * _dot(ktw, vaug))
            m_s[...] = m_new
            return carry

        lax.fori_loop(0, n_chunks, body, 0)
        if emit_state:
            sfin_ref[0, 0, d] = st_s[...]
            mfin_ref[0, 0, d:d + 1, :] = m_s[...]


def _mlstm(proj, gates, par, normw, s0, m0, *, row0, n_seq, seq_len, emit_state):
    zero_init = s0 is None
    rb = row0 // seq_len
    nc = seq_len // CHUNK
    in_specs = [
        pl.BlockSpec((seq_len, 256), lambda b, s: (rb + b, COL_Q // 256 + s)),
        pl.BlockSpec((seq_len, 256), lambda b, s: (rb + b, COL_K // 256 + s)),
        pl.BlockSpec((seq_len, 512), lambda b, s: (rb + b, COL_V // 512 + s)),
        pl.BlockSpec((seq_len, LANES), lambda b, s: (rb + b, s)),
        pl.BlockSpec((1, 8, LANES), lambda b, s: (s, 0, 0)),
        pl.BlockSpec((1, 1, 512), lambda b, s: (s, 0, 0)),
    ]
    args = [proj, proj, proj, gates, par, normw]
    if not zero_init:
        in_specs.append(pl.BlockSpec((1, 1, 2, 2, 128, 256), lambda b, s: (b, s, 0, 0, 0, 0)))
        in_specs.append(pl.BlockSpec((1, 1, 1, LANES), lambda b, s: (b, s, 0, 0)))
        args += [s0, m0]
    out_specs = [pl.BlockSpec((seq_len, 512), lambda b, s: (b, s))]
    out_shape = [jax.ShapeDtypeStruct((n_seq * seq_len, 1024), BF16)]
    if emit_state:
        out_specs.append(pl.BlockSpec((1, 1, 2, 2, 128, 256), lambda b, s: (b, s, 0, 0, 0, 0)))
        out_shape.append(jax.ShapeDtypeStruct((n_seq, 2, 2, 2, 128, 256), F32))
        out_specs.append(pl.BlockSpec((1, 1, 2, LANES), lambda b, s: (b, s, 0, 0)))
        out_shape.append(jax.ShapeDtypeStruct((n_seq, 2, 2, LANES), F32))
    gate_rows = pltpu.VMEM((nc * GROWS, LANES), F32)
    res = pl.pallas_call(
        functools.partial(_mlstm_kernel, seq_len=seq_len, zero_init=zero_init, emit_state=emit_state),
        grid=(n_seq, 2),
        in_specs=in_specs,
        out_specs=out_specs,
        out_shape=out_shape,
        scratch_shapes=[pltpu.VMEM((nc * 4, CHUNK, LANES), BF16),
                        pltpu.VMEM((nc * 4, CHUNK, CHUNK), F32),
                        pltpu.VMEM((nc * 2, LANES, CHUNK), F32),
                        pltpu.VMEM((seq_len, 512), F32),
                        gate_rows, gate_rows, gate_rows, gate_rows, gate_rows, gate_rows,
                        pltpu.VMEM((nc, CHUNK, LANES), F32),
                        pltpu.VMEM((nc, CHUNK, LANES), F32),
                        pltpu.VMEM((2, 128, 256), F32),
                        pltpu.VMEM((1, LANES), F32)],
        compiler_params=_cparams(("parallel", "arbitrary")),
        name="mlstm_scan",
    )(*args)
    return res if emit_state else (res[0], None, None)


def _outproj_kernel(yc_ref, yl_ref, z_ref, hc_ref, hl_ref, o_ref, gs_ref, gm_ref, h_ref, wos_ref, wom_ref,
                    wout_ref, sn_ref, nf_ref, g1_ref, sc2_ref, sh2_ref, hout_ref, u2_ref, *, n_ctx_tiles):
    is_ctx = pl.program_id(0) < n_ctx_tiles
    y_in = jnp.where(is_ctx, yc_ref[...].astype(F32), yl_ref[...].astype(F32))
    hm_in = jnp.where(is_ctx, hc_ref[...].astype(F32), hl_ref[...].astype(F32))
    y = y_in * _silu(z_ref[...].astype(F32))
    y = _rms(y, sn_ref[...]).astype(BF16)
    y_ssd = _dot(y, wos_ref[...])
    hm = (hm_in * jax.nn.sigmoid(o_ref[...].astype(F32))).astype(BF16)
    y_ml = _dot(hm, wom_ref[...])
    mix = (jax.nn.sigmoid(gs_ref[...].astype(F32)) * y_ssd
           + jax.nn.sigmoid(gm_ref[...].astype(F32)) * y_ml).astype(BF16)
    h = h_ref[...] + g1_ref[0] * _dot(mix, wout_ref[...])
    hout_ref[...] = h
    u2_ref[...] = (_rms(h, nf_ref[...]) * (1.0 + sc2_ref[0]) + sh2_ref[0]).astype(u2_ref.dtype)


def _outproj(y_ctx, y_lat, hm_ctx, hm_lat, proj, h, wos, wom, wout, ssd_norm, norm_ffn, mod48, n_ctx_rows,
             lat_len, tm=512):
    t, d = h.shape
    nct = n_ctx_rows // tm
    mi = functools.partial(_mod_index, tm=tm, n_ctx_rows=n_ctx_rows, lat_len=lat_len)
    row = lambda i: (i, 0)
    ctx_row = lambda i: (jnp.minimum(i, nct - 1), 0)
    lat_row = lambda i: (jnp.maximum(i - nct, 0), 0)
    const = lambda i: (0, 0)
    col = lambda k: (lambda i: (i, k))
    modspec = lambda which: pl.BlockSpec((1, 1, d), lambda i: (mi(i, which=which), 0, 0))
    return pl.pallas_call(
        functools.partial(_outproj_kernel, n_ctx_tiles=nct),
        grid=(t // tm,),
        in_specs=[pl.BlockSpec((tm, d), ctx_row), pl.BlockSpec((tm, d), lat_row),
                  pl.BlockSpec((tm, d), col(COL_Z // d)),
                  pl.BlockSpec((tm, d), ctx_row), pl.BlockSpec((tm, d), lat_row),
                  pl.BlockSpec((tm, d), col(COL_O // d)),
                  pl.BlockSpec((tm, d), col(COL_GS // d)), pl.BlockSpec((tm, d), col(COL_GM // d)),
                  pl.BlockSpec((tm, d), row),
                  pl.BlockSpec((d, d), const), pl.BlockSpec((d, d), const), pl.BlockSpec((d, d), const),
                  pl.BlockSpec((1, d), const), pl.BlockSpec((1, d), const),
                  modspec(2), modspec(4), modspec(3)],
        out_specs=[pl.BlockSpec((tm, d), row), pl.BlockSpec((tm, d), row)],
        out_shape=[jax.ShapeDtypeStruct((t, d), F32), jax.ShapeDtypeStruct((t, d), BF16)],
        compiler_params=_cparams(("parallel",)),
        name="merge_outproj",
    )(y_ctx, y_lat, proj, hm_ctx, hm_lat, proj, proj, proj, h, wos, wom, wout, ssd_norm.reshape(1, d),
      norm_ffn.reshape(1, d), mod48, mod48, mod48)


def _ffn_kernel(g_ref, v_ref, cw_ref, wd_ref, h_ref, g2_ref, o_ref, *, n_ctx_tiles, ctx_len, n_grid_rows):
    i = pl.program_id(0)
    j = pl.program_id(1)
    tm = g_ref.shape[0]
    is_lat = i >= n_ctx_tiles
    lat_f = is_lat.astype(F32)
    g = g_ref[...].astype(F32)
    row = lax.broadcasted_iota(jnp.int32, (tm, 1), 0)
    width = jnp.where(is_lat, GRID_W, ctx_len)
    colpos = row & (width - 1)
    m_left = (colpos != 0).astype(F32)
    m_right = (colpos != width - 1).astype(F32)
    gl = pltpu.roll(g, 1, 0) * m_left
    gr = pltpu.roll(g, tm - 1, 0) * m_right

    def taps(r):
        return gl * cw_ref[3 * r:3 * r + 1, :] + g * cw_ref[3 * r + 1:3 * r + 2, :] + gr * cw_ref[3 * r + 2:3 * r + 3, :]

    m_top = (row >= GRID_W).astype(F32) * lat_f
    m_bot = (row < (n_grid_rows - 1) * GRID_W).astype(F32) * lat_f
    conv = (taps(1) + pltpu.roll(taps(0), GRID_W, 0) * m_top + pltpu.roll(taps(2), tm - GRID_W, 0) * m_bot
            + cw_ref[9:10, :])
    act = (_silu(conv) * v_ref[...].astype(F32)).astype(BF16)
    part = _dot(act, wd_ref[...])

    @pl.when(j == 0)
    def _():
        o_ref[...] = part

    @pl.when(j > 0)
    def _():
        o_ref[...] += part

    @pl.when(j == pl.num_programs(1) - 1)
    def _():
        o_ref[...] = h_ref[...] + g2_ref[0] * o_ref[...]


def _ffn(gv, cw16, wd, h, mod48, n_ctx_rows, ctx_len, lat_len, tf=256):
    t, d = h.shape
    tm = lat_len
    nj = D_FF // tf
    mi = functools.partial(_mod_index, tm=tm, n_ctx_rows=n_ctx_rows, lat_len=lat_len)
    return pl.pallas_call(
        functools.partial(_ffn_kernel, n_ctx_tiles=n_ctx_rows // tm, ctx_len=ctx_len,
                          n_grid_rows=lat_len // GRID_W),
        grid=(t // tm, nj),
        in_specs=[pl.BlockSpec((tm, tf), lambda i, j: (i, j)),
                  pl.BlockSpec((tm, tf), lambda i, j: (i, nj + j)),
                  pl.BlockSpec((16, tf), lambda i, j: (0, j)),
                  pl.BlockSpec((tf, d), lambda i, j: (j, 0)),
                  pl.BlockSpec((tm, d), lambda i, j: (i, 0)),
                  pl.BlockSpec((1, 1, d), lambda i, j: (mi(i, which=5), 0, 0))],
        out_specs=pl.BlockSpec((tm, d), lambda i, j: (i, 0)),
        out_shape=jax.ShapeDtypeStruct((t, d), F32),
        compiler_params=_cparams(("parallel", "arbitrary")),
        name="convffn_down",
    )(gv, gv, cw16, wd, h, mod48)


def _arrange_w_in(w):
    z, xbc, dt, q, k, v, o, ig, fg, gates = (
        w[:, 0:1024], w[:, 1024:2560], w[:, 2560:2592], w[:, 2592:3104], w[:, 3104:3616],
        w[:, 3616:4640], w[:, 4640:5664], w[:, 5664:5680], w[:, 5680:5696], w[:, 5696:7744])
    big = jnp.concatenate([z, o, gates, xbc, q, k, v], axis=1).astype(BF16)
    blocks = []
    for s in (0, 1):
        blocks += [dt[:, s * 8:s * 8 + 8], dt[:, 16 + s * 8:16 + s * 8 + 8],
                   ig[:, s * 4:s * 4 + 4], ig[:, 8 + s * 4:8 + s * 4 + 4],
                   fg[:, s * 4:s * 4 + 4], fg[:, 8 + s * 4:8 + s * 4 + 4],
                   jnp.zeros((w.shape[0], LANES - 32), w.dtype)]
    small = jnp.concatenate(blocks, axis=1).astype(BF16)
    return big, small


def _slab_rows(parts, n_rows=8):
    out = jnp.zeros((2, n_rows, LANES), F32)
    for r, lane0, val in parts:
        out = out.at[:, r, lane0:lane0 + val.shape[1]].set(val)
    return out


def _ssd_params(dt_bias, a_log, d_skip, conv_w, conv_b):
    par = _slab_rows([(0, 0, dt_bias[0].reshape(2, 8)), (0, 8, dt_bias[1].reshape(2, 8)),
                      (1, 0, a_log[0].reshape(2, 8)), (1, 8, a_log[1].reshape(2, 8))])
    dvec = jnp.repeat(d_skip, SSD_HEADDIM).reshape(2, 1, 512)
    cw = jnp.concatenate([conv_w, conv_b[None], jnp.zeros((4, conv_w.shape[1]), F32)], axis=0)
    cwx = cw[:, 0:1024].reshape(8, 2, 512).transpose(1, 0, 2)
    cwb = cw[:, 1024:1280].reshape(8, 2, LANES).transpose(1, 0, 2)
    cwc = cw[:, 1280:1536].reshape(8, 2, LANES).transpose(1, 0, 2)
    return par, dvec, cwx, cwb, cwc


def _ml_params(i_bias, f_bias, ml_norm):
    par = _slab_rows([(0, LANE_I, i_bias[0].reshape(2, 4)), (0, LANE_I + 4, i_bias[1].reshape(2, 4)),
                      (0, LANE_F, f_bias[0].reshape(2, 4)), (0, LANE_F + 4, f_bias[1].reshape(2, 4))])
    return par, ml_norm.reshape(2, 1, 512)


def _ssd_state_to_kernel(s):
    b = s.shape[0]
    s = s.reshape(b, 2, 2, 2, 4, 64, 64).transpose(0, 2, 1, 3, 6, 4, 5)
    return s.reshape(b, 2, 2, 128, 256)


def _ssd_state_from_kernel(s):
    b = s.shape[0]
    s = s.reshape(b, 2, 2, 2, 64, 4, 64).transpose(0, 2, 1, 3, 5, 6, 4)
    return s.reshape(b, 2, SSD_HEADS, SSD_HEADDIM, SSD_STATE)


def _ml_state_to_kernel(c0, n0, m0):
    b = c0.shape[0]
    aug = jnp.concatenate([jnp.swapaxes(c0, -1, -2),
                           jnp.broadcast_to(n0[..., None], n0.shape + (LANES,))], axis=-1)
    aug = aug.reshape(b, 2, 2, 2, 2, 64, 256).transpose(0, 2, 1, 3, 4, 5, 6).reshape(b, 2, 2, 2, 128, 256)
    m = m0.reshape(b, 2, 2, 4).transpose(0, 2, 1, 3).reshape(b, 2, 1, 8)
    mk = jnp.zeros((b, 2, 1, LANES), F32).at[..., 0:8].set(m)
    return aug, mk


def _ml_state_from_kernel(st, mfin):
    b = st.shape[0]
    st = st.reshape(b, 2, 2, 2, 2, 64, 256).transpose(0, 2, 1, 3, 4, 5, 6).reshape(b, 2, ML_HEADS, 64, 256)
    c = jnp.swapaxes(st[..., 0:ML_DV], -1, -2)
    n = st[..., ML_DV]
    m = jnp.stack([mfin[:, :, 0, 0:4], mfin[:, :, 1, 4:8]], axis=1)
    return c, n, m.reshape(b, 2, ML_HEADS)


def kernel(x_prompt, x_sample, state_ssd, state_mlstm_C, state_mlstm_n, state_mlstm_m, c, c_ctx, w_ada, b_ada,
           norm_mix, w_in, ssd_conv_w, ssd_conv_b, ssd_dt_bias, ssd_a_log, ssd_d, ssd_norm, w_o_ssd, ml_i_bias,
           ml_f_bias, ml_norm, w_o_ml, w_out, norm_ffn, w_up, ffn_conv_w, ffn_conv_b, w_down, final_norm):
    nb, ctx_len, d = x_prompt.shape
    nlat, lat_len, _ = x_sample.shape
    depth = w_in.shape[0]
    n_ctx_rows = nb * ctx_len
    n_lat_rows = nlat * lat_len

    h = jnp.concatenate([x_prompt.reshape(n_ctx_rows, d), x_sample.reshape(n_lat_rows, d)], axis=0)
    cond8 = jnp.concatenate([c_ctx[None], c, jnp.zeros((8 - 1 - nlat, d), F32)], axis=0)
    mod = _ada(cond8, w_ada, b_ada)

    ssd_l, c_l, n_l, m_l = [], [], [], []
    for l in range(depth):
        mod48 = mod[l].reshape(48, 1, d)
        w_big, w_small = _arrange_w_in(w_in[l])
        par_s, dvec, cwx, cwb, cwc = _ssd_params(ssd_dt_bias[l], ssd_a_log[l], ssd_d[l], ssd_conv_w[l],
                                                 ssd_conv_b[l])
        par_m, normw = _ml_params(ml_i_bias[l], ml_f_bias[l], ml_norm[l])

        u = _norm_mod(h, norm_mix[l], mod48, n_ctx_rows, lat_len, which_scale=1, which_shift=0)
        proj = _matmul(u, w_big, BF16, tm=2048, tn=512, name="in_proj")
        gates = _matmul(u, w_small, F32, tm=2048, tn=GATE_COLS, name="gate_proj")

        y_ctx, s_ctx = _ssd(proj, gates, par_s, dvec, cwx, cwb, cwc, None,
                            row0=0, n_seq=nb, seq_len=ctx_len, emit_state=True)
        y_lat, _ = _ssd(proj, gates, par_s, dvec, cwx, cwb, cwc, _ssd_state_to_kernel(state_ssd[:, l]),
                        row0=n_ctx_rows, n_seq=nlat, seq_len=lat_len, emit_state=False)
        hm_ctx, st_ctx, m_ctx = _mlstm(proj, gates, par_m, normw, None, None,
                                       row0=0, n_seq=nb, seq_len=ctx_len, emit_state=True)
        s0m, m0m = _ml_state_to_kernel(state_mlstm_C[:, l], state_mlstm_n[:, l], state_mlstm_m[:, l])
        hm_lat, _, _ = _mlstm(proj, gates, par_m, normw, s0m, m0m,
                              row0=n_ctx_rows, n_seq=nlat, seq_len=lat_len, emit_state=False)
        h, u2 = _outproj(y_ctx, y_lat, hm_ctx, hm_lat, proj, h,w_o_ssd[l].astype(BF16), w_o_ml[l].astype(BF16),
                         w_out[l].astype(BF16), ssd_norm[l], norm_ffn[l], mod48, n_ctx_rows, lat_len)

        gv = _matmul(u2, w_up[l].astype(BF16), BF16, tm=2048, tn=512, name="up_proj")
        cw16 = jnp.concatenate([ffn_conv_w[l].reshape(9, D_FF), ffn_conv_b[l][None],
                                jnp.zeros((6, D_FF), F32)], axis=0)
        h = _ffn(gv, cw16, w_down[l].astype(BF16), h, mod48, n_ctx_rows, ctx_len, lat_len)

        ssd_l.append(_ssd_state_from_kernel(s_ctx))
        c_new, n_new, m_new = _ml_state_from_kernel(st_ctx, m_ctx)
        c_l.append(c_new)
        n_l.append(n_new)
        m_l.append(m_new)

    y_prompt = _final_norm(h, final_norm, 0, n_ctx_rows).reshape(nb, ctx_len, d)
    y_sample = _final_norm(h, final_norm, n_ctx_rows, n_lat_rows).reshape(nlat, lat_len, d)
    return (y_prompt, y_sample, jnp.stack(ssd_l, axis=1), jnp.stack(c_l, axis=1), jnp.stack(n_l, axis=1),
            jnp.stack(m_l, axis=1))
```

```python
import functools

import jax
import jax.numpy as jnp
import numpy as np
from jax import lax
from jax.experimental import pallas as pl
from jax.experimental.pallas import tpu as pltpu

F32 = jnp.float32
BF16 = jnp.bfloat16

D_MODEL = 1024
CHUNK = 128
EPS = 1e-6
GRID_W = 64
SSD_HEADS = 16
SSD_HEADDIM = 64
SSD_STATE = 64
SSD_GROUPS = 4
ML_HEADS = 8
ML_DV = 128
ML_DK = 64
D_FF = 2816

LANES = 128
BF16_SUBLANES = 16
VMEM_LIMIT = 56 * 1024 * 1024

COL_Z, COL_O, COL_GS, COL_GM = 0, 1024, 2048, 3072
COL_X, COL_B, COL_C = 4096, 5120, 5376
COL_Q, COL_K, COL_V = 5632, 6144, 6656
PROJ_COLS = 7680
GATE_COLS = 256
LANE_I = 16
LANE_F = 24
GROWS = 16
LOG2E = 1.4426950408889634


def _cparams(sem):
    return pltpu.CompilerParams(dimension_semantics=sem, vmem_limit_bytes=VMEM_LIMIT)


def _silu(x):
    return x * jax.nn.sigmoid(x)


def _softplus(x):
    return jnp.maximum(x, 0.0) + jnp.log1p(jnp.exp(-jnp.abs(x)))


def _dot(a, b):
    return jnp.dot(a, b, preferred_element_type=F32)


def _dot_nt(a, b):
    return lax.dot_general(a, b, (((1,), (1,)), ((), ())), preferred_element_type=F32)


def _pair(lane_lo, col0, col1):
    return jnp.where(lane_lo, col0, col1)


def _tri2():
    s = lax.broadcasted_iota(jnp.int32, (CHUNK, 2 * CHUNK), 0)
    t = lax.broadcasted_iota(jnp.int32, (CHUNK, 2 * CHUNK), 1)
    keep = ((t < CHUNK) & (s <= t)) | ((t >= CHUNK) & (s >= t - CHUNK))
    return jnp.where(keep, 1.0, 0.0).astype(BF16)


def _lane_cumsums(x, tri2):
    hi = x.astype(BF16)
    r1 = x - hi.astype(F32)
    mid = r1.astype(BF16)
    lo = (r1 - mid.astype(F32)).astype(BF16)
    cs = _dot(hi, tri2) + _dot(mid, tri2) + _dot(lo, tri2)
    return cs[:, 0:CHUNK], cs[:, CHUNK:2 * CHUNK]


def _pad_rows_t(x):
    return jnp.concatenate([x, jnp.zeros((CHUNK - x.shape[0], LANES), F32)], axis=0).T


def _ada_kernel(c_ref, w_ref, b_ref, o_ref):
    cond = _silu(c_ref[...]).astype(BF16)
    o_ref[0] = _dot(cond, w_ref[0].astype(BF16)) + b_ref[0]


def _ada(cond8, w_ada, b_ada):
    depth, d, n = w_ada.shape
    tn = 1536
    return pl.pallas_call(
        _ada_kernel,
        grid=(depth, n // tn),
        in_specs=[pl.BlockSpec((8, d), lambda l, j: (0, 0)),
                  pl.BlockSpec((1, d, tn), lambda l, j: (l, 0, j)),
                  pl.BlockSpec((1, 1, tn), lambda l, j: (l, 0, j))],
        out_specs=pl.BlockSpec((1, 8, tn), lambda l, j: (l, 0, j)),
        out_shape=jax.ShapeDtypeStruct((depth, 8, n), F32),
        compiler_params=_cparams(("arbitrary", "arbitrary")),
        name="ada_mod",
    )(cond8, w_ada, b_ada.reshape(depth, 1, n))


def _mod_index(i, tm, n_ctx_rows, lat_len, which):
    n_ctx_tiles = n_ctx_rows // tm
    tiles_per_seq = lat_len // tm
    row = jnp.where(i < n_ctx_tiles, 0, 1 + (i - n_ctx_tiles) // tiles_per_seq)
    return row * 6 + which


def _rms(x, w):
    return (x * lax.rsqrt(jnp.mean(x * x, axis=-1, keepdims=True) + EPS)) * w


class _Rows:
    def __init__(self, ctx, lat, n_ctx_rows):
        self.ctx, self.lat, self.n_ctx_rows = ctx, lat, n_ctx_rows
        self.lat_row0 = n_ctx_rows if lat is ctx else 0

    def specs(self, tm, d):
        nct = self.n_ctx_rows // tm
        off = self.lat_row0 // tm
        return [pl.BlockSpec((tm, d), lambda i, *_: (jnp.minimum(i, nct - 1), 0)),
                pl.BlockSpec((tm, d), lambda i, *_: (jnp.maximum(i - nct, 0) + off, 0))]


IN_TM = 1024
IN_TN = 512


def _in_proj_kernel(ha_ref, hb_ref, nw_ref, sc_ref, sh_ref, w_ref, wg_ref, proj_ref, gate_ref, u_s, *,
                    n_ctx_tiles, n_col_tiles):
    i = pl.program_id(0)
    j = pl.program_id(1)

    @pl.when(j == 0)
    def _():
        x = jnp.where(i < n_ctx_tiles, ha_ref[...], hb_ref[...])
        u_s[...] = (_rms(x, nw_ref[...]) * (1.0 + sc_ref[0]) + sh_ref[0]).astype(u_s.dtype)

    @pl.when(j < n_col_tiles)
    def _():
        proj_ref[...] = _dot(u_s[...], w_ref[...]).astype(proj_ref.dtype)

    @pl.when(j == n_col_tiles)
    def _():
        gate_ref[...] = _dot(u_s[...], wg_ref[...])


def _in_proj(rows, w, mod48, w_big, w_small, n_rows, lat_len):
    d = w.shape[0]
    tm, tn = IN_TM, IN_TN
    nj = PROJ_COLS // tn
    mi = functools.partial(_mod_index, tm=tm, n_ctx_rows=rows.n_ctx_rows, lat_len=lat_len)
    return pl.pallas_call(
        functools.partial(_in_proj_kernel, n_ctx_tiles=rows.n_ctx_rows // tm, n_col_tiles=nj),
        grid=(n_rows // tm, nj + 1),
        in_specs=rows.specs(tm, d) + [
            pl.BlockSpec((1, d), lambda i, j: (0, 0)),
            pl.BlockSpec((1, 1, d), lambda i, j: (mi(i, which=1), 0, 0)),
            pl.BlockSpec((1, 1, d), lambda i, j: (mi(i, which=0), 0, 0)),
            pl.BlockSpec((d, tn), lambda i, j: (0, jnp.minimum(j, nj - 1))),
            pl.BlockSpec((d, GATE_COLS), lambda i, j: (0, 0))],
        out_specs=[pl.BlockSpec((tm, tn), lambda i, j: (i, jnp.minimum(j, nj - 1))),
                   pl.BlockSpec((tm, GATE_COLS), lambda i, j: (i, 0))],
        out_shape=[jax.ShapeDtypeStruct((n_rows, PROJ_COLS), BF16),
                   jax.ShapeDtypeStruct((n_rows, GATE_COLS), F32)],
        scratch_shapes=[pltpu.VMEM((tm, d), BF16)],
        compiler_params=_cparams(("parallel", "arbitrary")),
        name="in_proj",
    )(rows.ctx, rows.lat, w.reshape(1, d), mod48, mod48, w_big, w_small)


def _norm_kernel(x_ref, w_ref, o_ref):
    o_ref[...] = _rms(x_ref[...], w_ref[...])


def _final_norm(h, w, row0, rows, tm=512):
    d = h.shape[1]
    off = row0 // tm
    return pl.pallas_call(
        _norm_kernel,
        grid=(rows // tm,),
        in_specs=[pl.BlockSpec((tm, d), lambda i: (i + off, 0)),
                  pl.BlockSpec((1, d), lambda i: (0, 0))],
        out_specs=pl.BlockSpec((tm, d), lambda i: (i, 0)),
        out_shape=jax.ShapeDtypeStruct((rows, d), F32),
        compiler_params=_cparams(("parallel",)),
        name="final_norm",
    )(h, w.reshape(1, d))


def _conv_silu_chunk(src_ref, cw_ref, c, n_chunks, seq_len):
    r0 = pl.multiple_of(c * CHUNK, CHUNK)
    cur = src_ref[pl.ds(r0, CHUNK), :].astype(F32)
    p0 = pl.multiple_of(jnp.maximum(r0 - BF16_SUBLANES, 0), BF16_SUBLANES)
    n0 = pl.multiple_of(jnp.minimum(r0 + CHUNK, seq_len - BF16_SUBLANES), BF16_SUBLANES)
    prv = src_ref[pl.ds(p0, BF16_SUBLANES), :].astype(F32)[BF16_SUBLANES - 1:BF16_SUBLANES, :]
    nxt = src_ref[pl.ds(n0, BF16_SUBLANES), :].astype(F32)[0:1, :]
    prv = jnp.where(c > 0, prv, 0.0)
    nxt = jnp.where(c < n_chunks - 1, nxt, 0.0)
    row = lax.broadcasted_iota(jnp.int32, cur.shape, 0)
    up = jnp.where(row == 0, prv, pltpu.roll(cur, 1, 0))
    dn = jnp.where(row == CHUNK - 1, nxt, pltpu.roll(cur, CHUNK - 1, 0))
    y = up * cw_ref[0, 0:1, :] + cur * cw_ref[0, 1:2, :] + dn * cw_ref[0, 2:3, :] + cw_ref[0, 3:4, :]
    return _silu(y)


def _ssd_kernel(*refs, seq_len, zero_init, emit_state):
    it = iter(refs)
    x_ref, b_ref, c_ref, g_ref = next(it), next(it), next(it), next(it)
    par_ref, d_ref, cwx_ref, cwb_ref, cwc_ref = next(it), next(it), next(it), next(it), next(it)
    s0_ref = None if zero_init else next(it)
    y_ref = next(it)
    sfin_ref = next(it) if emit_state else None
    (xbd_s, cm_s, g_s, bt_s, yacc_s, cumc_s, at_s, dtt_s, cumt_s, dtet_s, cdt_s, st_s) = (
        next(it) for _ in range(12))

    n_chunks = seq_len // CHUNK
    lane = lax.broadcasted_iota(jnp.int32, (CHUNK, LANES), 1)
    lane_lo = lane < 64
    lane_row_lo = lax.broadcasted_iota(jnp.int32, (1, LANES), 1) < 64
    trow = lax.broadcasted_iota(jnp.int32, (CHUNK, CHUNK), 0)
    scol = lax.broadcasted_iota(jnp.int32, (CHUNK, CHUNK), 1)
    bias_row = par_ref[0, 0:1, :]
    a_row = -jnp.exp(par_ref[0, 1:2, :])

    def phase_a(c, carry):
        r0 = pl.multiple_of(c * CHUNK, CHUNK)
        xck = _conv_silu_chunk(x_ref, cwx_ref, c, n_chunks, seq_len)
        bck = _conv_silu_chunk(b_ref, cwb_ref, c, n_chunks, seq_len)
        cck = _conv_silu_chunk(c_ref, cwc_ref, c, n_chunks, seq_len)
        yacc_s[pl.ds(r0, CHUNK), :] = d_ref[0] * xck
        for p in range(4):
            xp = xck[:, p * LANES:(p + 1) * LANES]
            xbd_s[c * 4 + p] = jnp.concatenate(
                [jnp.where(lane_lo, xp, 0.0), jnp.where(lane_lo, 0.0, xp)], axis=0).astype(BF16)
        bbf = bck.astype(BF16)
        for g2 in (0, 1):
            in_g = lane_lo if g2 == 0 else jnp.logical_not(lane_lo)
            cm = jnp.where(in_g, cck, 0.0).astype(BF16)
            cm_s[c * 2 + g2] = cm
            g_s[c * 2 + g2] = _dot_nt(cm, bbf)
        bt_s[c] = bck.T
        dt = _softplus(g_ref[pl.ds(r0, CHUNK), :] + bias_row)
        g0 = pl.multiple_of(c * GROWS, GROWS)
        at_s[pl.ds(g0, GROWS), :] = (dt * a_row).T[0:GROWS, :]
        dtt_s[pl.ds(g0, GROWS), :] = dt.T[0:GROWS, :]
        return carry

    lax.fori_loop(0, n_chunks, phase_a, 0)

    cf, cr = _lane_cumsums(at_s[...], _tri2())
    fwd_row = (lax.broadcasted_iota(jnp.int32, (n_chunks * GROWS, 1), 0) & (GROWS - 1)) < 8
    cum_t = jnp.where(fwd_row, cf, cr)
    tot_t = jnp.where(fwd_row, cum_t[:, CHUNK - 1:CHUNK], cum_t[:, 0:1])
    cumt_s[...] = cum_t * LOG2E
    dtet_s[...] = dtt_s[...] * jnp.exp(tot_t - cum_t)
    cdt_s[...] = jnp.broadcast_to(jnp.exp(tot_t), cum_t.shape)

    def phase_c(c, carry):
        g0 = pl.multiple_of(c * GROWS, GROWS)
        cumc_s[c] = _pad_rows_t(cumt_s[pl.ds(g0, GROWS), :])
        return carry

    lax.fori_loop(0, n_chunks, phase_c, 0)

    for d in (0, 1):
        mask = (scol <= trow) if d == 0 else (scol >= trow)
        if zero_init:
            st_s[...] = jnp.zeros(st_s.shape, F32)
        else:
            st_s[...] = s0_ref[0, 0, d]

        def body(ci, carry, d=d, mask=mask):
            c = ci if d == 0 else n_chunks - 1 - ci
            r0 = pl.multiple_of(c * CHUNK, CHUNK)
            cumc = cumc_s[c]
            st = st_s[...]
            st_bf = st.astype(BF16)

            def grow(ref, j):
                return ref[pl.ds(c * GROWS + j, 1), :]

            for g2 in (0, 1):
                yoff = _dot(cm_s[c * 2 + g2], st_bf)
                gmat = g_s[c * 2 + g2]
                bt_g = bt_s[c, g2 * 64:(g2 + 1) * 64, :]
                for pr in (0, 1):
                    p = g2 * 2 + pr
                    j0 = d * 8 + g2 * 4 + pr * 2
                    j1 = j0 + 1
                    cb0 = jnp.broadcast_to(cumc[:, j0:j0 + 1], (CHUNK, LANES))
                    cb1 = jnp.broadcast_to(cumc[:, j1:j1 + 1], (CHUNK, LANES))
                    m0 = gmat * jnp.exp2(jnp.where(mask, cb0 - grow(cumt_s, j0), -jnp.inf))
                    m1 = gmat * jnp.exp2(jnp.where(mask, cb1 - grow(cumt_s, j1), -jnp.inf))
                    mcat = jnp.concatenate([(m0 * grow(dtt_s, j0)).astype(BF16),
                                            (m1 * grow(dtt_s, j1)).astype(BF16)], axis=1)
                    xbd = xbd_s[c * 4 + p]
                    yp = _dot(mcat, xbd) + (jnp.exp2(jnp.where(lane_lo, cb0, cb1))
                                            * yoff[:, pr * LANES:(pr + 1) * LANES])
                    c0 = p * LANES
                    if d == 0:
                        yacc_s[pl.ds(r0, CHUNK), c0:c0 + LANES] += yp
                    else:
                        y_ref[pl.ds(r0, CHUNK), c0:c0 + LANES] = (
                            yacc_s[pl.ds(r0, CHUNK), c0:c0 + LANES] + yp).astype(y_ref.dtype)
                    lhs = jnp.concatenate([(bt_g * grow(dtet_s, j0)).astype(BF16),
                                           (bt_g * grow(dtet_s, j1)).astype(BF16)], axis=1)
                    cd = jnp.where(lane_row_lo, grow(cdt_s, j0), grow(cdt_s, j1))
                    st_s[g2 * 64:(g2 + 1) * 64, pr * LANES:(pr + 1) * LANES] = (
                        cd * st[g2 * 64:(g2 + 1) * 64, pr * LANES:(pr + 1) * LANES] + _dot(lhs, xbd))
            return carry

        lax.fori_loop(0, n_chunks, body, 0, unroll=min(4, n_chunks))
        if emit_state:
            sfin_ref[0, 0, d] = st_s[...]


def _ssd(proj, gates, par, dvec, cwx, cwb, cwc, s0, *, row0, n_seq, seq_len, emit_state):
    zero_init = s0 is None
    rb = row0 // seq_len
    nc = seq_len // CHUNK
    in_specs = [
        pl.BlockSpec((seq_len, 512), lambda b, s: (rb + b, COL_X // 512 + s)),
        pl.BlockSpec((seq_len, LANES), lambda b, s: (rb + b, COL_B // LANES + s)),
        pl.BlockSpec((seq_len, LANES), lambda b, s: (rb + b, COL_C // LANES + s)),
        pl.BlockSpec((seq_len, LANES), lambda b, s: (rb + b, s)),
        pl.BlockSpec((1, 8, LANES), lambda b, s: (s, 0, 0)),
        pl.BlockSpec((1, 1, 512), lambda b, s: (s, 0, 0)),
        pl.BlockSpec((1, 8, 512), lambda b, s: (s, 0, 0)),
        pl.BlockSpec((1, 8, LANES), lambda b, s: (s, 0, 0)),
        pl.BlockSpec((1, 8, LANES), lambda b, s: (s, 0, 0)),
    ]
    args = [proj, proj, proj, gates, par, dvec, cwx, cwb, cwc]
    if not zero_init:
        in_specs.append(pl.BlockSpec((1, 1, 2, 128, 256), lambda b, s: (b, s, 0, 0, 0)))
        args.append(s0)
    out_specs = [pl.BlockSpec((seq_len, 512), lambda b, s: (b, s))]
    out_shape = [jax.ShapeDtypeStruct((n_seq * seq_len, 1024), BF16)]
    if emit_state:
        out_specs.append(pl.BlockSpec((1, 1, 2, 128, 256), lambda b, s: (b, s, 0, 0, 0)))
        out_shape.append(jax.ShapeDtypeStruct((n_seq, 2, 2, 128, 256), F32))
    gate_rows = pltpu.VMEM((nc * GROWS, LANES), F32)
    res = pl.pallas_call(
        functools.partial(_ssd_kernel, seq_len=seq_len, zero_init=zero_init, emit_state=emit_state),
        grid=(n_seq, 2),
        in_specs=in_specs,
        out_specs=out_specs,
        out_shape=out_shape,
        scratch_shapes=[pltpu.VMEM((nc * 4, 2 * CHUNK, LANES), BF16),
                        pltpu.VMEM((nc * 2, CHUNK, LANES), BF16),
                        pltpu.VMEM((nc * 2, CHUNK, CHUNK), F32),
                        pltpu.VMEM((nc, LANES, CHUNK), F32),
                        pltpu.VMEM((seq_len, 512), F32),
                        pltpu.VMEM((nc, CHUNK, LANES), F32),
                        gate_rows, gate_rows, gate_rows, gate_rows, gate_rows,
                        pltpu.VMEM((128, 256), F32)],
        compiler_params=_cparams(("parallel", "arbitrary")),
        name="ssd_scan",
    )(*args)
    return res if emit_state else (res[0], None)


def _mlstm_kernel(*refs, seq_len, zero_init, emit_state):
    it = iter(refs)
    q_ref, k_ref, v_ref, g_ref, par_ref, nw_ref = (next(it) for _ in range(6))
    s0_ref, m0_ref = (None, None) if zero_init else (next(it), next(it))
    hm_ref = next(it)
    sfin_ref, mfin_ref = (next(it), next(it)) if emit_state else (None, None)
    (qm_s, s_s, kt_s, hacc_s, gt_s, cumt_s, et_s, wendt_s, pmaxt_s, mloct_s, cumc_s, pmaxc_s, st_s, m_s) = (
        next(it) for _ in range(14))

    n_chunks = seq_len // CHUNK
    n_rows = n_chunks * GROWS
    lane = lax.broadcasted_iota(jnp.int32, (CHUNK, LANES), 1)
    lane_lo = lane < 64
    trow = lax.broadcasted_iota(jnp.int32, (CHUNK, CHUNK), 0)
    scol = lax.broadcasted_iota(jnp.int32, (CHUNK, CHUNK), 1)
    ones_blk = jnp.ones((CHUNK, LANES), BF16)
    bias_row = par_ref[0, 0:1, :]
    qscale = ML_DK ** -0.5

    def phase_a(c, carry):
        r0 = pl.multiple_of(c * CHUNK, CHUNK)
        for pr in (0, 1):
            qp = q_ref[pl.ds(r0, CHUNK), pr * LANES:(pr + 1) * LANES].astype(F32)
            kp = k_ref[pl.ds(r0, CHUNK), pr * LANES:(pr + 1) * LANES]
            kt_s[c * 2 + pr] = kp.astype(F32).T
            for hh in (0, 1):
                in_h = lane_lo if hh == 0 else jnp.logical_not(lane_lo)
                qm = jnp.where(in_h, qp * qscale, 0.0).astype(BF16)
                qm_s[c * 4 + pr * 2 + hh] = qm
                s_s[c * 4 + pr * 2 + hh] = _dot_nt(qm, kp)
        gk = g_ref[pl.ds(r0, CHUNK), :] + bias_row
        comb = jnp.where(lane < LANE_F, gk, -_softplus(-gk))
        g0 = pl.multiple_of(c * GROWS, GROWS)
        gt_s[pl.ds(g0, GROWS), :] = comb.T[LANE_I:LANE_I + GROWS, :]
        return carry

    lax.fori_loop(0, n_chunks, phase_a, 0)

    g_all = gt_s[...]
    cf, cr = _lane_cumsums(g_all, _tri2())
    rr = lax.broadcasted_iota(jnp.int32, (n_rows, 1), 0) & (GROWS - 1)
    fwd_row = rr < 12
    cum_t = jnp.where(fwd_row, cf, cr)
    tot_t = jnp.where(fwd_row, cum_t[:, CHUNK - 1:CHUNK], cum_t[:, 0:1])
    e_t = pltpu.roll(g_all, 8, 0) - cum_t
    mloc_t = jnp.max(e_t, axis=1, keepdims=True) + tot_t
    lane_b = lax.broadcasted_iota(jnp.int32, (n_rows, LANES), 1)
    pf, pb = e_t, e_t
    k = 1
    while k < CHUNK:
        pf = jnp.maximum(pf, jnp.where(lane_b >= k, pltpu.roll(pf, k, 1), -jnp.inf))
        pb = jnp.maximum(pb, jnp.where(lane_b < CHUNK - k, pltpu.roll(pb, CHUNK - k, 1), -jnp.inf))
        k *= 2
    cumt_s[...] = cum_t
    et_s[...] = e_t * LOG2E
    wendt_s[...] = jnp.exp(e_t + tot_t - mloc_t)
    pmaxt_s[...] = jnp.where(fwd_row, pf, pb) * LOG2E
    mloct_s[...] = jnp.broadcast_to(mloc_t, e_t.shape)

    def phase_c(c, carry):
        g8 = pl.multiple_of(c * GROWS + 8, 8)
        cumc_s[c] = _pad_rows_t(jnp.concatenate([cumt_s[pl.ds(g8, 8), :], mloct_s[pl.ds(g8, 8), :]], axis=0))
        pmaxc_s[c] = _pad_rows_t(pmaxt_s[pl.ds(g8, 8), :])
        return carry

    lax.fori_loop(0, n_chunks, phase_c, 0)

    for d in (0, 1):
        mask = (scol <= trow) if d == 0 else (scol >= trow)
        if zero_init:
            st_s[...] = jnp.zeros(st_s.shape, F32)
            m_s[...] = jnp.zeros(m_s.shape, F32)
        else:
            st_s[...] = s0_ref[0, 0, d]
            m_s[...] = m0_ref[0, 0]

        def body(ci, carry, d=d, mask=mask):
            c = ci if d == 0 else n_chunks - 1 - ci
            r0 = pl.multiple_of(c * CHUNK, CHUNK)
            cumc = cumc_s[c]
            m_prev = m_s[...]
            m_prev2 = m_prev * LOG2E
            mx = jnp.maximum(m_prev2, pmaxc_s[c])
            negmt = -(cumc * LOG2E + mx)
            tot = cumc[CHUNK - 1:CHUNK, :] if d == 0 else cumc[0:1, :]
            mloc = pltpu.roll(cumc[0:1, :], LANES - 8, 1)
            m_new = jnp.maximum(tot + m_prev, mloc)
            a_old = jnp.exp(tot + m_prev - m_new)
            a_loc = jnp.exp(mloc - m_new)
            for pr in (0, 1):
                stp = st_s[pr]
                stp_bf = stp.astype(BF16)
                kt = kt_s[c * 2 + pr]
                for hh in (0, 1):
                    hl = pr * 2 + hh
                    j = d * 4 + hl
                    c0 = hl * LANES
                    e_row = et_s[pl.ds(c * GROWS + 8 + j, 1), :]
                    wend_row = wendt_s[pl.ds(c * GROWS + 8 + j, 1), :]
                    mxb = jnp.broadcast_to(mx[:, j:j + 1], (CHUNK, LANES))
                    sqk = (s_s[c * 4 + hl] * jnp.exp2(jnp.where(mask, e_row - mxb, -jnp.inf))).astype(BF16)
                    qw = (qm_s[c * 4 + hl].astype(F32) * jnp.exp2(m_prev2[:, j:j + 1] - mxb)).astype(BF16)
                    vaug = jnp.concatenate([v_ref[pl.ds(r0, CHUNK), c0:c0 + LANES], ones_blk], axis=1)
                    nd = _dot(jnp.concatenate([sqk, qw], axis=1), jnp.concatenate([vaug, stp_bf], axis=0))
                    emt = jnp.exp2(jnp.broadcast_to(negmt[:, j:j + 1], (CHUNK, LANES)))
                    den = jnp.maximum(jnp.abs(nd[:, LANES:2 * LANES]), emt)
                    hout = nd[:, 0:LANES] / den
                    if d == 0:
                        hacc_s[pl.ds(r0, CHUNK), c0:c0 + LANES] = hout
                    else:
                        hacc_s[pl.ds(r0, CHUNK), c0:c0 + LANES] += hout
                    ktw = (kt[hh * 64:(hh + 1) * 64, :] * wend_row).astype(BF16)
                    st_s[pr, hh * 64:(hh + 1) * 64, :] = (a_old[:, j:j + 1] * stp[hh * 64:(hh + 1) * 64, :]
                                                          + a_loc[:, j:j + 1] * _dot(ktw, vaug))
            m_s[...] = m_new
            return carry

        lax.fori_loop(0, n_chunks, body, 0, unroll=2)
        if emit_state:
            sfin_ref[0, 0, d] = st_s[...]
            mfin_ref[0, 0, d:d + 1, :] = m_s[...]

    def phase_e(c, carry):
        r0 = pl.multiple_of(c * CHUNK, CHUNK)
        for hl in range(4):
            c0 = hl * LANES
            hm_ref[pl.ds(r0, CHUNK), c0:c0 + LANES] = _rms(
                hacc_s[pl.ds(r0, CHUNK), c0:c0 + LANES], nw_ref[0, :, c0:c0 + LANES]).astype(hm_ref.dtype)
        return carry

    lax.fori_loop(0, n_chunks, phase_e, 0)


def _mlstm(proj, gates, par, normw, s0, m0, *, row0, n_seq, seq_len, emit_state):
    zero_init = s0 is None
    rb = row0 // seq_len
    nc = seq_len // CHUNK
    in_specs = [
        pl.BlockSpec((seq_len, 256), lambda b, s: (rb + b, COL_Q // 256 + s)),
        pl.BlockSpec((seq_len, 256), lambda b, s: (rb + b, COL_K // 256 + s)),
        pl.BlockSpec((seq_len, 512), lambda b, s: (rb + b, COL_V // 512 + s)),
        pl.BlockSpec((seq_len, LANES), lambda b, s: (rb + b, s)),
        pl.BlockSpec((1, 8, LANES), lambda b, s: (s, 0, 0)),
        pl.BlockSpec((1, 1, 512), lambda b, s: (s, 0, 0)),
    ]
    args = [proj, proj, proj, gates, par, normw]
    if not zero_init:
        in_specs.append(pl.BlockSpec((1, 1, 2, 2, 128, 256), lambda b, s: (b, s, 0, 0, 0, 0)))
        in_specs.append(pl.BlockSpec((1, 1, 1, LANES), lambda b, s: (b, s, 0, 0)))
        args += [s0, m0]
    out_specs = [pl.BlockSpec((seq_len, 512), lambda b, s: (b, s))]
    out_shape = [jax.ShapeDtypeStruct((n_seq * seq_len, 1024), BF16)]
    if emit_state:
        out_specs.append(pl.BlockSpec((1, 1, 2, 2, 128, 256), lambda b, s: (b, s, 0, 0, 0, 0)))
        out_shape.append(jax.ShapeDtypeStruct((n_seq, 2, 2, 2, 128, 256), F32))
        out_specs.append(pl.BlockSpec((1, 1, 2, LANES), lambda b, s: (b, s, 0, 0)))
        out_shape.append(jax.ShapeDtypeStruct((n_seq, 2, 2, LANES), F32))
    gate_rows = pltpu.VMEM((nc * GROWS, LANES), F32)
    res = pl.pallas_call(
        functools.partial(_mlstm_kernel, seq_len=seq_len, zero_init=zero_init, emit_state=emit_state),
        grid=(n_seq, 2),
        in_specs=in_specs,
        out_specs=out_specs,
        out_shape=out_shape,
        scratch_shapes=[pltpu.VMEM((nc * 4, CHUNK, LANES), BF16),
                        pltpu.VMEM((nc * 4, CHUNK, CHUNK), F32),
                        pltpu.VMEM((nc * 2, LANES, CHUNK), F32),
                        pltpu.VMEM((seq_len, 512), F32),
                        gate_rows, gate_rows, gate_rows, gate_rows, gate_rows, gate_rows,
                        pltpu.VMEM((nc, CHUNK, LANES), F32),
                        pltpu.VMEM((nc, CHUNK, LANES), F32),
                        pltpu.VMEM((2, 128, 256), F32),
                        pltpu.VMEM((1, LANES), F32)],
        compiler_params=_cparams(("parallel", "arbitrary")),
        name="mlstm_scan",
    )(*args)
    return res if emit_state else (res[0], None, None)


def _outproj_kernel(yc_ref, yl_ref, z_ref, hc_ref, hl_ref, o_ref, gs_ref, gm_ref, ha_ref, hb_ref, wos_ref,
                    wom_ref, wout_ref, sn_ref, nf_ref, g1_ref, sc2_ref, sh2_ref, hout_ref, u2_ref, *,
                    n_ctx_tiles):
    is_ctx = pl.program_id(0) < n_ctx_tiles
    y_in = jnp.where(is_ctx, yc_ref[...].astype(F32), yl_ref[...].astype(F32))
    hm_in = jnp.where(is_ctx, hc_ref[...].astype(F32), hl_ref[...].astype(F32))
    y = y_in * _silu(z_ref[...].astype(F32))
    y = _rms(y, sn_ref[...]).astype(BF16)
    y_ssd = _dot(y, wos_ref[...])
    hm = (hm_in * jax.nn.sigmoid(o_ref[...].astype(F32))).astype(BF16)
    y_ml = _dot(hm, wom_ref[...])
    mix = (jax.nn.sigmoid(gs_ref[...].astype(F32)) * y_ssd
           + jax.nn.sigmoid(gm_ref[...].astype(F32)) * y_ml).astype(BF16)
    h = jnp.where(is_ctx, ha_ref[...], hb_ref[...]) + g1_ref[0] * _dot(mix, wout_ref[...])
    hout_ref[...] = h
    u2_ref[...] = (_rms(h, nf_ref[...]) * (1.0 + sc2_ref[0]) + sh2_ref[0]).astype(u2_ref.dtype)


def _outproj(y_ctx, y_lat, hm_ctx, hm_lat, proj, rows, wos, wom, wout, ssd_norm, norm_ffn, mod48, lat_len,
             tm=512):
    t, d = proj.shape[0], wos.shape[0]
    n_ctx_rows = rows.n_ctx_rows
    nct = n_ctx_rows // tm
    mi = functools.partial(_mod_index, tm=tm, n_ctx_rows=n_ctx_rows, lat_len=lat_len)
    row = lambda i: (i, 0)
    ctx_row = lambda i: (jnp.minimum(i, nct - 1), 0)
    lat_row = lambda i: (jnp.maximum(i - nct, 0), 0)
    const = lambda i: (0, 0)
    col = lambda k: (lambda i: (i, k))
    modspec = lambda which: pl.BlockSpec((1, 1, d), lambda i: (mi(i, which=which), 0, 0))
    return pl.pallas_call(
        functools.partial(_outproj_kernel, n_ctx_tiles=nct),
        grid=(t // tm,),
        in_specs=[pl.BlockSpec((tm, d), ctx_row), pl.BlockSpec((tm, d), lat_row),
                  pl.BlockSpec((tm, d), col(COL_Z // d)),
                  pl.BlockSpec((tm, d), ctx_row), pl.BlockSpec((tm, d), lat_row),
                  pl.BlockSpec((tm, d), col(COL_O // d)),
                  pl.BlockSpec((tm, d), col(COL_GS // d)), pl.BlockSpec((tm, d), col(COL_GM // d))]
        + rows.specs(tm, d) + [
                  pl.BlockSpec((d, d), const), pl.BlockSpec((d, d), const), pl.BlockSpec((d, d), const),
                  pl.BlockSpec((1, d), const), pl.BlockSpec((1, d), const),
                  modspec(2), modspec(4), modspec(3)],
        out_specs=[pl.BlockSpec((tm, d), row), pl.BlockSpec((tm, d), row)],
        out_shape=[jax.ShapeDtypeStruct((t, d), F32), jax.ShapeDtypeStruct((t, d), BF16)],
        compiler_params=_cparams(("parallel",)),
        name="merge_outproj",
    )(y_ctx, y_lat, proj, hm_ctx, hm_lat, proj, proj, proj, rows.ctx, rows.lat, wos, wom, wout,
      ssd_norm.reshape(1, d), norm_ffn.reshape(1, d), mod48, mod48, mod48)


FFN_TM = 512
FFN_TF = 256


def _ffn_kernel(u_ref, ut_ref, ub_ref, wup_ref, cw_ref, wd_ref, h_ref, g2_ref, o_ref, act_s, *,
                n_ctx_tiles, ctx_len, tiles_per_seq):
    i = pl.program_id(0)
    tm, tf = FFN_TM, FFN_TF
    n_chunks = D_FF // tf
    ext = tm + 2 * GRID_W

    def cw(r, j):
        return cw_ref[r:r + 1, j * tf:(j + 1) * tf]

    @pl.when(i < n_ctx_tiles)
    def _():
        u = u_ref[...]
        pos = lax.broadcasted_iota(jnp.int32, (tm, tf), 0) & (ctx_len - 1)
        for j in range(n_chunks):
            g = _dot(u, wup_ref[:, j * tf:(j + 1) * tf])
            val = _dot(u, wup_ref[:, D_FF + j * tf:D_FF + (j + 1) * tf])
            gl = jnp.where(pos != 0, pltpu.roll(g, 1, 0), 0.0)
            gr = jnp.where(pos != ctx_len - 1, pltpu.roll(g, tm - 1, 0), 0.0)
            conv = gl * cw(3, j) + g * cw(4, j) + gr * cw(5, j) + cw(9, j)
            act_s[:, j * tf:(j + 1) * tf] = (_silu(conv) * val).astype(BF16)

    @pl.when(i >= n_ctx_tiles)
    def _():
        ti = (i - n_ctx_tiles) % tiles_per_seq
        u = u_ref[...]
        top = jnp.where(ti > 0, ut_ref[...], jnp.zeros_like(ut_ref[...]))
        bot = jnp.where(ti < tiles_per_seq - 1, ub_ref[...], jnp.zeros_like(ub_ref[...]))
        uext = jnp.concatenate([top, u, bot], axis=0)
        col = lax.broadcasted_iota(jnp.int32, (ext, tf), 0) & (GRID_W - 1)
        for j in range(n_chunks):
            gx = _dot(uext, wup_ref[:, j * tf:(j + 1) * tf])
            val = _dot(u, wup_ref[:, D_FF + j * tf:D_FF + (j + 1) * tf])
            gl = jnp.where(col != 0, pltpu.roll(gx, 1, 0), 0.0)
            gr = jnp.where(col != GRID_W - 1, pltpu.roll(gx, ext - 1, 0), 0.0)

            def taps(r, lo):
                return (gl[lo:lo + tm] * cw(3 * r, j) + gx[lo:lo + tm] * cw(3 * r + 1, j)
                        + gr[lo:lo + tm] * cw(3 * r + 2, j))

            conv = taps(1, GRID_W) + taps(0, 0) + taps(2, 2 * GRID_W) + cw(9, j)
            act_s[:, j * tf:(j + 1) * tf] = (_silu(conv) * val).astype(BF16)

    o_ref[...] = h_ref[...] + g2_ref[0] * _dot(act_s[...], wd_ref[...])


def _ffn(u2, wup, cw16, wd, h, mod48, n_ctx_rows, ctx_len, lat_len):
    t, d = h.shape
    tm = FFN_TM
    hb = tm // GRID_W
    n_hblocks = t // GRID_W
    mi = functools.partial(_mod_index, tm=tm, n_ctx_rows=n_ctx_rows, lat_len=lat_len)
    resident = lambda shape: pl.BlockSpec(shape, lambda i: (0, 0), pipeline_mode=pl.Buffered(1))
    return pl.pallas_call(
        functools.partial(_ffn_kernel, n_ctx_tiles=n_ctx_rows // tm, ctx_len=ctx_len,
                          tiles_per_seq=lat_len // tm),
        grid=(t // tm,),
        in_specs=[pl.BlockSpec((tm, d), lambda i: (i, 0)),
                  pl.BlockSpec((GRID_W, d), lambda i: (jnp.maximum(i * hb - 1, 0), 0)),
                  pl.BlockSpec((GRID_W, d), lambda i: (jnp.minimum((i + 1) * hb, n_hblocks - 1), 0)),
                  resident((d, 2 * D_FF)),
                  resident((16, D_FF)),
                  resident((D_FF, d)),
                  pl.BlockSpec((tm, d), lambda i: (i, 0)),
                  pl.BlockSpec((1, 1, d), lambda i: (mi(i, which=5), 0, 0))],
        out_specs=pl.BlockSpec((tm, d), lambda i: (i, 0)),
        out_shape=jax.ShapeDtypeStruct((t, d), F32),
        scratch_shapes=[pltpu.VMEM((tm, D_FF), BF16)],
        compiler_params=_cparams(("parallel",)),
        name="convffn",
    )(u2, u2, u2, wup, cw16, wd, h, mod48)


def _arrange_w_in(w):
    z, xbc, dt, q, k, v, o, ig, fg, gates = (
        w[:, 0:1024], w[:, 1024:2560], w[:, 2560:2592], w[:, 2592:3104], w[:, 3104:3616],
        w[:, 3616:4640], w[:, 4640:5664], w[:, 5664:5680], w[:, 5680:5696], w[:, 5696:7744])
    big = jnp.concatenate([z, o, gates, xbc, q, k, v], axis=1).astype(BF16)
    blocks = []
    for s in (0, 1):
        blocks += [dt[:, s * 8:s * 8 + 8], dt[:, 16 + s * 8:16 + s * 8 + 8],
                   ig[:, s * 4:s * 4 + 4], ig[:, 8 + s * 4:8 + s * 4 + 4],
                   fg[:, s * 4:s * 4 + 4], fg[:, 8 + s * 4:8 + s * 4 + 4],
                   jnp.zeros((w.shape[0], LANES - 32), w.dtype)]
    small = jnp.concatenate(blocks, axis=1).astype(BF16)
    return big, small


def _slab_rows(parts, n_rows=8):
    out = jnp.zeros((2, n_rows, LANES), F32)
    for r, lane0, val in parts:
        out = out.at[:, r, lane0:lane0 + val.shape[1]].set(val)
    return out


def _ssd_params(dt_bias, a_log, d_skip, conv_w, conv_b):
    par = _slab_rows([(0, 0, dt_bias[0].reshape(2, 8)), (0, 8, dt_bias[1].reshape(2, 8)),
                      (1, 0, a_log[0].reshape(2, 8)), (1, 8, a_log[1].reshape(2, 8))])
    dvec = jnp.repeat(d_skip, SSD_HEADDIM).reshape(2, 1, 512)
    cw = jnp.concatenate([conv_w, conv_b[None], jnp.zeros((4, conv_w.shape[1]), F32)], axis=0)
    cwx = cw[:, 0:1024].reshape(8, 2, 512).transpose(1, 0, 2)
    cwb = cw[:, 1024:1280].reshape(8, 2, LANES).transpose(1, 0, 2)
    cwc = cw[:, 1280:1536].reshape(8, 2, LANES).transpose(1, 0, 2)
    return par, dvec, cwx, cwb, cwc


def _ml_params(i_bias, f_bias, ml_norm):
    par = _slab_rows([(0, LANE_I, i_bias[0].reshape(2, 4)), (0, LANE_I + 4, i_bias[1].reshape(2, 4)),
                      (0, LANE_F, f_bias[0].reshape(2, 4)), (0, LANE_F + 4, f_bias[1].reshape(2, 4))])
    return par, ml_norm.reshape(2, 1, 512)


def _ssd_state_to_kernel(s):
    b = s.shape[0]
    s = s.reshape(b, 2, 2, 2, 4, 64, 64).transpose(0, 2, 1, 3, 6, 4, 5)
    return s.reshape(b, 2, 2, 128, 256)


def _ssd_state_from_kernel(s):
    b = s.shape[0]
    s = s.reshape(b, 2, 2, 2, 64, 4, 64).transpose(0, 2, 1, 3, 5, 6, 4)
    return s.reshape(b, 2, SSD_HEADS, SSD_HEADDIM, SSD_STATE)


def _ml_state_to_kernel(c0, n0, m0):
    b = c0.shape[0]
    aug = jnp.concatenate([jnp.swapaxes(c0, -1, -2),
                           jnp.broadcast_to(n0[..., None], n0.shape + (LANES,))], axis=-1)
    aug = aug.reshape(b, 2, 2, 2, 2, 64, 256).transpose(0, 2, 1, 3, 4, 5, 6).reshape(b, 2, 2, 2, 128, 256)
    m = m0.reshape(b, 2, 2, 4).transpose(0, 2, 1, 3).reshape(b, 2, 1, 8)
    mk = jnp.zeros((b, 2, 1, LANES), F32).at[..., 0:8].set(m)
    return aug, mk


def _ml_state_from_kernel(st, mfin):
    b = st.shape[0]
    st = st.reshape(b, 2, 2, 2, 2, 64, 256).transpose(0, 2, 1, 3, 4, 5, 6).reshape(b, 2, ML_HEADS, 64, 256)
    c = jnp.swapaxes(st[..., 0:ML_DV], -1, -2)
    n = st[..., ML_DV]
    m = jnp.stack([mfin[:, :, 0, 0:4], mfin[:, :, 1, 4:8]], axis=1)
    return c, n, m.reshape(b, 2, ML_HEADS)


def kernel(x_prompt, x_sample, state_ssd, state_mlstm_C, state_mlstm_n, state_mlstm_m, c, c_ctx, w_ada, b_ada,
           norm_mix, w_in, ssd_conv_w, ssd_conv_b, ssd_dt_bias, ssd_a_log, ssd_d, ssd_norm, w_o_ssd, ml_i_bias,
           ml_f_bias, ml_norm, w_o_ml, w_out, norm_ffn, w_up, ffn_conv_w, ffn_conv_b, w_down, final_norm):
    nb, ctx_len, d = x_prompt.shape
    nlat, lat_len, _ = x_sample.shape
    depth = w_in.shape[0]
    n_ctx_rows = nb * ctx_len
    n_lat_rows = nlat * lat_len

    n_rows = n_ctx_rows + n_lat_rows
    rows = _Rows(x_prompt.reshape(n_ctx_rows, d), x_sample.reshape(n_lat_rows, d), n_ctx_rows)
    cond8 = jnp.concatenate([c_ctx[None], c, jnp.zeros((8 - 1 - nlat, d), F32)], axis=0)
    mod = _ada(cond8, w_ada, b_ada)

    ssd_l, c_l, n_l, m_l = [], [], [], []
    for l in range(depth):
        mod48 = mod[l].reshape(48, 1, d)
        w_big, w_small = _arrange_w_in(w_in[l])
        par_s, dvec, cwx, cwb, cwc = _ssd_params(ssd_dt_bias[l], ssd_a_log[l], ssd_d[l], ssd_conv_w[l],
                                                 ssd_conv_b[l])
        par_m, normw = _ml_params(ml_i_bias[l], ml_f_bias[l], ml_norm[l])

        proj, gates = _in_proj(rows, norm_mix[l], mod48, w_big, w_small, n_rows, lat_len)

        y_ctx, s_ctx = _ssd(proj, gates, par_s, dvec, cwx, cwb, cwc, None,
                            row0=0, n_seq=nb, seq_len=ctx_len, emit_state=True)
        y_lat, _ = _ssd(proj, gates, par_s, dvec, cwx, cwb, cwc, _ssd_state_to_kernel(state_ssd[:, l]),
                        row0=n_ctx_rows, n_seq=nlat, seq_len=lat_len, emit_state=False)
        hm_ctx, st_ctx, m_ctx = _mlstm(proj, gates, par_m, normw, None, None,
                                       row0=0, n_seq=nb, seq_len=ctx_len, emit_state=True)
        s0m, m0m = _ml_state_to_kernel(state_mlstm_C[:, l], state_mlstm_n[:, l], state_mlstm_m[:, l])
        hm_lat, _, _ = _mlstm(proj, gates, par_m, normw, s0m, m0m,
                              row0=n_ctx_rows, n_seq=nlat, seq_len=lat_len, emit_state=False)
        h, u2 = _outproj(y_ctx, y_lat, hm_ctx, hm_lat, proj, rows, w_o_ssd[l].astype(BF16),
                         w_o_ml[l].astype(BF16), w_out[l].astype(BF16), ssd_norm[l], norm_ffn[l], mod48, lat_len)

        cw16 = jnp.concatenate([ffn_conv_w[l].reshape(9, D_FF), ffn_conv_b[l][None],
                                jnp.zeros((6, D_FF), F32)], axis=0)
        h = _ffn(u2, w_up[l].astype(BF16), cw16, w_down[l].astype(BF16), h, mod48, n_ctx_rows, ctx_len, lat_len)
        rows = _Rows(h, h, n_ctx_rows)

        ssd_l.append(_ssd_state_from_kernel(s_ctx))
        c_new, n_new, m_new = _ml_state_from_kernel(st_ctx, m_ctx)
        c_l.append(c_new)
        n_l.append(n_new)
        m_l.append(m_new)

    y_prompt = _final_norm(h, final_norm, 0, n_ctx_rows).reshape(nb, ctx_len, d)
    y_sample = _final_norm(h, final_norm, n_ctx_rows, n_lat_rows).reshape(nlat, lat_len, d)
    return (y_prompt, y_sample, jnp.stack(ssd_l, axis=1), jnp.stack(c_l, axis=1), jnp.stack(n_l, axis=1),
            jnp.stack(m_l, axis=1))
```

```python
import functools

import jax
import jax.numpy as jnp
import numpy as np
from jax import lax
from jax.experimental import pallas as pl
from jax.experimental.pallas import tpu as pltpu

F32 = jnp.float32
BF16 = jnp.bfloat16

D_MODEL = 1024
CHUNK = 128
EPS = 1e-6
GRID_W = 64
SSD_HEADS = 16
SSD_HEADDIM = 64
SSD_STATE = 64
SSD_GROUPS = 4
ML_HEADS = 8
ML_DV = 128
ML_DK = 64
D_FF = 2816

LANES = 128
BF16_SUBLANES = 16
VMEM_LIMIT = 56 * 1024 * 1024

COL_Z, COL_O, COL_GS, COL_GM = 0, 1024, 2048, 3072
COL_X, COL_B, COL_C = 4096, 5120, 5376
COL_Q, COL_K, COL_V = 5632, 6144, 6656
PROJ_COLS = 7680
GATE_COLS = 256
LANE_I = 16
LANE_F = 24
GROWS = 16
LOG2E = 1.4426950408889634


def _cparams(sem):
    return pltpu.CompilerParams(dimension_semantics=sem, vmem_limit_bytes=VMEM_LIMIT)


def _silu(x):
    return x * jax.nn.sigmoid(x)


def _softplus(x):
    return jnp.maximum(x, 0.0) + jnp.log1p(jnp.exp(-jnp.abs(x)))


def _dot(a, b):
    return jnp.dot(a, b, preferred_element_type=F32)


def _dot_nt(a, b):
    return lax.dot_general(a, b, (((1,), (1,)), ((), ())), preferred_element_type=F32)


def _pair(lane_lo, col0, col1):
    return jnp.where(lane_lo, col0, col1)


def _tri2():
    s = lax.broadcasted_iota(jnp.int32, (CHUNK, 2 * CHUNK), 0)
    t = lax.broadcasted_iota(jnp.int32, (CHUNK, 2 * CHUNK), 1)
    keep = ((t < CHUNK) & (s <= t)) | ((t >= CHUNK) & (s >= t - CHUNK))
    return jnp.where(keep, 1.0, 0.0).astype(BF16)


def _lane_cumsums(x, tri2):
    hi = x.astype(BF16)
    r1 = x - hi.astype(F32)
    mid = r1.astype(BF16)
    lo = (r1 - mid.astype(F32)).astype(BF16)
    cs = _dot(hi, tri2) + _dot(mid, tri2) + _dot(lo, tri2)
    return cs[:, 0:CHUNK], cs[:, CHUNK:2 * CHUNK]


def _pad_rows_t(x):
    return jnp.concatenate([x, jnp.zeros((CHUNK - x.shape[0], LANES), F32)], axis=0).T


def _ada_kernel(c_ref, w_ref, b_ref, o_ref):
    cond = _silu(c_ref[...]).astype(BF16)
    o_ref[0] = _dot(cond, w_ref[0].astype(BF16)) + b_ref[0]


def _ada(cond8, w_ada, b_ada):
    depth, d, n = w_ada.shape
    tn = 1536
    return pl.pallas_call(
        _ada_kernel,
        grid=(depth, n // tn),
        in_specs=[pl.BlockSpec((8, d), lambda l, j: (0, 0)),
                  pl.BlockSpec((1, d, tn), lambda l, j: (l, 0, j)),
                  pl.BlockSpec((1, 1, tn), lambda l, j: (l, 0, j))],
        out_specs=pl.BlockSpec((1, 8, tn), lambda l, j: (l, 0, j)),
        out_shape=jax.ShapeDtypeStruct((depth, 8, n), F32),
        compiler_params=_cparams(("arbitrary", "arbitrary")),
        name="ada_mod",
    )(cond8, w_ada, b_ada.reshape(depth, 1, n))


def _mod_index(i, tm, n_ctx_rows, lat_len, which):
    n_ctx_tiles = n_ctx_rows // tm
    tiles_per_seq = lat_len // tm
    row = jnp.where(i < n_ctx_tiles, 0, 1 + (i - n_ctx_tiles) // tiles_per_seq)
    return row * 6 + which


def _rms(x, w):
    return (x * lax.rsqrt(jnp.mean(x * x, axis=-1, keepdims=True) + EPS)) * w


class _Rows:
    def __init__(self, ctx, lat, n_ctx_rows):
        self.ctx, self.lat, self.n_ctx_rows = ctx, lat, n_ctx_rows
        self.lat_row0 = n_ctx_rows if lat is ctx else 0

    def specs(self, tm, d):
        nct = self.n_ctx_rows // tm
        off = self.lat_row0 // tm
        return [pl.BlockSpec((tm, d), lambda i, *_: (jnp.minimum(i, nct - 1), 0)),
                pl.BlockSpec((tm, d), lambda i, *_: (jnp.maximum(i - nct, 0) + off, 0))]


IN_TM = 1024
IN_TN = 1536


def _in_proj_kernel(ha_ref, hb_ref, nw_ref, sc_ref, sh_ref, w_ref, wg_ref, proj_ref, gate_ref, u_s, *,
                    n_ctx_tiles, n_col_tiles):
    i = pl.program_id(0)
    j = pl.program_id(1)

    @pl.when(j == 0)
    def _():
        x = jnp.where(i < n_ctx_tiles, ha_ref[...], hb_ref[...])
        u_s[...] = (_rms(x, nw_ref[...]) * (1.0 + sc_ref[0]) + sh_ref[0]).astype(u_s.dtype)

    @pl.when(j < n_col_tiles)
    def _():
        proj_ref[...] = _dot(u_s[...], w_ref[...]).astype(proj_ref.dtype)

    @pl.when(j == n_col_tiles)
    def _():
        gate_ref[...] = _dot(u_s[...], wg_ref[...])


def _in_proj(rows, w, mod48, w_big, w_small, n_rows, lat_len):
    d = w.shape[0]
    tm, tn = IN_TM, IN_TN
    nj = PROJ_COLS // tn
    mi = functools.partial(_mod_index, tm=tm, n_ctx_rows=rows.n_ctx_rows, lat_len=lat_len)
    return pl.pallas_call(
        functools.partial(_in_proj_kernel, n_ctx_tiles=rows.n_ctx_rows // tm, n_col_tiles=nj),
        grid=(n_rows // tm, nj + 1),
        in_specs=rows.specs(tm, d) + [
            pl.BlockSpec((1, d), lambda i, j: (0, 0)),
            pl.BlockSpec((1, 1, d), lambda i, j: (mi(i, which=1), 0, 0)),
            pl.BlockSpec((1, 1, d), lambda i, j: (mi(i, which=0), 0, 0)),
            pl.BlockSpec((d, tn), lambda i, j: (0, jnp.minimum(j, nj - 1))),
            pl.BlockSpec((d, GATE_COLS), lambda i, j: (0, 0))],
        out_specs=[pl.BlockSpec((tm, tn), lambda i, j: (i, jnp.minimum(j, nj - 1))),
                   pl.BlockSpec((tm, GATE_COLS), lambda i, j: (i, 0))],
        out_shape=[jax.ShapeDtypeStruct((n_rows, PROJ_COLS), BF16),
                   jax.ShapeDtypeStruct((n_rows, GATE_COLS), F32)],
        scratch_shapes=[pltpu.VMEM((tm, d), BF16)],
        compiler_params=_cparams(("parallel", "arbitrary")),
        name="in_proj",
    )(rows.ctx, rows.lat, w.reshape(1, d), mod48, mod48, w_big, w_small)


def _norm_kernel(x_ref, w_ref, o_ref):
    o_ref[...] = _rms(x_ref[...], w_ref[...])


def _final_norm(h, w, row0, rows, tm=512):
    d = h.shape[1]
    off = row0 // tm
    return pl.pallas_call(
        _norm_kernel,
        grid=(rows // tm,),
        in_specs=[pl.BlockSpec((tm, d), lambda i: (i + off, 0)),
                  pl.BlockSpec((1, d), lambda i: (0, 0))],
        out_specs=pl.BlockSpec((tm, d), lambda i: (i, 0)),
        out_shape=jax.ShapeDtypeStruct((rows, d), F32),
        compiler_params=_cparams(("parallel",)),
        name="final_norm",
    )(h, w.reshape(1, d))


def _conv_silu_chunk(src_ref, cw_ref, c, n_chunks, seq_len):
    r0 = pl.multiple_of(c * CHUNK, CHUNK)
    cur = src_ref[pl.ds(r0, CHUNK), :].astype(F32)
    p0 = pl.multiple_of(jnp.maximum(r0 - BF16_SUBLANES, 0), BF16_SUBLANES)
    n0 = pl.multiple_of(jnp.minimum(r0 + CHUNK, seq_len - BF16_SUBLANES), BF16_SUBLANES)
    prv = src_ref[pl.ds(p0, BF16_SUBLANES), :].astype(F32)[BF16_SUBLANES - 1:BF16_SUBLANES, :]
    nxt = src_ref[pl.ds(n0, BF16_SUBLANES), :].astype(F32)[0:1, :]
    prv = jnp.where(c > 0, prv, 0.0)
    nxt = jnp.where(c < n_chunks - 1, nxt, 0.0)
    row = lax.broadcasted_iota(jnp.int32, cur.shape, 0)
    up = jnp.where(row == 0, prv, pltpu.roll(cur, 1, 0))
    dn = jnp.where(row == CHUNK - 1, nxt, pltpu.roll(cur, CHUNK - 1, 0))
    y = up * cw_ref[0, 0:1, :] + cur * cw_ref[0, 1:2, :] + dn * cw_ref[0, 2:3, :] + cw_ref[0, 3:4, :]
    return _silu(y)


def _ssd_kernel(*refs, seq_len, zero_init, emit_state):
    it = iter(refs)
    x_ref, b_ref, c_ref, g_ref = next(it), next(it), next(it), next(it)
    par_ref, d_ref, cwx_ref, cwb_ref, cwc_ref = next(it), next(it), next(it), next(it), next(it)
    s0_ref = None if zero_init else next(it)
    y_ref = next(it)
    sfin_ref = next(it) if emit_state else None
    (xbd_s, cm_s, g_s, bt_s, yacc_s, cumc_s, at_s, dtt_s, cumt_s, dtet_s, cdt_s, st_s) = (
        next(it) for _ in range(12))

    n_chunks = seq_len // CHUNK
    lane = lax.broadcasted_iota(jnp.int32, (CHUNK, LANES), 1)
    lane_lo = lane < 64
    lane_row_lo = lax.broadcasted_iota(jnp.int32, (1, LANES), 1) < 64
    trow = lax.broadcasted_iota(jnp.int32, (CHUNK, CHUNK), 0)
    scol = lax.broadcasted_iota(jnp.int32, (CHUNK, CHUNK), 1)
    bias_row = par_ref[0, 0:1, :]
    a_row = -jnp.exp(par_ref[0, 1:2, :])

    def phase_a(c, carry):
        r0 = pl.multiple_of(c * CHUNK, CHUNK)
        xck = _conv_silu_chunk(x_ref, cwx_ref, c, n_chunks, seq_len)
        bck = _conv_silu_chunk(b_ref, cwb_ref, c, n_chunks, seq_len)
        cck = _conv_silu_chunk(c_ref, cwc_ref, c, n_chunks, seq_len)
        yacc_s[pl.ds(r0, CHUNK), :] = d_ref[0] * xck
        for p in range(4):
            xp = xck[:, p * LANES:(p + 1) * LANES]
            xbd_s[c * 4 + p] = jnp.concatenate(
                [jnp.where(lane_lo, xp, 0.0), jnp.where(lane_lo, 0.0, xp)], axis=0).astype(BF16)
        bbf = bck.astype(BF16)
        for g2 in (0, 1):
            in_g = lane_lo if g2 == 0 else jnp.logical_not(lane_lo)
            cm = jnp.where(in_g, cck, 0.0).astype(BF16)
            cm_s[c * 2 + g2] = cm
            g_s[c * 2 + g2] = _dot_nt(cm, bbf)
        bt_s[c] = bck.T
        dt = _softplus(g_ref[pl.ds(r0, CHUNK), :] + bias_row)
        g0 = pl.multiple_of(c * GROWS, GROWS)
        at_s[pl.ds(g0, GROWS), :] = (dt * a_row).T[0:GROWS, :]
        dtt_s[pl.ds(g0, GROWS), :] = dt.T[0:GROWS, :]
        return carry

    lax.fori_loop(0, n_chunks, phase_a, 0)

    cf, cr = _lane_cumsums(at_s[...], _tri2())
    fwd_row = (lax.broadcasted_iota(jnp.int32, (n_chunks * GROWS, 1), 0) & (GROWS - 1)) < 8
    cum_t = jnp.where(fwd_row, cf, cr)
    tot_t = jnp.where(fwd_row, cum_t[:, CHUNK - 1:CHUNK], cum_t[:, 0:1])
    cumt_s[...] = cum_t * LOG2E
    dtet_s[...] = dtt_s[...] * jnp.exp(tot_t - cum_t)
    cdt_s[...] = jnp.broadcast_to(jnp.exp(tot_t), cum_t.shape)

    def phase_c(c, carry):
        g0 = pl.multiple_of(c * GROWS, GROWS)
        cumc_s[c] = _pad_rows_t(cumt_s[pl.ds(g0, GROWS), :])
        return carry

    lax.fori_loop(0, n_chunks, phase_c, 0)

    for d in (0, 1):
        mask = (scol <= trow) if d == 0 else (scol >= trow)
        if zero_init:
            st_s[...] = jnp.zeros(st_s.shape, F32)
        else:
            st_s[...] = s0_ref[0, 0, d]

        def body(ci, carry, d=d, mask=mask):
            c = ci if d == 0 else n_chunks - 1 - ci
            r0 = pl.multiple_of(c * CHUNK, CHUNK)
            cumc = cumc_s[c]
            st = st_s[...]
            st_bf = st.astype(BF16)

            def grow(ref, j):
                return ref[pl.ds(c * GROWS + j, 1), :]

            for g2 in (0, 1):
                yoff = _dot(cm_s[c * 2 + g2], st_bf)
                gmat = g_s[c * 2 + g2]
                bt_g = bt_s[c, g2 * 64:(g2 + 1) * 64, :]
                for pr in (0, 1):
                    p = g2 * 2 + pr
                    j0 = d * 8 + g2 * 4 + pr * 2
                    j1 = j0 + 1
                    cb0 = jnp.broadcast_to(cumc[:, j0:j0 + 1], (CHUNK, LANES))
                    cb1 = jnp.broadcast_to(cumc[:, j1:j1 + 1], (CHUNK, LANES))
                    m0 = gmat * jnp.exp2(jnp.where(mask, cb0 - grow(cumt_s, j0), -jnp.inf))
                    m1 = gmat * jnp.exp2(jnp.where(mask, cb1 - grow(cumt_s, j1), -jnp.inf))
                    mcat = jnp.concatenate([(m0 * grow(dtt_s, j0)).astype(BF16),
                                            (m1 * grow(dtt_s, j1)).astype(BF16)], axis=1)
                    xbd = xbd_s[c * 4 + p]
                    yp = _dot(mcat, xbd) + (jnp.exp2(jnp.where(lane_lo, cb0, cb1))
                                            * yoff[:, pr * LANES:(pr + 1) * LANES])
                    c0 = p * LANES
                    if d == 0:
                        yacc_s[pl.ds(r0, CHUNK), c0:c0 + LANES] += yp
                    else:
                        y_ref[pl.ds(r0, CHUNK), c0:c0 + LANES] = (
                            yacc_s[pl.ds(r0, CHUNK), c0:c0 + LANES] + yp).astype(y_ref.dtype)
                    lhs = jnp.concatenate([(bt_g * grow(dtet_s, j0)).astype(BF16),
                                           (bt_g * grow(dtet_s, j1)).astype(BF16)], axis=1)
                    cd = jnp.where(lane_row_lo, grow(cdt_s, j0), grow(cdt_s, j1))
                    st_s[g2 * 64:(g2 + 1) * 64, pr * LANES:(pr + 1) * LANES] = (
                        cd * st[g2 * 64:(g2 + 1) * 64, pr * LANES:(pr + 1) * LANES] + _dot(lhs, xbd))
            return carry

        lax.fori_loop(0, n_chunks, body, 0, unroll=min(4, n_chunks))
        if emit_state:
            sfin_ref[0, 0, d] = st_s[...]


def _ssd(proj, gates, par, dvec, cwx, cwb, cwc, s0, *, row0, n_seq, seq_len, emit_state):
    zero_init = s0 is None
    rb = row0 // seq_len
    nc = seq_len // CHUNK
    in_specs = [
        pl.BlockSpec((seq_len, 512), lambda b, s: (rb + b, COL_X // 512 + s)),
        pl.BlockSpec((seq_len, LANES), lambda b, s: (rb + b, COL_B // LANES + s)),
        pl.BlockSpec((seq_len, LANES), lambda b, s: (rb + b, COL_C // LANES + s)),
        pl.BlockSpec((seq_len, LANES), lambda b, s: (rb + b, s)),
        pl.BlockSpec((1, 8, LANES), lambda b, s: (s, 0, 0)),
        pl.BlockSpec((1, 1, 512), lambda b, s: (s, 0, 0)),
        pl.BlockSpec((1, 8, 512), lambda b, s: (s, 0, 0)),
        pl.BlockSpec((1, 8, LANES), lambda b, s: (s, 0, 0)),
        pl.BlockSpec((1, 8, LANES), lambda b, s: (s, 0, 0)),
    ]
    args = [proj, proj, proj, gates, par, dvec, cwx, cwb, cwc]
    if not zero_init:
        in_specs.append(pl.BlockSpec((1, 1, 2, 128, 256), lambda b, s: (b, s, 0, 0, 0)))
        args.append(s0)
    out_specs = [pl.BlockSpec((seq_len, 512), lambda b, s: (b, s))]
    out_shape = [jax.ShapeDtypeStruct((n_seq * seq_len, 1024), BF16)]
    if emit_state:
        out_specs.append(pl.BlockSpec((1, 1, 2, 128, 256), lambda b, s: (b, s, 0, 0, 0)))
        out_shape.append(jax.ShapeDtypeStruct((n_seq, 2, 2, 128, 256), F32))
    gate_rows = pltpu.VMEM((nc * GROWS, LANES), F32)
    res = pl.pallas_call(
        functools.partial(_ssd_kernel, seq_len=seq_len, zero_init=zero_init, emit_state=emit_state),
        grid=(n_seq, 2),
        in_specs=in_specs,
        out_specs=out_specs,
        out_shape=out_shape,
        scratch_shapes=[pltpu.VMEM((nc * 4, 2 * CHUNK, LANES), BF16),
                        pltpu.VMEM((nc * 2, CHUNK, LANES), BF16),
                        pltpu.VMEM((nc * 2, CHUNK, CHUNK), F32),
                        pltpu.VMEM((nc, LANES, CHUNK), F32),
                        pltpu.VMEM((seq_len, 512), F32),
                        pltpu.VMEM((nc, CHUNK, LANES), F32),
                        gate_rows, gate_rows, gate_rows, gate_rows, gate_rows,
                        pltpu.VMEM((128, 256), F32)],
        compiler_params=_cparams(("parallel", "arbitrary")),
        name="ssd_scan",
    )(*args)
    return res if emit_state else (res[0], None)


def _mlstm_kernel(*refs, seq_len, zero_init, emit_state):
    it = iter(refs)
    q_ref, k_ref, v_ref, g_ref, par_ref, nw_ref = (next(it) for _ in range(6))
    s0_ref, m0_ref = (None, None) if zero_init else (next(it), next(it))
    hm_ref = next(it)
    sfin_ref, mfin_ref = (next(it), next(it)) if emit_state else (None, None)
    (qm_s, s_s, kt_s, hacc_s, gt_s, cumt_s, et_s, wendt_s, pmaxt_s, mloct_s, cumc_s, pmaxc_s, st_s, m_s) = (
        next(it) for _ in range(14))

    n_chunks = seq_len // CHUNK
    n_rows = n_chunks * GROWS
    lane = lax.broadcasted_iota(jnp.int32, (CHUNK, LANES), 1)
    lane_lo = lane < 64
    trow = lax.broadcasted_iota(jnp.int32, (CHUNK, CHUNK), 0)
    scol = lax.broadcasted_iota(jnp.int32, (CHUNK, CHUNK), 1)
    ones_blk = jnp.ones((CHUNK, LANES), BF16)
    bias_row = par_ref[0, 0:1, :]
    qscale = ML_DK ** -0.5

    def phase_a(c, carry):
        r0 = pl.multiple_of(c * CHUNK, CHUNK)
        for pr in (0, 1):
            qp = q_ref[pl.ds(r0, CHUNK), pr * LANES:(pr + 1) * LANES].astype(F32)
            kp = k_ref[pl.ds(r0, CHUNK), pr * LANES:(pr + 1) * LANES]
            kt_s[c * 2 + pr] = kp.astype(F32).T
            for hh in (0, 1):
                in_h = lane_lo if hh == 0 else jnp.logical_not(lane_lo)
                qm = jnp.where(in_h, qp * qscale, 0.0).astype(BF16)
                qm_s[c * 4 + pr * 2 + hh] = qm
                s_s[c * 4 + pr * 2 + hh] = _dot_nt(qm, kp)
        gk = g_ref[pl.ds(r0, CHUNK), :] + bias_row
        comb = jnp.where(lane < LANE_F, gk, -_softplus(-gk))
        g0 = pl.multiple_of(c * GROWS, GROWS)
        gt_s[pl.ds(g0, GROWS), :] = comb.T[LANE_I:LANE_I + GROWS, :]
        return carry

    lax.fori_loop(0, n_chunks, phase_a, 0)

    g_all = gt_s[...]
    cf, cr = _lane_cumsums(g_all, _tri2())
    rr = lax.broadcasted_iota(jnp.int32, (n_rows, 1), 0) & (GROWS - 1)
    fwd_row = rr < 12
    cum_t = jnp.where(fwd_row, cf, cr)
    tot_t = jnp.where(fwd_row, cum_t[:, CHUNK - 1:CHUNK], cum_t[:, 0:1])
    e_t = pltpu.roll(g_all, 8, 0) - cum_t
    mloc_t = jnp.max(e_t, axis=1, keepdims=True) + tot_t
    lane_b = lax.broadcasted_iota(jnp.int32, (n_rows, LANES), 1)
    pf, pb = e_t, e_t
    k = 1
    while k < CHUNK:
        pf = jnp.maximum(pf, jnp.where(lane_b >= k, pltpu.roll(pf, k, 1), -jnp.inf))
        pb = jnp.maximum(pb, jnp.where(lane_b < CHUNK - k, pltpu.roll(pb, CHUNK - k, 1), -jnp.inf))
        k *= 2
    cumt_s[...] = cum_t
    et_s[...] = e_t * LOG2E
    wendt_s[...] = jnp.exp(e_t + tot_t - mloc_t)
    pmaxt_s[...] = jnp.where(fwd_row, pf, pb) * LOG2E
    mloct_s[...] = jnp.broadcast_to(mloc_t, e_t.shape)

    def phase_c(c, carry):
        g8 = pl.multiple_of(c * GROWS + 8, 8)
        cumc_s[c] = _pad_rows_t(jnp.concatenate([cumt_s[pl.ds(g8, 8), :], mloct_s[pl.ds(g8, 8), :]], axis=0))
        pmaxc_s[c] = _pad_rows_t(pmaxt_s[pl.ds(g8, 8), :])
        return carry

    lax.fori_loop(0, n_chunks, phase_c, 0)

    for d in (0, 1):
        mask = (scol <= trow) if d == 0 else (scol >= trow)
        if zero_init:
            st_s[...] = jnp.zeros(st_s.shape, F32)
            m_s[...] = jnp.zeros(m_s.shape, F32)
        else:
            st_s[...] = s0_ref[0, 0, d]
            m_s[...] = m0_ref[0, 0]

        def body(ci, carry, d=d, mask=mask):
            c = ci if d == 0 else n_chunks - 1 - ci
            r0 = pl.multiple_of(c * CHUNK, CHUNK)
            cumc = cumc_s[c]
            m_prev = m_s[...]
            m_prev2 = m_prev * LOG2E
            mx = jnp.maximum(m_prev2, pmaxc_s[c])
            negmt = -(cumc * LOG2E + mx)
            tot = cumc[CHUNK - 1:CHUNK, :] if d == 0 else cumc[0:1, :]
            mloc = pltpu.roll(cumc[0:1, :], LANES - 8, 1)
            m_new = jnp.maximum(tot + m_prev, mloc)
            a_old = jnp.exp(tot + m_prev - m_new)
            a_loc = jnp.exp(mloc - m_new)
            for pr in (0, 1):
                stp = st_s[pr]
                stp_bf = stp.astype(BF16)
                kt = kt_s[c * 2 + pr]
                for hh in (0, 1):
                    hl = pr * 2 + hh
                    j = d * 4 + hl
                    c0 = hl * LANES
                    e_row = et_s[pl.ds(c * GROWS + 8 + j, 1), :]
                    wend_row = wendt_s[pl.ds(c * GROWS + 8 + j, 1), :]
                    mxb = jnp.broadcast_to(mx[:, j:j + 1], (CHUNK, LANES))
                    sqk = (s_s[c * 4 + hl] * jnp.exp2(jnp.where(mask, e_row - mxb, -jnp.inf))).astype(BF16)
                    qw = qm_s[c * 4 + hl] * jnp.exp2(m_prev2[:, j:j + 1] - mxb).astype(BF16)
                    vaug = jnp.concatenate([v_ref[pl.ds(r0, CHUNK), c0:c0 + LANES], ones_blk], axis=1)
                    nd = _dot(jnp.concatenate([sqk, qw], axis=1), jnp.concatenate([vaug, stp_bf], axis=0))
                    emt = jnp.exp2(jnp.broadcast_to(negmt[:, j:j + 1], (CHUNK, LANES)))
                    den = jnp.maximum(jnp.abs(nd[:, LANES:2 * LANES]), emt)
                    hout = nd[:, 0:LANES] / den
                    if d == 0:
                        hacc_s[pl.ds(r0, CHUNK), c0:c0 + LANES] = hout
                    else:
                        hacc_s[pl.ds(r0, CHUNK), c0:c0 + LANES] += hout
                    ktw = (kt[hh * 64:(hh + 1) * 64, :] * (wend_row * a_loc[:, j:j + 1])).astype(BF16)
                    st_s[pr, hh * 64:(hh + 1) * 64, :] = (a_old[:, j:j + 1] * stp[hh * 64:(hh + 1) * 64, :]
                                                          + _dot(ktw, vaug))
            m_s[...] = m_new
            return carry

        lax.fori_loop(0, n_chunks, body, 0, unroll=min(4, n_chunks))
        if emit_state:
            sfin_ref[0, 0, d] = st_s[...]
            mfin_ref[0, 0, d:d + 1, :] = m_s[...]

    def phase_e(c, carry):
        r0 = pl.multiple_of(c * CHUNK, CHUNK)
        for hl in range(4):
            c0 = hl * LANES
            hm_ref[pl.ds(r0, CHUNK), c0:c0 + LANES] = _rms(
                hacc_s[pl.ds(r0, CHUNK), c0:c0 + LANES], nw_ref[0, :, c0:c0 + LANES]).astype(hm_ref.dtype)
        return carry

    lax.fori_loop(0, n_chunks, phase_e, 0)


def _mlstm(proj, gates, par, normw, s0, m0, *, row0, n_seq, seq_len, emit_state):
    zero_init = s0 is None
    rb = row0 // seq_len
    nc = seq_len // CHUNK
    in_specs = [
        pl.BlockSpec((seq_len, 256), lambda b, s: (rb + b, COL_Q // 256 + s)),
        pl.BlockSpec((seq_len, 256), lambda b, s: (rb + b, COL_K // 256 + s)),
        pl.BlockSpec((seq_len, 512), lambda b, s: (rb + b, COL_V // 512 + s)),
        pl.BlockSpec((seq_len, LANES), lambda b, s: (rb + b, s)),
        pl.BlockSpec((1, 8, LANES), lambda b, s: (s, 0, 0)),
        pl.BlockSpec((1, 1, 512), lambda b, s: (s, 0, 0)),
    ]
    args = [proj, proj, proj, gates, par, normw]
    if not zero_init:
        in_specs.append(pl.BlockSpec((1, 1, 2, 2, 128, 256), lambda b, s: (b, s, 0, 0, 0, 0)))
        in_specs.append(pl.BlockSpec((1, 1, 1, LANES), lambda b, s: (b, s, 0, 0)))
        args += [s0, m0]
    out_specs = [pl.BlockSpec((seq_len, 512), lambda b, s: (b, s))]
    out_shape = [jax.ShapeDtypeStruct((n_seq * seq_len, 1024), BF16)]
    if emit_state:
        out_specs.append(pl.BlockSpec((1, 1, 2, 2, 128, 256), lambda b, s: (b, s, 0, 0, 0, 0)))
        out_shape.append(jax.ShapeDtypeStruct((n_seq, 2, 2, 2, 128, 256), F32))
        out_specs.append(pl.BlockSpec((1, 1, 2, LANES), lambda b, s: (b, s, 0, 0)))
        out_shape.append(jax.ShapeDtypeStruct((n_seq, 2, 2, LANES), F32))
    gate_rows = pltpu.VMEM((nc * GROWS, LANES), F32)
    res = pl.pallas_call(
        functools.partial(_mlstm_kernel, seq_len=seq_len, zero_init=zero_init, emit_state=emit_state),
        grid=(n_seq, 2),
        in_specs=in_specs,
        out_specs=out_specs,
        out_shape=out_shape,
        scratch_shapes=[pltpu.VMEM((nc * 4, CHUNK, LANES), BF16),
                        pltpu.VMEM((nc * 4, CHUNK, CHUNK), F32),
                        pltpu.VMEM((nc * 2, LANES, CHUNK), F32),
                        pltpu.VMEM((seq_len, 512), F32),
                        gate_rows, gate_rows, gate_rows, gate_rows, gate_rows, gate_rows,
                        pltpu.VMEM((nc, CHUNK, LANES), F32),
                        pltpu.VMEM((nc, CHUNK, LANES), F32),
                        pltpu.VMEM((2, 128, 256), F32),
                        pltpu.VMEM((1, LANES), F32)],
        compiler_params=_cparams(("parallel", "arbitrary")),
        name="mlstm_scan",
    )(*args)
    return res if emit_state else (res[0], None, None)


def _outproj_kernel(yc_ref, yl_ref, z_ref, hc_ref, hl_ref, o_ref, gs_ref, gm_ref, ha_ref, hb_ref, wos_ref,
                    wom_ref, wout_ref, sn_ref, nf_ref, g1_ref, sc2_ref, sh2_ref, hout_ref, u2_ref, *,
                    n_ctx_tiles):
    is_ctx = pl.program_id(0) < n_ctx_tiles
    y_in = jnp.where(is_ctx, yc_ref[...], yl_ref[...]).astype(F32)
    hm_in = jnp.where(is_ctx, hc_ref[...], hl_ref[...]).astype(F32)
    y = y_in * _silu(z_ref[...].astype(F32))
    y = _rms(y, sn_ref[...]).astype(BF16)
    y_ssd = _dot(y, wos_ref[...])
    hm = (hm_in * jax.nn.sigmoid(o_ref[...].astype(F32))).astype(BF16)
    y_ml = _dot(hm, wom_ref[...])
    mix = (jax.nn.sigmoid(gs_ref[...].astype(F32)) * y_ssd
           + jax.nn.sigmoid(gm_ref[...].astype(F32)) * y_ml).astype(BF16)
    h = jnp.where(is_ctx, ha_ref[...], hb_ref[...]) + g1_ref[0] * _dot(mix, wout_ref[...])
    hout_ref[...] = h
    u2_ref[...] = (_rms(h, nf_ref[...]) * (1.0 + sc2_ref[0]) + sh2_ref[0]).astype(u2_ref.dtype)


def _outproj(y_ctx, y_lat, hm_ctx, hm_lat, proj, rows, wos, wom, wout, ssd_norm, norm_ffn, mod48, lat_len,
             tm=512):
    t, d = proj.shape[0], wos.shape[0]
    n_ctx_rows = rows.n_ctx_rows
    nct = n_ctx_rows // tm
    mi = functools.partial(_mod_index, tm=tm, n_ctx_rows=n_ctx_rows, lat_len=lat_len)
    row = lambda i: (i, 0)
    ctx_row = lambda i: (jnp.minimum(i, nct - 1), 0)
    lat_row = lambda i: (jnp.maximum(i - nct, 0), 0)
    const = lambda i: (0, 0)
    col = lambda k: (lambda i: (i, k))
    modspec = lambda which: pl.BlockSpec((1, 1, d), lambda i: (mi(i, which=which), 0, 0))
    return pl.pallas_call(
        functools.partial(_outproj_kernel, n_ctx_tiles=nct),
        grid=(t // tm,),
        in_specs=[pl.BlockSpec((tm, d), ctx_row), pl.BlockSpec((tm, d), lat_row),
                  pl.BlockSpec((tm, d), col(COL_Z // d)),
                  pl.BlockSpec((tm, d), ctx_row), pl.BlockSpec((tm, d), lat_row),
                  pl.BlockSpec((tm, d), col(COL_O // d)),
                  pl.BlockSpec((tm, d), col(COL_GS // d)), pl.BlockSpec((tm, d), col(COL_GM // d))]
        + rows.specs(tm, d) + [
                  pl.BlockSpec((d, d), const), pl.BlockSpec((d, d), const), pl.BlockSpec((d, d), const),
                  pl.BlockSpec((1, d), const), pl.BlockSpec((1, d), const),
                  modspec(2), modspec(4), modspec(3)],
        out_specs=[pl.BlockSpec((tm, d), row), pl.BlockSpec((tm, d), row)],
        out_shape=[jax.ShapeDtypeStruct((t, d), F32), jax.ShapeDtypeStruct((t, d), BF16)],
        compiler_params=_cparams(("parallel",)),
        name="merge_outproj",
    )(y_ctx, y_lat, proj, hm_ctx, hm_lat, proj, proj, proj, rows.ctx, rows.lat, wos, wom, wout,
      ssd_norm.reshape(1, d), norm_ffn.reshape(1, d), mod48, mod48, mod48)


FFN_TM = 512
FFN_TF = 256


def _ffn_kernel(u_ref, ut_ref, ub_ref, wup_ref, cw_ref, wd_ref, h_ref, g2_ref, o_ref, act_s, *,
                n_ctx_tiles, ctx_len, tiles_per_seq):
    i = pl.program_id(0)
    tm, tf = FFN_TM, FFN_TF
    n_chunks = D_FF // tf
    ext = tm + 2 * GRID_W

    def cw(r, j):
        return cw_ref[r:r + 1, j * tf:(j + 1) * tf]

    @pl.when(i < n_ctx_tiles)
    def _():
        u = u_ref[...]
        pos = lax.broadcasted_iota(jnp.int32, (tm, tf), 0) & (ctx_len - 1)
        for j in range(n_chunks):
            g = _dot(u, wup_ref[:, j * tf:(j + 1) * tf])
            val = _dot(u, wup_ref[:, D_FF + j * tf:D_FF + (j + 1) * tf])
            gl = jnp.where(pos != 0, pltpu.roll(g, 1, 0), 0.0)
            gr = jnp.where(pos != ctx_len - 1, pltpu.roll(g, tm - 1, 0), 0.0)
            conv = gl * cw(3, j) + g * cw(4, j) + gr * cw(5, j) + cw(9, j)
            act_s[:, j * tf:(j + 1) * tf] = (_silu(conv) * val).astype(BF16)

    @pl.when(i >= n_ctx_tiles)
    def _():
        ti = (i - n_ctx_tiles) % tiles_per_seq
        u = u_ref[...]
        top = jnp.where(ti > 0, ut_ref[...], jnp.zeros_like(ut_ref[...]))
        bot = jnp.where(ti < tiles_per_seq - 1, ub_ref[...], jnp.zeros_like(ub_ref[...]))
        uext = jnp.concatenate([top, u, bot], axis=0)
        col = lax.broadcasted_iota(jnp.int32, (ext, tf), 0) & (GRID_W - 1)
        for j in range(n_chunks):
            gx = _dot(uext, wup_ref[:, j * tf:(j + 1) * tf])
            val = _dot(u, wup_ref[:, D_FF + j * tf:D_FF + (j + 1) * tf])
            gl = jnp.where(col != 0, pltpu.roll(gx, 1, 0), 0.0)
            gr = jnp.where(col != GRID_W - 1, pltpu.roll(gx, ext - 1, 0), 0.0)

            def taps(r, lo):
                return (gl[lo:lo + tm] * cw(3 * r, j) + gx[lo:lo + tm] * cw(3 * r + 1, j)
                        + gr[lo:lo + tm] * cw(3 * r + 2, j))

            conv = taps(1, GRID_W) + taps(0, 0) + taps(2, 2 * GRID_W) + cw(9, j)
            act_s[:, j * tf:(j + 1) * tf] = (_silu(conv) * val).astype(BF16)

    o_ref[...] = h_ref[...] + g2_ref[0] * _dot(act_s[...], wd_ref[...])


def _ffn(u2, wup, cw16, wd, h, mod48, n_ctx_rows, ctx_len, lat_len):
    t, d = h.shape
    tm = FFN_TM
    hb = tm // GRID_W
    n_hblocks = t // GRID_W
    mi = functools.partial(_mod_index, tm=tm, n_ctx_rows=n_ctx_rows, lat_len=lat_len)
    resident = lambda shape: pl.BlockSpec(shape, lambda i: (0, 0), pipeline_mode=pl.Buffered(1))
    return pl.pallas_call(
        functools.partial(_ffn_kernel, n_ctx_tiles=n_ctx_rows // tm, ctx_len=ctx_len,
                          tiles_per_seq=lat_len // tm),
        grid=(t // tm,),
        in_specs=[pl.BlockSpec((tm, d), lambda i: (i, 0)),
                  pl.BlockSpec((GRID_W, d), lambda i: (jnp.maximum(i * hb - 1, 0), 0)),
                  pl.BlockSpec((GRID_W, d), lambda i: (jnp.minimum((i + 1) * hb, n_hblocks - 1), 0)),
                  resident((d, 2 * D_FF)),
                  resident((16, D_FF)),
                  resident((D_FF, d)),
                  pl.BlockSpec((tm, d), lambda i: (i, 0)),
                  pl.BlockSpec((1, 1, d), lambda i: (mi(i, which=5), 0, 0))],
        out_specs=pl.BlockSpec((tm, d), lambda i: (i, 0)),
        out_shape=jax.ShapeDtypeStruct((t, d), F32),
        scratch_shapes=[pltpu.VMEM((tm, D_FF), BF16)],
        compiler_params=_cparams(("parallel",)),
        name="convffn",
    )(u2, u2, u2, wup, cw16, wd, h, mod48)


def _arrange_w_in(w):
    z, xbc, dt, q, k, v, o, ig, fg, gates = (
        w[:, 0:1024], w[:, 1024:2560], w[:, 2560:2592], w[:, 2592:3104], w[:, 3104:3616],
        w[:, 3616:4640], w[:, 4640:5664], w[:, 5664:5680], w[:, 5680:5696], w[:, 5696:7744])
    big = jnp.concatenate([z, o, gates, xbc, q, k, v], axis=1).astype(BF16)
    blocks = []
    for s in (0, 1):
        blocks += [dt[:, s * 8:s * 8 + 8], dt[:, 16 + s * 8:16 + s * 8 + 8],
                   ig[:, s * 4:s * 4 + 4], ig[:, 8 + s * 4:8 + s * 4 + 4],
                   fg[:, s * 4:s * 4 + 4], fg[:, 8 + s * 4:8 + s * 4 + 4],
                   jnp.zeros((w.shape[0], LANES - 32), w.dtype)]
    small = jnp.concatenate(blocks, axis=1).astype(BF16)
    return big, small


def _slab_rows(parts, n_rows=8):
    out = jnp.zeros((2, n_rows, LANES), F32)
    for r, lane0, val in parts:
        out = out.at[:, r, lane0:lane0 + val.shape[1]].set(val)
    return out


def _ssd_params(dt_bias, a_log, d_skip, conv_w, conv_b):
    par = _slab_rows([(0, 0, dt_bias[0].reshape(2, 8)), (0, 8, dt_bias[1].reshape(2, 8)),
                      (1, 0, a_log[0].reshape(2, 8)), (1, 8, a_log[1].reshape(2, 8))])
    dvec = jnp.repeat(d_skip, SSD_HEADDIM).reshape(2, 1, 512)
    cw = jnp.concatenate([conv_w, conv_b[None], jnp.zeros((4, conv_w.shape[1]), F32)], axis=0)
    cwx = cw[:, 0:1024].reshape(8, 2, 512).transpose(1, 0, 2)
    cwb = cw[:, 1024:1280].reshape(8, 2, LANES).transpose(1, 0, 2)
    cwc = cw[:, 1280:1536].reshape(8, 2, LANES).transpose(1, 0, 2)
    return par, dvec, cwx, cwb, cwc


def _ml_params(i_bias, f_bias, ml_norm):
    par = _slab_rows([(0, LANE_I, i_bias[0].reshape(2, 4)), (0, LANE_I + 4, i_bias[1].reshape(2, 4)),
                      (0, LANE_F, f_bias[0].reshape(2, 4)), (0, LANE_F + 4, f_bias[1].reshape(2, 4))])
    return par, ml_norm.reshape(2, 1, 512)


def _ssd_state_to_kernel(s):
    b = s.shape[0]
    s = s.reshape(b, 2, 2, 2, 4, 64, 64).transpose(0, 2, 1, 3, 6, 4, 5)
    return s.reshape(b, 2, 2, 128, 256)


def _ssd_state_from_kernel(s):
    b = s.shape[0]
    s = s.reshape(b, 2, 2, 2, 64, 4, 64).transpose(0, 2, 1, 3, 5, 6, 4)
    return s.reshape(b, 2, SSD_HEADS, SSD_HEADDIM, SSD_STATE)


def _ml_state_to_kernel(c0, n0, m0):
    b = c0.shape[0]
    aug = jnp.concatenate([jnp.swapaxes(c0, -1, -2),
                           jnp.broadcast_to(n0[..., None], n0.shape + (LANES,))], axis=-1)
    aug = aug.reshape(b, 2, 2, 2, 2, 64, 256).transpose(0, 2, 1, 3, 4, 5, 6).reshape(b, 2, 2, 2, 128, 256)
    m = m0.reshape(b, 2, 2, 4).transpose(0, 2, 1, 3).reshape(b, 2, 1, 8)
    mk = jnp.zeros((b, 2, 1, LANES), F32).at[..., 0:8].set(m)
    return aug, mk


def _ml_state_from_kernel(st, mfin):
    b = st.shape[0]
    st = st.reshape(b, 2, 2, 2, 2, 64, 256).transpose(0, 2, 1, 3, 4, 5, 6).reshape(b, 2, ML_HEADS, 64, 256)
    c = jnp.swapaxes(st[..., 0:ML_DV], -1, -2)
    n = st[..., ML_DV]
    m = jnp.stack([mfin[:, :, 0, 0:4], mfin[:, :, 1, 4:8]], axis=1)
    return c, n, m.reshape(b, 2, ML_HEADS)


def kernel(x_prompt, x_sample, state_ssd, state_mlstm_C, state_mlstm_n, state_mlstm_m, c, c_ctx, w_ada, b_ada,
           norm_mix, w_in, ssd_conv_w, ssd_conv_b, ssd_dt_bias, ssd_a_log, ssd_d, ssd_norm, w_o_ssd, ml_i_bias,
           ml_f_bias, ml_norm, w_o_ml, w_out, norm_ffn, w_up, ffn_conv_w, ffn_conv_b, w_down, final_norm):
    nb, ctx_len, d = x_prompt.shape
    nlat, lat_len, _ = x_sample.shape
    depth = w_in.shape[0]
    n_ctx_rows = nb * ctx_len
    n_lat_rows = nlat * lat_len

    n_rows = n_ctx_rows + n_lat_rows
    rows = _Rows(x_prompt.reshape(n_ctx_rows, d), x_sample.reshape(n_lat_rows, d), n_ctx_rows)
    cond8 = jnp.concatenate([c_ctx[None], c, jnp.zeros((8 - 1 - nlat, d), F32)], axis=0)
    mod = _ada(cond8, w_ada, b_ada)

    ssd_l, c_l, n_l, m_l = [], [], [], []
    for l in range(depth):
        mod48 = mod[l].reshape(48, 1, d)
        w_big, w_small = _arrange_w_in(w_in[l])
        par_s, dvec, cwx, cwb, cwc = _ssd_params(ssd_dt_bias[l], ssd_a_log[l], ssd_d[l], ssd_conv_w[l],
                                                 ssd_conv_b[l])
        par_m, normw = _ml_params(ml_i_bias[l], ml_f_bias[l], ml_norm[l])

        proj, gates = _in_proj(rows, norm_mix[l], mod48, w_big, w_small, n_rows, lat_len)

        y_ctx, s_ctx = _ssd(proj, gates, par_s, dvec, cwx, cwb, cwc, None,
                            row0=0, n_seq=nb, seq_len=ctx_len, emit_state=True)
        y_lat, _ = _ssd(proj, gates, par_s, dvec, cwx, cwb, cwc, _ssd_state_to_kernel(state_ssd[:, l]),
                        row0=n_ctx_rows, n_seq=nlat, seq_len=lat_len, emit_state=False)
        hm_ctx, st_ctx, m_ctx = _mlstm(proj, gates, par_m, normw, None, None,
                                       row0=0, n_seq=nb, seq_len=ctx_len, emit_state=True)
        s0m, m0m = _ml_state_to_kernel(state_mlstm_C[:, l], state_mlstm_n[:, l], state_mlstm_m[:, l])
        hm_lat, _, _ = _mlstm(proj, gates, par_m, normw, s0m, m0m,
                              row0=n_ctx_rows, n_seq=nlat, seq_len=lat_len, emit_state=False)
        h, u2 = _outproj(y_ctx, y_lat, hm_ctx, hm_lat, proj, rows, w_o_ssd[l].astype(BF16),
                         w_o_ml[l].astype(BF16), w_out[l].astype(BF16), ssd_norm[l], norm_ffn[l], mod48, lat_len)

        cw16 = jnp.concatenate([ffn_conv_w[l].reshape(9, D_FF), ffn_conv_b[l][None],
                                jnp.zeros((6, D_FF), F32)], axis=0)
        h = _ffn(u2, w_up[l].astype(BF16), cw16, w_down[l].astype(BF16), h, mod48, n_ctx_rows, ctx_len, lat_len)
        rows = _Rows(h, h, n_ctx_rows)

        ssd_l.append(_ssd_state_from_kernel(s_ctx))
        c_new, n_new, m_new = _ml_state_from_kernel(st_ctx, m_ctx)
        c_l.append(c_new)
        n_l.append(n_new)
        m_l.append(m_new)

    y_prompt = _final_norm(h, final_norm, 0, n_ctx_rows).reshape(nb, ctx_len, d)
    y_sample = _final_norm(h, final_norm, n_ctx_rows, n_lat_rows).reshape(nlat, lat_len, d)
    return (y_prompt, y_sample, jnp.stack(ssd_l, axis=1), jnp.stack(c_l, axis=1), jnp.stack(n_l, axis=1),
            jnp.stack(m_l, axis=1))
```

```python
import functools

import jax
import jax.numpy as jnp
import numpy as np
from jax import lax
from jax.experimental import pallas as pl
from jax.experimental.pallas import tpu as pltpu

F32 = jnp.float32
BF16 = jnp.bfloat16

D_MODEL = 1024
CHUNK = 128
EPS = 1e-6
GRID_W = 64
SSD_HEADS = 16
SSD_HEADDIM = 64
SSD_STATE = 64
SSD_GROUPS = 4
ML_HEADS = 8
ML_DV = 128
ML_DK = 64
D_FF = 2816

LANES = 128
BF16_SUBLANES = 16
VMEM_LIMIT = 56 * 1024 * 1024

COL_Z, COL_O, COL_GS, COL_GM = 0, 1024, 2048, 3072
COL_X, COL_B, COL_C = 4096, 5120, 5376
COL_Q, COL_K, COL_V = 5632, 6144, 6656
PROJ_COLS = 7680
GATE_COLS = 256
LANE_I = 16
LANE_F = 24
GROWS = 16
LOG2E = 1.4426950408889634
CTX_SEQS_PER_STEP = 4


def _cparams(sem):
    return pltpu.CompilerParams(dimension_semantics=sem, vmem_limit_bytes=VMEM_LIMIT)


def _silu(x):
    return x * jax.nn.sigmoid(x)


def _softplus(x):
    return jnp.maximum(x, 0.0) + jnp.log1p(jnp.exp(-jnp.abs(x)))


def _dot(a, b):
    return jnp.dot(a, b, preferred_element_type=F32)


def _dot_nt(a, b):
    return lax.dot_general(a, b, (((1,), (1,)), ((), ())), preferred_element_type=F32)


def _pair(lane_lo, col0, col1):
    return jnp.where(lane_lo, col0, col1)


def _tri2():
    s = lax.broadcasted_iota(jnp.int32, (CHUNK, 2 * CHUNK), 0)
    t = lax.broadcasted_iota(jnp.int32, (CHUNK, 2 * CHUNK), 1)
    keep = ((t < CHUNK) & (s <= t)) | ((t >= CHUNK) & (s >= t - CHUNK))
    return jnp.where(keep, 1.0, 0.0).astype(BF16)


def _lane_cumsums(x, tri2):
    hi = x.astype(BF16)
    r1 = x - hi.astype(F32)
    mid = r1.astype(BF16)
    lo = (r1 - mid.astype(F32)).astype(BF16)
    cs = _dot(hi, tri2) + _dot(mid, tri2) + _dot(lo, tri2)
    return cs[:, 0:CHUNK], cs[:, CHUNK:2 * CHUNK]


def _pad_rows_t(x):
    return jnp.concatenate([x, jnp.zeros((CHUNK - x.shape[0], LANES), F32)], axis=0).T


def _ada_kernel(c_ref, w_ref, b_ref, o_ref):
    cond = _silu(c_ref[...]).astype(BF16)
    o_ref[0] = _dot(cond, w_ref[0].astype(BF16)) + b_ref[0]


def _ada(cond8, w_ada, b_ada):
    depth, d, n = w_ada.shape
    tn = 1536
    return pl.pallas_call(
        _ada_kernel,
        grid=(depth, n // tn),
        in_specs=[pl.BlockSpec((8, d), lambda l, j: (0, 0)),
                  pl.BlockSpec((1, d, tn), lambda l, j: (l, 0, j)),
                  pl.BlockSpec((1, 1, tn), lambda l, j: (l, 0, j))],
        out_specs=pl.BlockSpec((1, 8, tn), lambda l, j: (l, 0, j)),
        out_shape=jax.ShapeDtypeStruct((depth, 8, n), F32),
        compiler_params=_cparams(("arbitrary", "arbitrary")),
        name="ada_mod",
    )(cond8, w_ada, b_ada.reshape(depth, 1, n))


def _mod_index(i, tm, n_ctx_rows, lat_len, which):
    n_ctx_tiles = n_ctx_rows // tm
    tiles_per_seq = lat_len // tm
    row = jnp.where(i < n_ctx_tiles, 0, 1 + (i - n_ctx_tiles) // tiles_per_seq)
    return row * 6 + which


def _rms(x, w):
    return (x * lax.rsqrt(jnp.mean(x * x, axis=-1, keepdims=True) + EPS)) * w


class _Rows:
    def __init__(self, ctx, lat, n_ctx_rows):
        self.ctx, self.lat, self.n_ctx_rows = ctx, lat, n_ctx_rows
        self.lat_row0 = n_ctx_rows if lat is ctx else 0

    def specs(self, tm, d):
        nct = self.n_ctx_rows // tm
        off = self.lat_row0 // tm
        return [pl.BlockSpec((tm, d), lambda i, *_: (jnp.minimum(i, nct - 1), 0)),
                pl.BlockSpec((tm, d), lambda i, *_: (jnp.maximum(i - nct, 0) + off, 0))]


IN_TM = 1024
IN_TN = 1536


def _in_proj_kernel(ha_ref, hb_ref, nw_ref, sc_ref, sh_ref, w_ref, wg_ref, proj_ref, gate_ref, u_s, *,
                    n_ctx_tiles, n_col_tiles):
    i = pl.program_id(0)
    j = pl.program_id(1)

    @pl.when(j == 0)
    def _():
        x = jnp.where(i < n_ctx_tiles, ha_ref[...], hb_ref[...])
        u_s[...] = (_rms(x, nw_ref[...]) * (1.0 + sc_ref[0]) + sh_ref[0]).astype(u_s.dtype)

    @pl.when(j < n_col_tiles)
    def _():
        proj_ref[...] = _dot(u_s[...], w_ref[...]).astype(proj_ref.dtype)

    @pl.when(j == n_col_tiles)
    def _():
        gate_ref[...] = _dot(u_s[...], wg_ref[...])


def _in_proj(rows, w, mod48, w_big, w_small, n_rows, lat_len):
    d = w.shape[0]
    tm, tn = IN_TM, IN_TN
    nj = PROJ_COLS // tn
    mi = functools.partial(_mod_index, tm=tm, n_ctx_rows=rows.n_ctx_rows, lat_len=lat_len)
    return pl.pallas_call(
        functools.partial(_in_proj_kernel, n_ctx_tiles=rows.n_ctx_rows // tm, n_col_tiles=nj),
        grid=(n_rows // tm, nj + 1),
        in_specs=rows.specs(tm, d) + [
            pl.BlockSpec((1, d), lambda i, j: (0, 0)),
            pl.BlockSpec((1, 1, d), lambda i, j: (mi(i, which=1), 0, 0)),
            pl.BlockSpec((1, 1, d), lambda i, j: (mi(i, which=0), 0, 0)),
            pl.BlockSpec((d, tn), lambda i, j: (0, jnp.minimum(j, nj - 1))),
            pl.BlockSpec((d, GATE_COLS), lambda i, j: (0, 0))],
        out_specs=[pl.BlockSpec((tm, tn), lambda i, j: (i, jnp.minimum(j, nj - 1))),
                   pl.BlockSpec((tm, GATE_COLS), lambda i, j: (i, 0))],
        out_shape=[jax.ShapeDtypeStruct((n_rows, PROJ_COLS), BF16),
                   jax.ShapeDtypeStruct((n_rows, GATE_COLS), F32)],
        scratch_shapes=[pltpu.VMEM((tm, d), BF16)],
        compiler_params=_cparams(("parallel", "arbitrary")),
        name="in_proj",
    )(rows.ctx, rows.lat, w.reshape(1, d), mod48, mod48, w_big, w_small)


def _norm_kernel(x_ref, w_ref, o_ref):
    o_ref[...] = _rms(x_ref[...], w_ref[...])


def _final_norm(h, w, row0, rows, tm=512):
    d = h.shape[1]
    off = row0 // tm
    return pl.pallas_call(
        _norm_kernel,
        grid=(rows // tm,),
        in_specs=[pl.BlockSpec((tm, d), lambda i: (i + off, 0)),
                  pl.BlockSpec((1, d), lambda i: (0, 0))],
        out_specs=pl.BlockSpec((tm, d), lambda i: (i, 0)),
        out_shape=jax.ShapeDtypeStruct((rows, d), F32),
        compiler_params=_cparams(("parallel",)),
        name="final_norm",
    )(h, w.reshape(1, d))


def _conv_silu_chunk(src_ref, cw_ref, c, cps, blk_len):
    r0 = pl.multiple_of(c * CHUNK, CHUNK)
    cur = src_ref[pl.ds(r0, CHUNK), :].astype(F32)
    p0 = pl.multiple_of(jnp.maximum(r0 - BF16_SUBLANES, 0), BF16_SUBLANES)
    n0 = pl.multiple_of(jnp.minimum(r0 + CHUNK, blk_len - BF16_SUBLANES), BF16_SUBLANES)
    prv = src_ref[pl.ds(p0, BF16_SUBLANES), :].astype(F32)[BF16_SUBLANES - 1:BF16_SUBLANES, :]
    nxt = src_ref[pl.ds(n0, BF16_SUBLANES), :].astype(F32)[0:1, :]
    cs = c % cps
    prv = jnp.where(cs > 0, prv, 0.0)
    nxt = jnp.where(cs < cps - 1, nxt, 0.0)
    row = lax.broadcasted_iota(jnp.int32, cur.shape, 0)
    up = jnp.where(row == 0, prv, pltpu.roll(cur, 1, 0))
    dn = jnp.where(row == CHUNK - 1, nxt, pltpu.roll(cur, CHUNK - 1, 0))
    y = up * cw_ref[0, 0:1, :] + cur * cw_ref[0, 1:2, :] + dn * cw_ref[0, 2:3, :] + cw_ref[0, 3:4, :]
    return _silu(y)


def _zero_other_layers(ref, n_sub):
    z = jnp.zeros(ref.shape[-2:], F32)
    for sub in range(n_sub):
        for layer in range(1, ref.shape[1]):
            for d in range(ref.shape[2]):
                for h in range(ref.shape[3]):
                    ref[sub, layer, d, h] = z


def _ssd_kernel(*refs, blk_len, seq_len, zero_init, state_out):
    it = iter(refs)
    x_ref, b_ref, c_ref, g_ref = next(it), next(it), next(it), next(it)
    par_ref, d_ref, cwx_ref, cwb_ref, cwc_ref = next(it), next(it), next(it), next(it), next(it)
    s0_ref = None if zero_init else next(it)
    if state_out == 'next':
        next(it)
    y_ref = next(it)
    sfin_ref = next(it) if state_out else None
    (xbd_s, cm_s, g_s, bt_s, yacc_s, cumc_s, at_s, dtt_s, cumt_s, dtet_s, cdt_s, st_s) = (
        next(it) for _ in range(12))

    n_chunks = blk_len // CHUNK
    cps = seq_len // CHUNK
    n_sub = blk_len // seq_len
    lane = lax.broadcasted_iota(jnp.int32, (CHUNK, LANES), 1)
    lane_lo = lane < 64
    lane_row_lo = lax.broadcasted_iota(jnp.int32, (1, LANES), 1) < 64
    trow = lax.broadcasted_iota(jnp.int32, (CHUNK, CHUNK), 0)
    scol = lax.broadcasted_iota(jnp.int32, (CHUNK, CHUNK), 1)
    bias_row = par_ref[0, 0:1, :]
    a_row = -jnp.exp(par_ref[0, 1:2, :])

    def phase_a(c, carry):
        r0 = pl.multiple_of(c * CHUNK, CHUNK)
        xck = _conv_silu_chunk(x_ref, cwx_ref, c, cps, blk_len)
        bck = _conv_silu_chunk(b_ref, cwb_ref, c, cps, blk_len)
        cck = _conv_silu_chunk(c_ref, cwc_ref, c, cps, blk_len)
        yacc_s[pl.ds(r0, CHUNK), :] = d_ref[0] * xck
        for p in range(4):
            xp = xck[:, p * LANES:(p + 1) * LANES]
            xbd_s[c * 4 + p] = jnp.concatenate(
                [jnp.where(lane_lo, xp, 0.0), jnp.where(lane_lo, 0.0, xp)], axis=0).astype(BF16)
        bbf = bck.astype(BF16)
        for g2 in (0, 1):
            in_g = lane_lo if g2 == 0 else jnp.logical_not(lane_lo)
            cm = jnp.where(in_g, cck, 0.0).astype(BF16)
            cm_s[c * 2 + g2] = cm
            g_s[c * 2 + g2] = _dot_nt(cm, bbf)
        bt_s[c] = bck.T
        dt = _softplus(g_ref[pl.ds(r0, CHUNK), :] + bias_row)
        g0 = pl.multiple_of(c * GROWS, GROWS)
        at_s[pl.ds(g0, GROWS), :] = (dt * a_row).T[0:GROWS, :]
        dtt_s[pl.ds(g0, GROWS), :] = dt.T[0:GROWS, :]
        return carry

    lax.fori_loop(0, n_chunks, phase_a, 0)

    cf, cr = _lane_cumsums(at_s[...], _tri2())
    fwd_row = (lax.broadcasted_iota(jnp.int32, (n_chunks * GROWS, 1), 0) & (GROWS - 1)) < 8
    cum_t = jnp.where(fwd_row, cf, cr)
    tot_t = jnp.where(fwd_row, cum_t[:, CHUNK - 1:CHUNK], cum_t[:, 0:1])
    cumt_s[...] = cum_t * LOG2E
    dtet_s[...] = dtt_s[...] * jnp.exp(tot_t - cum_t)
    cdt_s[...] = jnp.broadcast_to(jnp.exp(tot_t), cum_t.shape)

    def phase_c(c, carry):
        g0 = pl.multiple_of(c * GROWS, GROWS)
        cumc_s[c] = _pad_rows_t(cumt_s[pl.ds(g0, GROWS), :])
        return carry

    lax.fori_loop(0, n_chunks, phase_c, 0)

    def scan_direction(sub, d):
        mask = (scol <= trow) if d == 0 else (scol >= trow)
        if zero_init:
            st_s[...] = jnp.zeros(st_s.shape, F32)
        else:
            st_s[...] = s0_ref[sub, 0, d]

        def body(ci, carry):
            c = sub * cps + (ci if d == 0 else cps - 1 - ci)
            r0 = pl.multiple_of(c * CHUNK, CHUNK)
            cumc = cumc_s[c]
            st = st_s[...]
            st_bf = st.astype(BF16)

            def grow(ref, j):
                return ref[pl.ds(c * GROWS + j, 1), :]

            for g2 in (0, 1):
                yoff = _dot(cm_s[c * 2 + g2], st_bf)
                gmat = g_s[c * 2 + g2]
                bt_g = bt_s[c, g2 * 64:(g2 + 1) * 64, :]
                for pr in (0, 1):
                    p = g2 * 2 + pr
                    j0 = d * 8 + g2 * 4 + pr * 2
                    j1 = j0 + 1
                    cb0 = jnp.broadcast_to(cumc[:, j0:j0 + 1], (CHUNK, LANES))
                    cb1 = jnp.broadcast_to(cumc[:, j1:j1 + 1], (CHUNK, LANES))
                    m0 = gmat * jnp.exp2(jnp.where(mask, cb0 - grow(cumt_s, j0), -jnp.inf))
                    m1 = gmat * jnp.exp2(jnp.where(mask, cb1 - grow(cumt_s, j1), -jnp.inf))
                    mcat = jnp.concatenate([(m0 * grow(dtt_s, j0)).astype(BF16),
                                            (m1 * grow(dtt_s, j1)).astype(BF16)], axis=1)
                    xbd = xbd_s[c * 4 + p]
                    yp = _dot(mcat, xbd) + (jnp.exp2(jnp.where(lane_lo, cb0, cb1))
                                            * yoff[:, pr * LANES:(pr + 1) * LANES])
                    c0 = p * LANES
                    if d == 0:
                        yacc_s[pl.ds(r0, CHUNK), c0:c0 + LANES] += yp
                    else:
                        y_ref[pl.ds(r0, CHUNK), c0:c0 + LANES] = (
                            yacc_s[pl.ds(r0, CHUNK), c0:c0 + LANES] + yp).astype(y_ref.dtype)
                    lhs = jnp.concatenate([(bt_g * grow(dtet_s, j0)).astype(BF16),
                                           (bt_g * grow(dtet_s, j1)).astype(BF16)], axis=1)
                    cd = jnp.where(lane_row_lo, grow(cdt_s, j0), grow(cdt_s, j1))
                    st_s[g2 * 64:(g2 + 1) * 64, pr * LANES:(pr + 1) * LANES] = (
                        cd * st[g2 * 64:(g2 + 1) * 64, pr * LANES:(pr + 1) * LANES] + _dot(lhs, xbd))
            return carry

        lax.fori_loop(0, cps, body, 0, unroll=min(4, cps))
        if state_out:
            for half in (0, 1):
                t = st_s[:, half * LANES:(half + 1) * LANES].T
                t_hi = pltpu.roll(t, 64, 1)
                for hloc in (0, 1):
                    hl = half * 2 + hloc
                    sfin_ref[sub, 0, d, hl] = t[hloc * 64:(hloc + 1) * 64, 0:64]
                    sfin_ref[sub, 0, d, 4 + hl] = t_hi[hloc * 64:(hloc + 1) * 64, 0:64]

    if state_out == 'first':
        _zero_other_layers(sfin_ref, n_sub)

    def scan_sequence(sub, carry):
        scan_direction(sub, 0)
        scan_direction(sub, 1)
        return carry

    lax.fori_loop(0, n_sub, scan_sequence, 0)


def _ssd(proj, gates, par, dvec, cwx, cwb, cwc, s0, *, row0, n_seq, seq_len, n_sub=1, state_layer=None,
         state_prev=None, depth=1):
    zero_init = s0 is None
    state_out = None if state_layer is None else ('first' if state_layer == 0 else 'next')
    blk_len = n_sub * seq_len
    rb = row0 // blk_len
    nc = blk_len // CHUNK
    in_specs = [
        pl.BlockSpec((blk_len, 512), lambda b, s: (rb + b, COL_X // 512 + s)),
        pl.BlockSpec((blk_len, LANES), lambda b, s: (rb + b, COL_B // LANES + s)),
        pl.BlockSpec((blk_len, LANES), lambda b, s: (rb + b, COL_C // LANES + s)),
        pl.BlockSpec((blk_len, LANES), lambda b, s: (rb + b, s)),
        pl.BlockSpec((1, 8, LANES), lambda b, s: (s, 0, 0)),
        pl.BlockSpec((1, 1, 512), lambda b, s: (s, 0, 0)),
        pl.BlockSpec((1, 8, 512), lambda b, s: (s, 0, 0)),
        pl.BlockSpec((1, 8, LANES), lambda b, s: (s, 0, 0)),
        pl.BlockSpec((1, 8, LANES), lambda b, s: (s, 0, 0)),
    ]
    args = [proj, proj, proj, gates, par, dvec, cwx, cwb, cwc]
    if not zero_init:
        in_specs.append(pl.BlockSpec((n_sub, 1, 2, 128, 256), lambda b, s: (b, s, 0, 0, 0)))
        args.append(s0)
    out_specs = [pl.BlockSpec((blk_len, 512), lambda b, s: (b, s))]
    out_shape = [jax.ShapeDtypeStruct((n_seq * seq_len, 1024), BF16)]
    aliases = {}
    if state_out:
        layers = depth if state_out == 'first' else 1
        layer0 = state_layer
        out_specs.append(pl.BlockSpec((n_sub, layers, 2, 8, SSD_HEADDIM, SSD_STATE),
                                      lambda b, s: (b, layer0, 0, s, 0, 0)))
        out_shape.append(jax.ShapeDtypeStruct((n_seq, depth, 2, SSD_HEADS, SSD_HEADDIM, SSD_STATE), F32))
        if state_out == 'next':
            aliases = {len(args): 1}
            in_specs.append(pl.BlockSpec(memory_space=pl.ANY))
            args.append(state_prev)
    gate_rows = pltpu.VMEM((nc * GROWS, LANES), F32)
    res = pl.pallas_call(
        functools.partial(_ssd_kernel, blk_len=blk_len, seq_len=seq_len, zero_init=zero_init,
                          state_out=state_out),
        grid=(n_seq // n_sub, 2),
        in_specs=in_specs,
        out_specs=out_specs,
        out_shape=out_shape,
        input_output_aliases=aliases,
        scratch_shapes=[pltpu.VMEM((nc * 4, 2 * CHUNK, LANES), BF16),
                        pltpu.VMEM((nc * 2, CHUNK, LANES), BF16),
                        pltpu.VMEM((nc * 2, CHUNK, CHUNK), F32),
                        pltpu.VMEM((nc, LANES, CHUNK), F32),
                        pltpu.VMEM((blk_len, 512), F32),
                        pltpu.VMEM((nc, CHUNK, LANES), F32),
                        gate_rows, gate_rows, gate_rows, gate_rows, gate_rows,
                        pltpu.VMEM((128, 256), F32)],
        compiler_params=_cparams(("parallel", "arbitrary")),
        name="ssd_scan",
    )(*args)
    return res if state_out else (res[0], None)


def _mlstm_kernel(*refs, blk_len, seq_len, zero_init, state_out):
    it = iter(refs)
    q_ref, k_ref, v_ref, g_ref, par_ref, nw_ref = (next(it) for _ in range(6))
    s0_ref, m0_ref = (None, None) if zero_init else (next(it), next(it))
    if state_out == 'next':
        next(it)
    hm_ref = next(it)
    cfin_ref, nfin_ref, mfin_ref = (next(it), next(it), next(it)) if state_out else (None, None, None)
    (qm_s, s_s, kt_s, hacc_s, gt_s, cumt_s, et_s, wendt_s, pmaxt_s, mloct_s, cumc_s, pmaxc_s, st_s, m_s) = (
        next(it) for _ in range(14))

    n_chunks = blk_len // CHUNK
    cps = seq_len // CHUNK
    n_sub = blk_len // seq_len
    n_rows = n_chunks * GROWS
    lane = lax.broadcasted_iota(jnp.int32, (CHUNK, LANES), 1)
    lane_lo = lane < 64
    trow = lax.broadcasted_iota(jnp.int32, (CHUNK, CHUNK), 0)
    scol = lax.broadcasted_iota(jnp.int32, (CHUNK, CHUNK), 1)
    ones_blk = jnp.ones((CHUNK, LANES), BF16)
    bias_row = par_ref[0, 0:1, :]
    qscale = ML_DK ** -0.5

    def phase_a(c, carry):
        r0 = pl.multiple_of(c * CHUNK, CHUNK)
        for pr in (0, 1):
            qp = q_ref[pl.ds(r0, CHUNK), pr * LANES:(pr + 1) * LANES].astype(F32)
            kp = k_ref[pl.ds(r0, CHUNK), pr * LANES:(pr + 1) * LANES]
            kt_s[c * 2 + pr] = kp.astype(F32).T
            for hh in (0, 1):
                in_h = lane_lo if hh == 0 else jnp.logical_not(lane_lo)
                qm = jnp.where(in_h, qp * qscale, 0.0).astype(BF16)
                qm_s[c * 4 + pr * 2 + hh] = qm
                s_s[c * 4 + pr * 2 + hh] = _dot_nt(qm, kp)
        gk = g_ref[pl.ds(r0, CHUNK), :] + bias_row
        comb = jnp.where(lane < LANE_F, gk, -_softplus(-gk))
        g0 = pl.multiple_of(c * GROWS, GROWS)
        gt_s[pl.ds(g0, GROWS), :] = comb.T[LANE_I:LANE_I + GROWS, :]
        return carry

    lax.fori_loop(0, n_chunks, phase_a, 0)

    g_all = gt_s[...]
    cf, cr = _lane_cumsums(g_all, _tri2())
    rr = lax.broadcasted_iota(jnp.int32, (n_rows, 1), 0) & (GROWS - 1)
    fwd_row = rr < 12
    cum_t = jnp.where(fwd_row, cf, cr)
    tot_t = jnp.where(fwd_row, cum_t[:, CHUNK - 1:CHUNK], cum_t[:, 0:1])
    e_t = pltpu.roll(g_all, 8, 0) - cum_t
    mloc_t = jnp.max(e_t, axis=1, keepdims=True) + tot_t
    lane_b = lax.broadcasted_iota(jnp.int32, (n_rows, LANES), 1)
    pf, pb = e_t, e_t
    k = 1
    while k < CHUNK:
        pf = jnp.maximum(pf, jnp.where(lane_b >= k, pltpu.roll(pf, k, 1), -jnp.inf))
        pb = jnp.maximum(pb, jnp.where(lane_b < CHUNK - k, pltpu.roll(pb, CHUNK - k, 1), -jnp.inf))
        k *= 2
    cumt_s[...] = cum_t
    et_s[...] = e_t * LOG2E
    wendt_s[...] = jnp.exp(e_t + tot_t - mloc_t)
    pmaxt_s[...] = jnp.where(fwd_row, pf, pb) * LOG2E
    mloct_s[...] = jnp.broadcast_to(mloc_t, e_t.shape)

    def phase_c(c, carry):
        g8 = pl.multiple_of(c * GROWS + 8, 8)
        cumc_s[c] = _pad_rows_t(jnp.concatenate([cumt_s[pl.ds(g8, 8), :], mloct_s[pl.ds(g8, 8), :]], axis=0))
        pmaxc_s[c] = _pad_rows_t(pmaxt_s[pl.ds(g8, 8), :])
        return carry

    lax.fori_loop(0, n_chunks, phase_c, 0)

    def scan_direction(sub, d):
        mask = (scol <= trow) if d == 0 else (scol >= trow)
        if zero_init:
            st_s[...] = jnp.zeros(st_s.shape, F32)
            m_s[...] = jnp.zeros(m_s.shape, F32)
        else:
            st_s[...] = s0_ref[sub, 0, d]
            m_s[...] = m0_ref[sub, 0]

        def body(ci, carry):
            c = sub * cps + (ci if d == 0 else cps - 1 - ci)
            r0 = pl.multiple_of(c * CHUNK, CHUNK)
            cumc = cumc_s[c]
            m_prev = m_s[...]
            m_prev2 = m_prev * LOG2E
            mx = jnp.maximum(m_prev2, pmaxc_s[c])
            negmt = -(cumc * LOG2E + mx)
            tot = cumc[CHUNK - 1:CHUNK, :] if d == 0 else cumc[0:1, :]
            mloc = pltpu.roll(cumc[0:1, :], LANES - 8, 1)
            m_new = jnp.maximum(tot + m_prev, mloc)
            a_old = jnp.exp(tot + m_prev - m_new)
            a_loc = jnp.exp(mloc - m_new)
            for pr in (0, 1):
                stp = st_s[pr]
                stp_bf = stp.astype(BF16)
                kt = kt_s[c * 2 + pr]
                for hh in (0, 1):
                    hl = pr * 2 + hh
                    j = d * 4 + hl
                    c0 = hl * LANES
                    e_row = et_s[pl.ds(c * GROWS + 8 + j, 1), :]
                    wend_row = wendt_s[pl.ds(c * GROWS + 8 + j, 1), :]
                    mxb = jnp.broadcast_to(mx[:, j:j + 1], (CHUNK, LANES))
                    sqk = (s_s[c * 4 + hl] * jnp.exp2(jnp.where(mask, e_row - mxb, -jnp.inf))).astype(BF16)
                    qw = qm_s[c * 4 + hl] * jnp.exp2(m_prev2[:, j:j + 1] - mxb).astype(BF16)
                    vaug = jnp.concatenate([v_ref[pl.ds(r0, CHUNK), c0:c0 + LANES], ones_blk], axis=1)
                    nd = _dot(jnp.concatenate([sqk, qw], axis=1), jnp.concatenate([vaug, stp_bf], axis=0))
                    emt = jnp.exp2(jnp.broadcast_to(negmt[:, j:j + 1], (CHUNK, LANES)))
                    den = jnp.maximum(jnp.abs(nd[:, LANES:2 * LANES]), emt)
                    hout = nd[:, 0:LANES] / den
                    if d == 0:
                        hacc_s[pl.ds(r0, CHUNK), c0:c0 + LANES] = hout
                    else:
                        hacc_s[pl.ds(r0, CHUNK), c0:c0 + LANES] += hout
                    ktw = (kt[hh * 64:(hh + 1) * 64, :] * (wend_row * a_loc[:, j:j + 1])).astype(BF16)
                    st_s[pr, hh * 64:(hh + 1) * 64, :] = (a_old[:, j:j + 1] * stp[hh * 64:(hh + 1) * 64, :]
                                                          + _dot(ktw, vaug))
            m_s[...] = m_new
            return carry

        lax.fori_loop(0, cps, body, 0, unroll=min(4, cps))
        if state_out:
            for pr in (0, 1):
                t = st_s[pr, :, 0:LANES].T
                t_hi = pltpu.roll(t, 64, 1)
                cfin_ref[sub, 0, d, pr * 2] = t[:, 0:64]
                cfin_ref[sub, 0, d, pr * 2 + 1] = t_hi[:, 0:64]
                nfin_ref[sub, 0, d, pr:pr + 1, :] = st_s[pr, :, LANES:2 * LANES].T[0:1, :]
            mfin_ref[sub, 0, d:d + 1, :] = m_s[...]

    if state_out == 'first':
        _zero_other_layers(cfin_ref, n_sub)
    if state_out:
        nfin_ref[...] = jnp.zeros(nfin_ref.shape, F32)

    def scan_sequence(sub, carry):
        scan_direction(sub, 0)
        scan_direction(sub, 1)
        return carry

    lax.fori_loop(0, n_sub, scan_sequence, 0)

    def phase_e(c, carry):
        r0 = pl.multiple_of(c * CHUNK, CHUNK)
        for hl in range(4):
            c0 = hl * LANES
            hm_ref[pl.ds(r0, CHUNK), c0:c0 + LANES] = _rms(
                hacc_s[pl.ds(r0, CHUNK), c0:c0 + LANES], nw_ref[0, :, c0:c0 + LANES]).astype(hm_ref.dtype)
        return carry

    lax.fori_loop(0, n_chunks, phase_e, 0)


def _mlstm(proj, gates, par, normw, s0, m0, *, row0, n_seq, seq_len, n_sub=1, state_layer=None,
           state_prev=None, depth=1):
    zero_init = s0 is None
    state_out = None if state_layer is None else ('first' if state_layer == 0 else 'next')
    blk_len = n_sub * seq_len
    rb = row0 // blk_len
    nc = blk_len // CHUNK
    in_specs = [
        pl.BlockSpec((blk_len, 256), lambda b, s: (rb + b, COL_Q // 256 + s)),
        pl.BlockSpec((blk_len, 256), lambda b, s: (rb + b, COL_K // 256 + s)),
        pl.BlockSpec((blk_len, 512), lambda b, s: (rb + b, COL_V // 512 + s)),
        pl.BlockSpec((blk_len, LANES), lambda b, s: (rb + b, s)),
        pl.BlockSpec((1, 8, LANES), lambda b, s: (s, 0, 0)),
        pl.BlockSpec((1, 1, 512), lambda b, s: (s, 0, 0)),
    ]
    args = [proj, proj, proj, gates, par, normw]
    if not zero_init:
        in_specs.append(pl.BlockSpec((n_sub, 1, 2, 2, 128, 256), lambda b, s: (b, s, 0, 0, 0, 0)))
        in_specs.append(pl.BlockSpec((n_sub, 1, 1, LANES), lambda b, s: (b, s, 0, 0)))
        args += [s0, m0]
    out_specs = [pl.BlockSpec((blk_len, 512), lambda b, s: (b, s))]
    out_shape = [jax.ShapeDtypeStruct((n_seq * seq_len, 1024), BF16)]
    aliases = {}
    if state_out:
        layers = depth if state_out == 'first' else 1
        layer0 = state_layer
        out_specs.append(pl.BlockSpec((n_sub, layers, 2, 4, ML_DV, ML_DK), lambda b, s: (b, layer0, 0, s, 0, 0)))
        out_shape.append(jax.ShapeDtypeStruct((n_seq, depth, 2, ML_HEADS, ML_DV, ML_DK), F32))
        out_specs.append(pl.BlockSpec((n_sub, 1, 2, 8, LANES), lambda b, s: (b, s, 0, 0, 0)))
        out_shape.append(jax.ShapeDtypeStruct((n_seq, 2, 2, 8, LANES), F32))
        out_specs.append(pl.BlockSpec((n_sub, 1, 2, LANES), lambda b, s: (b, s, 0, 0)))
        out_shape.append(jax.ShapeDtypeStruct((n_seq, 2, 2, LANES), F32))
        if state_out == 'next':
            aliases = {len(args): 1}
            in_specs.append(pl.BlockSpec(memory_space=pl.ANY))
            args.append(state_prev)
    gate_rows = pltpu.VMEM((nc * GROWS, LANES), F32)
    res = pl.pallas_call(
        functools.partial(_mlstm_kernel, blk_len=blk_len, seq_len=seq_len, zero_init=zero_init,
                          state_out=state_out),
        grid=(n_seq // n_sub, 2),
        in_specs=in_specs,
        out_specs=out_specs,
        out_shape=out_shape,
        input_output_aliases=aliases,
        scratch_shapes=[pltpu.VMEM((nc * 4, CHUNK, LANES), BF16),
                        pltpu.VMEM((nc * 4, CHUNK, CHUNK), F32),
                        pltpu.VMEM((nc * 2, LANES, CHUNK), F32),
                        pltpu.VMEM((blk_len, 512), F32),
                        gate_rows, gate_rows, gate_rows, gate_rows, gate_rows, gate_rows,
                        pltpu.VMEM((nc, CHUNK, LANES), F32),
                        pltpu.VMEM((nc, CHUNK, LANES), F32),
                        pltpu.VMEM((2, 128, 256), F32),
                        pltpu.VMEM((1, LANES), F32)],
        compiler_params=_cparams(("parallel", "arbitrary")),
        name="mlstm_scan",
    )(*args)
    return res if state_out else (res[0], None, None, None)


def _outproj_kernel(yc_ref, yl_ref, z_ref, hc_ref, hl_ref, o_ref, gs_ref, gm_ref, ha_ref, hb_ref, wos_ref,
                    wom_ref, wout_ref, sn_ref, nf_ref, g1_ref, sc2_ref, sh2_ref, hout_ref, u2_ref, *,
                    n_ctx_tiles):
    is_ctx = pl.program_id(0) < n_ctx_tiles
    y_in = jnp.where(is_ctx, yc_ref[...], yl_ref[...]).astype(F32)
    hm_in = jnp.where(is_ctx, hc_ref[...], hl_ref[...]).astype(F32)
    y = y_in * _silu(z_ref[...].astype(F32))
    y = _rms(y, sn_ref[...]).astype(BF16)
    y_ssd = _dot(y, wos_ref[...])
    hm = (hm_in * jax.nn.sigmoid(o_ref[...].astype(F32))).astype(BF16)
    y_ml = _dot(hm, wom_ref[...])
    mix = (jax.nn.sigmoid(gs_ref[...].astype(F32)) * y_ssd
           + jax.nn.sigmoid(gm_ref[...].astype(F32)) * y_ml).astype(BF16)
    h = jnp.where(is_ctx, ha_ref[...], hb_ref[...]) + g1_ref[0] * _dot(mix, wout_ref[...])
    hout_ref[...] = h
    u2_ref[...] = (_rms(h, nf_ref[...]) * (1.0 + sc2_ref[0]) + sh2_ref[0]).astype(u2_ref.dtype)


def _outproj(y_ctx, y_lat, hm_ctx, hm_lat, proj, rows, wos, wom, wout, ssd_norm, norm_ffn, mod48, lat_len,
             tm=512):
    t, d = proj.shape[0], wos.shape[0]
    n_ctx_rows = rows.n_ctx_rows
    nct = n_ctx_rows // tm
    mi = functools.partial(_mod_index, tm=tm, n_ctx_rows=n_ctx_rows, lat_len=lat_len)
    row = lambda i: (i, 0)
    ctx_row = lambda i: (jnp.minimum(i, nct - 1), 0)
    lat_row = lambda i: (jnp.maximum(i - nct, 0), 0)
    const = lambda i: (0, 0)
    col = lambda k: (lambda i: (i, k))
    modspec = lambda which: pl.BlockSpec((1, 1, d), lambda i: (mi(i, which=which), 0, 0))
    return pl.pallas_call(
        functools.partial(_outproj_kernel, n_ctx_tiles=nct),
        grid=(t // tm,),
        in_specs=[pl.BlockSpec((tm, d), ctx_row), pl.BlockSpec((tm, d), lat_row),
                  pl.BlockSpec((tm, d), col(COL_Z // d)),
                  pl.BlockSpec((tm, d), ctx_row), pl.BlockSpec((tm, d), lat_row),
                  pl.BlockSpec((tm, d), col(COL_O // d)),
                  pl.BlockSpec((tm, d), col(COL_GS // d)), pl.BlockSpec((tm, d), col(COL_GM // d))]
        + rows.specs(tm, d) + [
                  pl.BlockSpec((d, d), const), pl.BlockSpec((d, d), const), pl.BlockSpec((d, d), const),
                  pl.BlockSpec((1, d), const), pl.BlockSpec((1, d), const),
                  modspec(2), modspec(4), modspec(3)],
        out_specs=[pl.BlockSpec((tm, d), row), pl.BlockSpec((tm, d), row)],
        out_shape=[jax.ShapeDtypeStruct((t, d), F32), jax.ShapeDtypeStruct((t, d), BF16)],
        compiler_params=_cparams(("parallel",)),
        name="merge_outproj",
    )(y_ctx, y_lat, proj, hm_ctx, hm_lat, proj, proj, proj, rows.ctx, rows.lat, wos, wom, wout,
      ssd_norm.reshape(1, d), norm_ffn.reshape(1, d), mod48, mod48, mod48)


FFN_TM = 512
FFN_TF = 256


def _ffn_kernel(u_ref, ut_ref, ub_ref, wup_ref, cw_ref, wd_ref, h_ref, g2_ref, o_ref, act_s, *,
                n_ctx_tiles, ctx_len, tiles_per_seq):
    i = pl.program_id(0)
    tm, tf = FFN_TM, FFN_TF
    n_chunks = D_FF // tf
    ext = tm + 2 * GRID_W

    def cw(r, j):
        return cw_ref[r:r + 1, j * tf:(j + 1) * tf]

    @pl.when(i < n_ctx_tiles)
    def _():
        u = u_ref[...]
        pos = lax.broadcasted_iota(jnp.int32, (tm, tf), 0) & (ctx_len - 1)
        for j in range(n_chunks):
            g = _dot(u, wup_ref[:, j * tf:(j + 1) * tf])
            val = _dot(u, wup_ref[:, D_FF + j * tf:D_FF + (j + 1) * tf])
            gl = jnp.where(pos != 0, pltpu.roll(g, 1, 0), 0.0)
            gr = jnp.where(pos != ctx_len - 1, pltpu.roll(g, tm - 1, 0), 0.0)
            conv = gl * cw(3, j) + g * cw(4, j) + gr * cw(5, j) + cw(9, j)
            act_s[:, j * tf:(j + 1) * tf] = (_silu(conv) * val).astype(BF16)

    @pl.when(i >= n_ctx_tiles)
    def _():
        ti = (i - n_ctx_tiles) % tiles_per_seq
        u = u_ref[...]
        top = jnp.where(ti > 0, ut_ref[...], jnp.zeros_like(ut_ref[...]))
        bot = jnp.where(ti < tiles_per_seq - 1, ub_ref[...], jnp.zeros_like(ub_ref[...]))
        uext = jnp.concatenate([top, u, bot], axis=0)
        col = lax.broadcasted_iota(jnp.int32, (ext, tf), 0) & (GRID_W - 1)
        for j in range(n_chunks):
            gx = _dot(uext, wup_ref[:, j * tf:(j + 1) * tf])
            val = _dot(u, wup_ref[:, D_FF + j * tf:D_FF + (j + 1) * tf])
            gl = jnp.where(col != 0, pltpu.roll(gx, 1, 0), 0.0)
            gr = jnp.where(col != GRID_W - 1, pltpu.roll(gx, ext - 1, 0), 0.0)

            def taps(r, lo):
                return (gl[lo:lo + tm] * cw(3 * r, j) + gx[lo:lo + tm] * cw(3 * r + 1, j)
                        + gr[lo:lo + tm] * cw(3 * r + 2, j))

            conv = taps(1, GRID_W) + taps(0, 0) + taps(2, 2 * GRID_W) + cw(9, j)
            act_s[:, j * tf:(j + 1) * tf] = (_silu(conv) * val).astype(BF16)

    o_ref[...] = h_ref[...] + g2_ref[0] * _dot(act_s[...], wd_ref[...])


def _ffn(u2, wup, cw16, wd, h, mod48, n_ctx_rows, ctx_len, lat_len):
    t, d = h.shape
    tm = FFN_TM
    hb = tm // GRID_W
    n_hblocks = t // GRID_W
    mi = functools.partial(_mod_index, tm=tm, n_ctx_rows=n_ctx_rows, lat_len=lat_len)
    resident = lambda shape: pl.BlockSpec(shape, lambda i: (0, 0), pipeline_mode=pl.Buffered(1))
    return pl.pallas_call(
        functools.partial(_ffn_kernel, n_ctx_tiles=n_ctx_rows // tm, ctx_len=ctx_len,
                          tiles_per_seq=lat_len // tm),
        grid=(t // tm,),
        in_specs=[pl.BlockSpec((tm, d), lambda i: (i, 0)),
                  pl.BlockSpec((GRID_W, d), lambda i: (jnp.maximum(i * hb - 1, 0), 0)),
                  pl.BlockSpec((GRID_W, d), lambda i: (jnp.minimum((i + 1) * hb, n_hblocks - 1), 0)),
                  resident((d, 2 * D_FF)),
                  resident((16, D_FF)),
                  resident((D_FF, d)),
                  pl.BlockSpec((tm, d), lambda i: (i, 0)),
                  pl.BlockSpec((1, 1, d), lambda i: (mi(i, which=5), 0, 0))],
        out_specs=pl.BlockSpec((tm, d), lambda i: (i, 0)),
        out_shape=jax.ShapeDtypeStruct((t, d), F32),
        scratch_shapes=[pltpu.VMEM((tm, D_FF), BF16)],
        compiler_params=_cparams(("parallel",)),
        name="convffn",
    )(u2, u2, u2, wup, cw16, wd, h, mod48)


def _arrange_w_in(w):
    z, xbc, dt, q, k, v, o, ig, fg, gates = (
        w[:, 0:1024], w[:, 1024:2560], w[:, 2560:2592], w[:, 2592:3104], w[:, 3104:3616],
        w[:, 3616:4640], w[:, 4640:5664], w[:, 5664:5680], w[:, 5680:5696], w[:, 5696:7744])
    big = jnp.concatenate([z, o, gates, xbc, q, k, v], axis=1).astype(BF16)
    blocks = []
    for s in (0, 1):
        blocks += [dt[:, s * 8:s * 8 + 8], dt[:, 16 + s * 8:16 + s * 8 + 8],
                   ig[:, s * 4:s * 4 + 4], ig[:, 8 + s * 4:8 + s * 4 + 4],
                   fg[:, s * 4:s * 4 + 4], fg[:, 8 + s * 4:8 + s * 4 + 4],
                   jnp.zeros((w.shape[0], LANES - 32), w.dtype)]
    small = jnp.concatenate(blocks, axis=1).astype(BF16)
    return big, small


def _slab_rows(parts, n_rows=8):
    out = jnp.zeros((2, n_rows, LANES), F32)
    for r, lane0, val in parts:
        out = out.at[:, r, lane0:lane0 + val.shape[1]].set(val)
    return out


def _ssd_params(dt_bias, a_log, d_skip, conv_w, conv_b):
    par = _slab_rows([(0, 0, dt_bias[0].reshape(2, 8)), (0, 8, dt_bias[1].reshape(2, 8)),
                      (1, 0, a_log[0].reshape(2, 8)), (1, 8, a_log[1].reshape(2, 8))])
    dvec = jnp.repeat(d_skip, SSD_HEADDIM).reshape(2, 1, 512)
    cw = jnp.concatenate([conv_w, conv_b[None], jnp.zeros((4, conv_w.shape[1]), F32)], axis=0)
    cwx = cw[:, 0:1024].reshape(8, 2, 512).transpose(1, 0, 2)
    cwb = cw[:, 1024:1280].reshape(8, 2, LANES).transpose(1, 0, 2)
    cwc = cw[:, 1280:1536].reshape(8, 2, LANES).transpose(1, 0, 2)
    return par, dvec, cwx, cwb, cwc


def _ml_params(i_bias, f_bias, ml_norm):
    par = _slab_rows([(0, LANE_I, i_bias[0].reshape(2, 4)), (0, LANE_I + 4, i_bias[1].reshape(2, 4)),
                      (0, LANE_F, f_bias[0].reshape(2, 4)), (0, LANE_F + 4, f_bias[1].reshape(2, 4))])
    return par, ml_norm.reshape(2, 1, 512)


def _ssd_state_to_kernel(s):
    b = s.shape[0]
    s = s.reshape(b, 2, 2, 2, 4, 64, 64).transpose(0, 2, 1, 3, 6, 4, 5)
    return s.reshape(b, 2, 2, 128, 256)


def _ml_state_to_kernel(c0, n0, m0):
    b = c0.shape[0]
    aug = jnp.concatenate([jnp.swapaxes(c0, -1, -2),
                           jnp.broadcast_to(n0[..., None], n0.shape + (LANES,))], axis=-1)
    aug = aug.reshape(b, 2, 2, 2, 2, 64, 256).transpose(0, 2, 1, 3, 4, 5, 6).reshape(b, 2, 2, 2, 128, 256)
    m = m0.reshape(b, 2, 2, 4).transpose(0, 2, 1, 3).reshape(b, 2, 1, 8)
    mk = jnp.zeros((b, 2, 1, LANES), F32).at[..., 0:8].set(m)
    return aug, mk


def _ml_nm_from_kernel(nfin, mfin):
    b = nfin.shape[0]
    n = nfin[:, :, :, 0:2, :].reshape(b, 2, 2, 2, 2, ML_DK).transpose(0, 2, 1, 3, 4, 5).reshape(b, 2, ML_HEADS, ML_DK)
    m = jnp.stack([mfin[:, :, 0, 0:4], mfin[:, :, 1, 4:8]], axis=1)
    return n, m.reshape(b, 2, ML_HEADS)


def kernel(x_prompt, x_sample, state_ssd, state_mlstm_C, state_mlstm_n, state_mlstm_m, c, c_ctx, w_ada, b_ada,
           norm_mix, w_in, ssd_conv_w, ssd_conv_b, ssd_dt_bias, ssd_a_log, ssd_d, ssd_norm, w_o_ssd, ml_i_bias,
           ml_f_bias, ml_norm, w_o_ml, w_out, norm_ffn, w_up, ffn_conv_w, ffn_conv_b, w_down, final_norm):
    nb, ctx_len, d = x_prompt.shape
    nlat, lat_len, _ = x_sample.shape
    depth = w_in.shape[0]
    n_ctx_rows = nb * ctx_len
    n_lat_rows = nlat * lat_len

    n_rows = n_ctx_rows + n_lat_rows
    rows = _Rows(x_prompt.reshape(n_ctx_rows, d), x_sample.reshape(n_lat_rows, d), n_ctx_rows)
    cond8 = jnp.concatenate([c_ctx[None], c, jnp.zeros((8 - 1 - nlat, d), F32)], axis=0)
    mod = _ada(cond8, w_ada, b_ada)

    new_ssd, new_c, n_l, m_l = None, None, [], []
    for l in range(depth):
        mod48 = mod[l].reshape(48, 1, d)
        w_big, w_small = _arrange_w_in(w_in[l])
        par_s, dvec, cwx, cwb, cwc = _ssd_params(ssd_dt_bias[l], ssd_a_log[l], ssd_d[l], ssd_conv_w[l],
                                                 ssd_conv_b[l])
        par_m, normw = _ml_params(ml_i_bias[l], ml_f_bias[l], ml_norm[l])

        proj, gates = _in_proj(rows, norm_mix[l], mod48, w_big, w_small, n_rows, lat_len)

        y_ctx, new_ssd = _ssd(proj, gates, par_s, dvec, cwx, cwb, cwc, None, row0=0, n_seq=nb, seq_len=ctx_len,
                              n_sub=CTX_SEQS_PER_STEP, state_layer=l, state_prev=new_ssd, depth=depth)
        y_lat, _ = _ssd(proj, gates, par_s, dvec, cwx, cwb, cwc, _ssd_state_to_kernel(state_ssd[:, l]),
                        row0=n_ctx_rows, n_seq=nlat, seq_len=lat_len)
        hm_ctx, new_c, n_ctx, m_ctx = _mlstm(proj, gates, par_m, normw, None, None, row0=0, n_seq=nb,
                                             seq_len=ctx_len, n_sub=CTX_SEQS_PER_STEP, state_layer=l,
                                             state_prev=new_c, depth=depth)
        s0m, m0m = _ml_state_to_kernel(state_mlstm_C[:, l], state_mlstm_n[:, l], state_mlstm_m[:, l])
        hm_lat, _, _, _ = _mlstm(proj, gates, par_m, normw, s0m, m0m,
                                 row0=n_ctx_rows, n_seq=nlat, seq_len=lat_len)
        h, u2 = _outproj(y_ctx, y_lat, hm_ctx, hm_lat, proj, rows, w_o_ssd[l].astype(BF16),
                         w_o_ml[l].astype(BF16), w_out[l].astype(BF16), ssd_norm[l], norm_ffn[l], mod48, lat_len)

        cw16 = jnp.concatenate([ffn_conv_w[l].reshape(9, D_FF), ffn_conv_b[l][None],
                                jnp.zeros((6, D_FF), F32)], axis=0)
        h = _ffn(u2, w_up[l].astype(BF16), cw16, w_down[l].astype(BF16), h, mod48, n_ctx_rows, ctx_len, lat_len)
        rows = _Rows(h, h, n_ctx_rows)

        n_new, m_new = _ml_nm_from_kernel(n_ctx, m_ctx)
        n_l.append(n_new)
        m_l.append(m_new)

    y_prompt = _final_norm(h, final_norm, 0, n_ctx_rows).reshape(nb, ctx_len, d)
    y_sample = _final_norm(h, final_norm, n_ctx_rows, n_lat_rows).reshape(nlat, lat_len, d)
    return (y_prompt, y_sample, new_ssd, new_c, jnp.stack(n_l, axis=1), jnp.stack(m_l, axis=1))
```

```python
import functools

import jax
import jax.numpy as jnp
import numpy as np
from jax import lax
from jax.experimental import pallas as pl
from jax.experimental.pallas import tpu as pltpu

F32 = jnp.float32
BF16 = jnp.bfloat16

D_MODEL = 1024
CHUNK = 128
EPS = 1e-6
GRID_W = 64
SSD_HEADS = 16
SSD_HEADDIM = 64
SSD_STATE = 64
SSD_GROUPS = 4
ML_HEADS = 8
ML_DV = 128
ML_DK = 64
D_FF = 2816

LANES = 128
BF16_SUBLANES = 16
VMEM_LIMIT = 56 * 1024 * 1024

COL_Z, COL_O, COL_GS, COL_GM = 0, 1024, 2048, 3072
COL_X, COL_B, COL_C = 4096, 5120, 5376
COL_Q, COL_K, COL_V = 5632, 6144, 6656
PROJ_COLS = 7680
GATE_COLS = 256
LANE_I = 16
LANE_F = 24
GROWS = 16
LOG2E = 1.4426950408889634
CTX_SEQS_PER_STEP = 8


def _cparams(sem):
    return pltpu.CompilerParams(dimension_semantics=sem, vmem_limit_bytes=VMEM_LIMIT)


def _silu(x):
    return x * jax.nn.sigmoid(x)


def _softplus(x):
    return jnp.maximum(x, 0.0) + jnp.log1p(jnp.exp(-jnp.abs(x)))


def _dot(a, b):
    return jnp.dot(a, b, preferred_element_type=F32)


def _dot_nt(a, b):
    return lax.dot_general(a, b, (((1,), (1,)), ((), ())), preferred_element_type=F32)


def _pair(lane_lo, col0, col1):
    return jnp.where(lane_lo, col0, col1)


def _tri2():
    s = lax.broadcasted_iota(jnp.int32, (CHUNK, 2 * CHUNK), 0)
    t = lax.broadcasted_iota(jnp.int32, (CHUNK, 2 * CHUNK), 1)
    keep = ((t < CHUNK) & (s <= t)) | ((t >= CHUNK) & (s >= t - CHUNK))
    return jnp.where(keep, 1.0, 0.0).astype(BF16)


def _lane_cumsums(x, tri2):
    hi = x.astype(BF16)
    r1 = x - hi.astype(F32)
    mid = r1.astype(BF16)
    lo = (r1 - mid.astype(F32)).astype(BF16)
    cs = _dot(hi, tri2) + _dot(mid, tri2) + _dot(lo, tri2)
    return cs[:, 0:CHUNK], cs[:, CHUNK:2 * CHUNK]


def _pad_rows_t(x):
    return jnp.concatenate([x, jnp.zeros((CHUNK - x.shape[0], LANES), F32)], axis=0).T


def _ada_kernel(c_ref, w_ref, b_ref, o_ref):
    cond = _silu(c_ref[...]).astype(BF16)
    o_ref[0] = _dot(cond, w_ref[0].astype(BF16)) + b_ref[0]


def _ada(cond8, w_ada, b_ada):
    depth, d, n = w_ada.shape
    tn = 1536
    return pl.pallas_call(
        _ada_kernel,
        grid=(depth, n // tn),
        in_specs=[pl.BlockSpec((8, d), lambda l, j: (0, 0)),
                  pl.BlockSpec((1, d, tn), lambda l, j: (l, 0, j)),
                  pl.BlockSpec((1, 1, tn), lambda l, j: (l, 0, j))],
        out_specs=pl.BlockSpec((1, 8, tn), lambda l, j: (l, 0, j)),
        out_shape=jax.ShapeDtypeStruct((depth, 8, n), F32),
        compiler_params=_cparams(("arbitrary", "arbitrary")),
        name="ada_mod",
    )(cond8, w_ada, b_ada.reshape(depth, 1, n))


def _mod_index(i, tm, n_ctx_rows, lat_len, which):
    n_ctx_tiles = n_ctx_rows // tm
    tiles_per_seq = lat_len // tm
    row = jnp.where(i < n_ctx_tiles, 0, 1 + (i - n_ctx_tiles) // tiles_per_seq)
    return row * 6 + which


def _rms(x, w):
    return (x * lax.rsqrt(jnp.mean(x * x, axis=-1, keepdims=True) + EPS)) * w


class _Rows:
    def __init__(self, ctx, lat, n_ctx_rows):
        self.ctx, self.lat, self.n_ctx_rows = ctx, lat, n_ctx_rows
        self.lat_row0 = n_ctx_rows if lat is ctx else 0

    def specs(self, tm, d):
        nct = self.n_ctx_rows // tm
        off = self.lat_row0 // tm
        return [pl.BlockSpec((tm, d), lambda i, *_: (jnp.minimum(i, nct - 1), 0)),
                pl.BlockSpec((tm, d), lambda i, *_: (jnp.maximum(i - nct, 0) + off, 0))]


IN_TM = 1024
IN_TN = 1536


def _in_proj_kernel(ha_ref, hb_ref, nw_ref, sc_ref, sh_ref, w_ref, wg_ref, proj_ref, gate_ref, u_s, *,
                    n_ctx_tiles, n_col_tiles):
    i = pl.program_id(0)
    j = pl.program_id(1)

    @pl.when(j == 0)
    def _():
        x = jnp.where(i < n_ctx_tiles, ha_ref[...], hb_ref[...])
        u_s[...] = (_rms(x, nw_ref[...]) * (1.0 + sc_ref[0]) + sh_ref[0]).astype(u_s.dtype)

    @pl.when(j < n_col_tiles)
    def _():
        proj_ref[...] = _dot(u_s[...], w_ref[...]).astype(proj_ref.dtype)

    @pl.when(j == n_col_tiles)
    def _():
        gate_ref[...] = _dot(u_s[...], wg_ref[...])


def _in_proj(rows, w, mod48, w_big, w_small, n_rows, lat_len):
    d = w.shape[0]
    tm, tn = IN_TM, IN_TN
    nj = PROJ_COLS // tn
    mi = functools.partial(_mod_index, tm=tm, n_ctx_rows=rows.n_ctx_rows, lat_len=lat_len)
    return pl.pallas_call(
        functools.partial(_in_proj_kernel, n_ctx_tiles=rows.n_ctx_rows // tm, n_col_tiles=nj),
        grid=(n_rows // tm, nj + 1),
        in_specs=rows.specs(tm, d) + [
            pl.BlockSpec((1, d), lambda i, j: (0, 0)),
            pl.BlockSpec((1, 1, d), lambda i, j: (mi(i, which=1), 0, 0)),
            pl.BlockSpec((1, 1, d), lambda i, j: (mi(i, which=0), 0, 0)),
            pl.BlockSpec((d, tn), lambda i, j: (0, jnp.minimum(j, nj - 1))),
            pl.BlockSpec((d, GATE_COLS), lambda i, j: (0, 0))],
        out_specs=[pl.BlockSpec((tm, tn), lambda i, j: (i, jnp.minimum(j, nj - 1))),
                   pl.BlockSpec((tm, GATE_COLS), lambda i, j: (i, 0))],
        out_shape=[jax.ShapeDtypeStruct((n_rows, PROJ_COLS), BF16),
                   jax.ShapeDtypeStruct((n_rows, GATE_COLS), F32)],
        scratch_shapes=[pltpu.VMEM((tm, d), BF16)],
        compiler_params=_cparams(("parallel", "arbitrary")),
        name="in_proj",
    )(rows.ctx, rows.lat, w.reshape(1, d), mod48, mod48, w_big, w_small)


def _conv_silu_chunk(src_ref, cw_ref, c, cps, blk_len):
    r0 = pl.multiple_of(c * CHUNK, CHUNK)
    cur = src_ref[pl.ds(r0, CHUNK), :].astype(F32)
    p0 = pl.multiple_of(jnp.maximum(r0 - BF16_SUBLANES, 0), BF16_SUBLANES)
    n0 = pl.multiple_of(jnp.minimum(r0 + CHUNK, blk_len - BF16_SUBLANES), BF16_SUBLANES)
    prv = src_ref[pl.ds(p0, BF16_SUBLANES), :].astype(F32)[BF16_SUBLANES - 1:BF16_SUBLANES, :]
    nxt = src_ref[pl.ds(n0, BF16_SUBLANES), :].astype(F32)[0:1, :]
    cs = c % cps
    prv = jnp.where(cs > 0, prv, 0.0)
    nxt = jnp.where(cs < cps - 1, nxt, 0.0)
    row = lax.broadcasted_iota(jnp.int32, cur.shape, 0)
    up = jnp.where(row == 0, prv, pltpu.roll(cur, 1, 0))
    dn = jnp.where(row == CHUNK - 1, nxt, pltpu.roll(cur, CHUNK - 1, 0))
    y = up * cw_ref[0, 0:1, :] + cur * cw_ref[0, 1:2, :] + dn * cw_ref[0, 2:3, :] + cw_ref[0, 3:4, :]
    return _silu(y)


def _zero_other_layers(ref, n_sub):
    z = jnp.zeros(ref.shape[-2:], F32)
    for sub in range(n_sub):
        for layer in range(1, ref.shape[1]):
            for d in range(ref.shape[2]):
                for h in range(ref.shape[3]):
                    ref[sub, layer, d, h] = z


def _ssd_kernel(*refs, blk_len, seq_len, zero_init, state_out):
    it = iter(refs)
    x_ref, b_ref, c_ref, g_ref = next(it), next(it), next(it), next(it)
    par_ref, d_ref, cwx_ref, cwb_ref, cwc_ref = next(it), next(it), next(it), next(it), next(it)
    s0_ref = None if zero_init else next(it)
    if state_out == 'next':
        next(it)
    y_ref = next(it)
    sfin_ref = next(it) if state_out else None
    (xbd_s, cm_s, g_s, bt_s, yacc_s, cumc_s, at_s, dtt_s, cumt_s, dtet_s, cdt_s, st_s) = (
        next(it) for _ in range(12))

    n_chunks = blk_len // CHUNK
    cps = seq_len // CHUNK
    n_sub = blk_len // seq_len
    lane = lax.broadcasted_iota(jnp.int32, (CHUNK, LANES), 1)
    lane_lo = lane < 64
    lane_row_lo = lax.broadcasted_iota(jnp.int32, (1, LANES), 1) < 64
    trow = lax.broadcasted_iota(jnp.int32, (CHUNK, CHUNK), 0)
    scol = lax.broadcasted_iota(jnp.int32, (CHUNK, CHUNK), 1)
    bias_row = par_ref[0, 0:1, :]
    a_row = -jnp.exp(par_ref[0, 1:2, :])

    def phase_a(c, carry):
        r0 = pl.multiple_of(c * CHUNK, CHUNK)
        xck = _conv_silu_chunk(x_ref, cwx_ref, c, cps, blk_len)
        bck = _conv_silu_chunk(b_ref, cwb_ref, c, cps, blk_len)
        cck = _conv_silu_chunk(c_ref, cwc_ref, c, cps, blk_len)
        yacc_s[pl.ds(r0, CHUNK), :] = d_ref[0] * xck
        for p in range(4):
            xp = xck[:, p * LANES:(p + 1) * LANES]
            xbd_s[c * 4 + p] = jnp.concatenate(
                [jnp.where(lane_lo, xp, 0.0), jnp.where(lane_lo, 0.0, xp)], axis=0).astype(BF16)
        bbf = bck.astype(BF16)
        for g2 in (0, 1):
            in_g = lane_lo if g2 == 0 else jnp.logical_not(lane_lo)
            cm = jnp.where(in_g, cck, 0.0).astype(BF16)
            cm_s[c * 2 + g2] = cm
            g_s[c * 2 + g2] = _dot_nt(cm, bbf)
        bt_s[c] = bck.T
        dt = _softplus(g_ref[pl.ds(r0, CHUNK), :] + bias_row)
        g0 = pl.multiple_of(c * GROWS, GROWS)
        at_s[pl.ds(g0, GROWS), :] = (dt * a_row).T[0:GROWS, :]
        dtt_s[pl.ds(g0, GROWS), :] = dt.T[0:GROWS, :]
        return carry

    lax.fori_loop(0, n_chunks, phase_a, 0)

    cf, cr = _lane_cumsums(at_s[...], _tri2())
    fwd_row = (lax.broadcasted_iota(jnp.int32, (n_chunks * GROWS, 1), 0) & (GROWS - 1)) < 8
    cum_t = jnp.where(fwd_row, cf, cr)
    tot_t = jnp.where(fwd_row, cum_t[:, CHUNK - 1:CHUNK], cum_t[:, 0:1])
    cumt_s[...] = cum_t * LOG2E
    dtet_s[...] = dtt_s[...] * jnp.exp(tot_t - cum_t)
    cdt_s[...] = jnp.broadcast_to(jnp.exp(tot_t), cum_t.shape)

    def phase_c(c, carry):
        g0 = pl.multiple_of(c * GROWS, GROWS)
        cumc_s[c] = _pad_rows_t(cumt_s[pl.ds(g0, GROWS), :])
        return carry

    lax.fori_loop(0, n_chunks, phase_c, 0)

    def scan_direction(sub, d):
        mask = (scol <= trow) if d == 0 else (scol >= trow)
        if zero_init:
            st_s[...] = jnp.zeros(st_s.shape, F32)
        else:
            st_s[...] = s0_ref[sub, 0, d]

        def body(ci, carry):
            c = sub * cps + (ci if d == 0 else cps - 1 - ci)
            r0 = pl.multiple_of(c * CHUNK, CHUNK)
            cumc = cumc_s[c]
            st = st_s[...]
            st_bf = st.astype(BF16)

            def grow(ref, j):
                return ref[pl.ds(c * GROWS + j, 1), :]

            for g2 in (0, 1):
                yoff = _dot(cm_s[c * 2 + g2], st_bf)
                gmat = g_s[c * 2 + g2]
                bt_g = bt_s[c, g2 * 64:(g2 + 1) * 64, :]
                for pr in (0, 1):
                    p = g2 * 2 + pr
                    j0 = d * 8 + g2 * 4 + pr * 2
                    j1 = j0 + 1
                    cb0 = jnp.broadcast_to(cumc[:, j0:j0 + 1], (CHUNK, LANES))
                    cb1 = jnp.broadcast_to(cumc[:, j1:j1 + 1], (CHUNK, LANES))
                    m0 = gmat * jnp.exp2(jnp.where(mask, cb0 - grow(cumt_s, j0), -jnp.inf))
                    m1 = gmat * jnp.exp2(jnp.where(mask, cb1 - grow(cumt_s, j1), -jnp.inf))
                    mcat = jnp.concatenate([(m0 * grow(dtt_s, j0)).astype(BF16),
                                            (m1 * grow(dtt_s, j1)).astype(BF16)], axis=1)
                    xbd = xbd_s[c * 4 + p]
                    yp = _dot(mcat, xbd) + (jnp.exp2(jnp.where(lane_lo, cb0, cb1))
                                            * yoff[:, pr * LANES:(pr + 1) * LANES])
                    c0 = p * LANES
                    if d == 0:
                        yacc_s[pl.ds(r0, CHUNK), c0:c0 + LANES] += yp
                    else:
                        y_ref[pl.ds(r0, CHUNK), c0:c0 + LANES] = (
                            yacc_s[pl.ds(r0, CHUNK), c0:c0 + LANES] + yp).astype(y_ref.dtype)
                    lhs = jnp.concatenate([(bt_g * grow(dtet_s, j0)).astype(BF16),
                                           (bt_g * grow(dtet_s, j1)).astype(BF16)], axis=1)
                    cd = jnp.where(lane_row_lo, grow(cdt_s, j0), grow(cdt_s, j1))
                    st_s[g2 * 64:(g2 + 1) * 64, pr * LANES:(pr + 1) * LANES] = (
                        cd * st[g2 * 64:(g2 + 1) * 64, pr * LANES:(pr + 1) * LANES] + _dot(lhs, xbd))
            return carry

        lax.fori_loop(0, cps, body, 0, unroll=min(4, cps))
        if state_out:
            for half in (0, 1):
                t = st_s[:, half * LANES:(half + 1) * LANES].T
                t_hi = pltpu.roll(t, 64, 1)
                for hloc in (0, 1):
                    hl = half * 2 + hloc
                    sfin_ref[sub, 0, d, hl] = t[hloc * 64:(hloc + 1) * 64, 0:64]
                    sfin_ref[sub, 0, d, 4 + hl] = t_hi[hloc * 64:(hloc + 1) * 64, 0:64]

    if state_out == 'first':
        _zero_other_layers(sfin_ref, n_sub)

    def scan_sequence(sub, carry):
        scan_direction(sub, 0)
        scan_direction(sub, 1)
        return carry

    lax.fori_loop(0, n_sub, scan_sequence, 0)


def _ssd(proj, gates, par, dvec, cwx, cwb, cwc, s0, *, row0, n_seq, seq_len, n_sub=1, state_layer=None,
         state_prev=None, depth=1):
    zero_init = s0 is None
    state_out = None if state_layer is None else ('first' if state_layer == 0 else 'next')
    blk_len = n_sub * seq_len
    rb = row0 // blk_len
    nc = blk_len // CHUNK
    in_specs = [
        pl.BlockSpec((blk_len, 512), lambda b, s: (rb + b, COL_X // 512 + s)),
        pl.BlockSpec((blk_len, LANES), lambda b, s: (rb + b, COL_B // LANES + s)),
        pl.BlockSpec((blk_len, LANES), lambda b, s: (rb + b, COL_C // LANES + s)),
        pl.BlockSpec((blk_len, LANES), lambda b, s: (rb + b, s)),
        pl.BlockSpec((1, 8, LANES), lambda b, s: (s, 0, 0)),
        pl.BlockSpec((1, 1, 512), lambda b, s: (s, 0, 0)),
        pl.BlockSpec((1, 8, 512), lambda b, s: (s, 0, 0)),
        pl.BlockSpec((1, 8, LANES), lambda b, s: (s, 0, 0)),
        pl.BlockSpec((1, 8, LANES), lambda b, s: (s, 0, 0)),
    ]
    args = [proj, proj, proj, gates, par, dvec, cwx, cwb, cwc]
    if not zero_init:
        in_specs.append(pl.BlockSpec((n_sub, 1, 2, 128, 256), lambda b, s: (b, s, 0, 0, 0)))
        args.append(s0)
    out_specs = [pl.BlockSpec((blk_len, 512), lambda b, s: (b, s))]
    out_shape = [jax.ShapeDtypeStruct((n_seq * seq_len, 1024), BF16)]
    aliases = {}
    if state_out:
        layers = depth if state_out == 'first' else 1
        layer0 = state_layer
        out_specs.append(pl.BlockSpec((n_sub, layers, 2, 8, SSD_HEADDIM, SSD_STATE),
                                      lambda b, s: (b, layer0, 0, s, 0, 0)))
        out_shape.append(jax.ShapeDtypeStruct((n_seq, depth, 2, SSD_HEADS, SSD_HEADDIM, SSD_STATE), F32))
        if state_out == 'next':
            aliases = {len(args): 1}
            in_specs.append(pl.BlockSpec(memory_space=pl.ANY))
            args.append(state_prev)
    gate_rows = pltpu.VMEM((nc * GROWS, LANES), F32)
    res = pl.pallas_call(
        functools.partial(_ssd_kernel, blk_len=blk_len, seq_len=seq_len, zero_init=zero_init,
                          state_out=state_out),
        grid=(n_seq // n_sub, 2),
        in_specs=in_specs,
        out_specs=out_specs,
        out_shape=out_shape,
        input_output_aliases=aliases,
        scratch_shapes=[pltpu.VMEM((nc * 4, 2 * CHUNK, LANES), BF16),
                        pltpu.VMEM((nc * 2, CHUNK, LANES), BF16),
                        pltpu.VMEM((nc * 2, CHUNK, CHUNK), F32),
                        pltpu.VMEM((nc, LANES, CHUNK), F32),
                        pltpu.VMEM((blk_len, 512), F32),
                        pltpu.VMEM((nc, CHUNK, LANES), F32),
                        gate_rows, gate_rows, gate_rows, gate_rows, gate_rows,
                        pltpu.VMEM((128, 256), F32)],
        compiler_params=_cparams(("parallel", "arbitrary")),
        name="ssd_scan",
    )(*args)
    return res if state_out else (res[0], None)


def _mlstm_kernel(*refs, blk_len, seq_len, zero_init, state_out):
    it = iter(refs)
    q_ref, k_ref, v_ref, g_ref, par_ref, nw_ref = (next(it) for _ in range(6))
    s0_ref, m0_ref = (None, None) if zero_init else (next(it), next(it))
    if state_out == 'next':
        next(it)
    hm_ref = next(it)
    cfin_ref, nfin_ref, mfin_ref = (next(it), next(it), next(it)) if state_out else (None, None, None)
    (qm_s, s_s, kt_s, hacc_s, gt_s, cumt_s, et_s, wendt_s, pmaxt_s, mloct_s, cumc_s, pmaxc_s, st_s, m_s) = (
        next(it) for _ in range(14))

    n_chunks = blk_len // CHUNK
    cps = seq_len // CHUNK
    n_sub = blk_len // seq_len
    n_rows = n_chunks * GROWS
    lane = lax.broadcasted_iota(jnp.int32, (CHUNK, LANES), 1)
    lane_lo = lane < 64
    trow = lax.broadcasted_iota(jnp.int32, (CHUNK, CHUNK), 0)
    scol = lax.broadcasted_iota(jnp.int32, (CHUNK, CHUNK), 1)
    ones_blk = jnp.ones((CHUNK, LANES), BF16)
    bias_row = par_ref[0, 0:1, :]
    qscale = ML_DK ** -0.5

    def phase_a(c, carry):
        r0 = pl.multiple_of(c * CHUNK, CHUNK)
        for pr in (0, 1):
            qp = q_ref[pl.ds(r0, CHUNK), pr * LANES:(pr + 1) * LANES].astype(F32)
            kp = k_ref[pl.ds(r0, CHUNK), pr * LANES:(pr + 1) * LANES]
            kt_s[c * 2 + pr] = kp.astype(F32).T
            for hh in (0, 1):
                in_h = lane_lo if hh == 0 else jnp.logical_not(lane_lo)
                qm = jnp.where(in_h, qp * qscale, 0.0).astype(BF16)
                qm_s[c * 4 + pr * 2 + hh] = qm
                s_s[c * 4 + pr * 2 + hh] = _dot_nt(qm, kp)
        gk = g_ref[pl.ds(r0, CHUNK), :] + bias_row
        comb = jnp.where(lane < LANE_F, gk, -_softplus(-gk))
        g0 = pl.multiple_of(c * GROWS, GROWS)
        gt_s[pl.ds(g0, GROWS), :] = comb.T[LANE_I:LANE_I + GROWS, :]
        return carry

    lax.fori_loop(0, n_chunks, phase_a, 0)

    g_all = gt_s[...]
    cf, cr = _lane_cumsums(g_all, _tri2())
    rr = lax.broadcasted_iota(jnp.int32, (n_rows, 1), 0) & (GROWS - 1)
    fwd_row = rr < 12
    cum_t = jnp.where(fwd_row, cf, cr)
    tot_t = jnp.where(fwd_row, cum_t[:, CHUNK - 1:CHUNK], cum_t[:, 0:1])
    e_t = pltpu.roll(g_all, 8, 0) - cum_t
    mloc_t = jnp.max(e_t, axis=1, keepdims=True) + tot_t
    lane_b = lax.broadcasted_iota(jnp.int32, (n_rows, LANES), 1)
    pf, pb = e_t, e_t
    k = 1
    while k < CHUNK:
        pf = jnp.maximum(pf, jnp.where(lane_b >= k, pltpu.roll(pf, k, 1), -jnp.inf))
        pb = jnp.maximum(pb, jnp.where(lane_b < CHUNK - k, pltpu.roll(pb, CHUNK - k, 1), -jnp.inf))
        k *= 2
    cumt_s[...] = cum_t
    et_s[...] = e_t * LOG2E
    wendt_s[...] = jnp.exp(e_t + tot_t - mloc_t)
    pmaxt_s[...] = jnp.where(fwd_row, pf, pb) * LOG2E
    mloct_s[...] = jnp.broadcast_to(mloc_t, e_t.shape)

    def phase_c(c, carry):
        g8 = pl.multiple_of(c * GROWS + 8, 8)
        cumc_s[c] = _pad_rows_t(jnp.concatenate([cumt_s[pl.ds(g8, 8), :], mloct_s[pl.ds(g8, 8), :]], axis=0))
        pmaxc_s[c] = _pad_rows_t(pmaxt_s[pl.ds(g8, 8), :])
        return carry

    lax.fori_loop(0, n_chunks, phase_c, 0)

    def scan_direction(sub, d):
        mask = (scol <= trow) if d == 0 else (scol >= trow)
        if zero_init:
            st_s[...] = jnp.zeros(st_s.shape, F32)
            m_s[...] = jnp.zeros(m_s.shape, F32)
        else:
            st_s[...] = s0_ref[sub, 0, d]
            m_s[...] = m0_ref[sub, 0]

        def body(ci, carry):
            c = sub * cps + (ci if d == 0 else cps - 1 - ci)
            r0 = pl.multiple_of(c * CHUNK, CHUNK)
            cumc = cumc_s[c]
            m_prev = m_s[...]
            m_prev2 = m_prev * LOG2E
            mx = jnp.maximum(m_prev2, pmaxc_s[c])
            negmt = -(cumc * LOG2E + mx)
            tot = cumc[CHUNK - 1:CHUNK, :] if d == 0 else cumc[0:1, :]
            mloc = pltpu.roll(cumc[0:1, :], LANES - 8, 1)
            m_new = jnp.maximum(tot + m_prev, mloc)
            a_old = jnp.exp(tot + m_prev - m_new)
            a_loc = jnp.exp(mloc - m_new)
            for pr in (0, 1):
                stp = st_s[pr]
                stp_bf = stp.astype(BF16)
                kt = kt_s[c * 2 + pr]
                for hh in (0, 1):
                    hl = pr * 2 + hh
                    j = d * 4 + hl
                    c0 = hl * LANES
                    e_row = et_s[pl.ds(c * GROWS + 8 + j, 1), :]
                    wend_row = wendt_s[pl.ds(c * GROWS + 8 + j, 1), :]
                    mxb = jnp.broadcast_to(mx[:, j:j + 1], (CHUNK, LANES))
                    sqk = (s_s[c * 4 + hl] * jnp.exp2(jnp.where(mask, e_row - mxb, -jnp.inf))).astype(BF16)
                    qw = qm_s[c * 4 + hl] * jnp.exp2(m_prev2[:, j:j + 1] - mxb).astype(BF16)
                    vaug = jnp.concatenate([v_ref[pl.ds(r0, CHUNK), c0:c0 + LANES], ones_blk], axis=1)
                    nd = _dot(jnp.concatenate([sqk, qw], axis=1), jnp.concatenate([vaug, stp_bf], axis=0))
                    emt = jnp.exp2(jnp.broadcast_to(negmt[:, j:j + 1], (CHUNK, LANES)))
                    den = jnp.maximum(jnp.abs(nd[:, LANES:2 * LANES]), emt)
                    hout = nd[:, 0:LANES] / den
                    if d == 0:
                        hacc_s[pl.ds(r0, CHUNK), c0:c0 + LANES] = hout
                    else:
                        hacc_s[pl.ds(r0, CHUNK), c0:c0 + LANES] += hout
                    ktw = (kt[hh * 64:(hh + 1) * 64, :] * (wend_row * a_loc[:, j:j + 1])).astype(BF16)
                    st_s[pr, hh * 64:(hh + 1) * 64, :] = (a_old[:, j:j + 1] * stp[hh * 64:(hh + 1) * 64, :]
                                                          + _dot(ktw, vaug))
            m_s[...] = m_new
            return carry

        lax.fori_loop(0, cps, body, 0, unroll=min(4, cps))
        if state_out:
            for pr in (0, 1):
                t = st_s[pr, :, 0:LANES].T
                t_hi = pltpu.roll(t, 64, 1)
                cfin_ref[sub, 0, d, pr * 2] = t[:, 0:64]
                cfin_ref[sub, 0, d, pr * 2 + 1] = t_hi[:, 0:64]
                nfin_ref[sub, 0, d, pr:pr + 1, :] = st_s[pr, :, LANES:2 * LANES].T[0:1, :]
            mfin_ref[sub, 0, d:d + 1, :] = m_s[...]

    if state_out == 'first':
        _zero_other_layers(cfin_ref, n_sub)
    if state_out:
        nfin_ref[...] = jnp.zeros(nfin_ref.shape, F32)

    def scan_sequence(sub, carry):
        scan_direction(sub, 0)
        scan_direction(sub, 1)
        return carry

    lax.fori_loop(0, n_sub, scan_sequence, 0)

    def phase_e(c, carry):
        r0 = pl.multiple_of(c * CHUNK, CHUNK)
        for hl in range(4):
            c0 = hl * LANES
            hm_ref[pl.ds(r0, CHUNK), c0:c0 + LANES] = _rms(
                hacc_s[pl.ds(r0, CHUNK), c0:c0 + LANES], nw_ref[0, :, c0:c0 + LANES]).astype(hm_ref.dtype)
        return carry

    lax.fori_loop(0, n_chunks, phase_e, 0)


def _mlstm(proj, gates, par, normw, s0, m0, *, row0, n_seq, seq_len, n_sub=1, state_layer=None,
           state_prev=None, depth=1):
    zero_init = s0 is None
    state_out = None if state_layer is None else ('first' if state_layer == 0 else 'next')
    blk_len = n_sub * seq_len
    rb = row0 // blk_len
    nc = blk_len // CHUNK
    in_specs = [
        pl.BlockSpec((blk_len, 256), lambda b, s: (rb + b, COL_Q // 256 + s)),
        pl.BlockSpec((blk_len, 256), lambda b, s: (rb + b, COL_K // 256 + s)),
        pl.BlockSpec((blk_len, 512), lambda b, s: (rb + b, COL_V // 512 + s)),
        pl.BlockSpec((blk_len, LANES), lambda b, s: (rb + b, s)),
        pl.BlockSpec((1, 8, LANES), lambda b, s: (s, 0, 0)),
        pl.BlockSpec((1, 1, 512), lambda b, s: (s, 0, 0)),
    ]
    args = [proj, proj, proj, gates, par, normw]
    if not zero_init:
        in_specs.append(pl.BlockSpec((n_sub, 1, 2, 2, 128, 256), lambda b, s: (b, s, 0, 0, 0, 0)))
        in_specs.append(pl.BlockSpec((n_sub, 1, 1, LANES), lambda b, s: (b, s, 0, 0)))
        args += [s0, m0]
    out_specs = [pl.BlockSpec((blk_len, 512), lambda b, s: (b, s))]
    out_shape = [jax.ShapeDtypeStruct((n_seq * seq_len, 1024), BF16)]
    aliases = {}
    if state_out:
        layers = depth if state_out == 'first' else 1
        layer0 = state_layer
        out_specs.append(pl.BlockSpec((n_sub, layers, 2, 4, ML_DV, ML_DK), lambda b, s: (b, layer0, 0, s, 0, 0)))
        out_shape.append(jax.ShapeDtypeStruct((n_seq, depth, 2, ML_HEADS, ML_DV, ML_DK), F32))
        out_specs.append(pl.BlockSpec((n_sub, 1, 2, 8, LANES), lambda b, s: (b, s, 0, 0, 0)))
        out_shape.append(jax.ShapeDtypeStruct((n_seq, 2, 2, 8, LANES), F32))
        out_specs.append(pl.BlockSpec((n_sub, 1, 2, LANES), lambda b, s: (b, s, 0, 0)))
        out_shape.append(jax.ShapeDtypeStruct((n_seq, 2, 2, LANES), F32))
        if state_out == 'next':
            aliases = {len(args): 1}
            in_specs.append(pl.BlockSpec(memory_space=pl.ANY))
            args.append(state_prev)
    gate_rows = pltpu.VMEM((nc * GROWS, LANES), F32)
    res = pl.pallas_call(
        functools.partial(_mlstm_kernel, blk_len=blk_len, seq_len=seq_len, zero_init=zero_init,
                          state_out=state_out),
        grid=(n_seq // n_sub, 2),
        in_specs=in_specs,
        out_specs=out_specs,
        out_shape=out_shape,
        input_output_aliases=aliases,
        scratch_shapes=[pltpu.VMEM((nc * 4, CHUNK, LANES), BF16),
                        pltpu.VMEM((nc * 4, CHUNK, CHUNK), F32),
                        pltpu.VMEM((nc * 2, LANES, CHUNK), F32),
                        pltpu.VMEM((blk_len, 512), F32),
                        gate_rows, gate_rows, gate_rows, gate_rows, gate_rows, gate_rows,
                        pltpu.VMEM((nc, CHUNK, LANES), F32),
                        pltpu.VMEM((nc, CHUNK, LANES), F32),
                        pltpu.VMEM((2, 128, 256), F32),
                        pltpu.VMEM((1, LANES), F32)],
        compiler_params=_cparams(("parallel", "arbitrary")),
        name="mlstm_scan",
    )(*args)
    return res if state_out else (res[0], None, None, None)


def _outproj_kernel(yc_ref, yl_ref, z_ref, hc_ref, hl_ref, o_ref, gs_ref, gm_ref, ha_ref, hb_ref, wos_ref,
                    wom_ref, wout_ref, sn_ref, nf_ref, g1_ref, sc2_ref, sh2_ref, hout_ref, u2_ref, *,
                    n_ctx_tiles):
    is_ctx = pl.program_id(0) < n_ctx_tiles
    y_in = jnp.where(is_ctx, yc_ref[...], yl_ref[...]).astype(F32)
    hm_in = jnp.where(is_ctx, hc_ref[...], hl_ref[...]).astype(F32)
    y = y_in * _silu(z_ref[...].astype(F32))
    y = _rms(y, sn_ref[...]).astype(BF16)
    y_ssd = _dot(y, wos_ref[...])
    hm = (hm_in * jax.nn.sigmoid(o_ref[...].astype(F32))).astype(BF16)
    y_ml = _dot(hm, wom_ref[...])
    mix = (jax.nn.sigmoid(gs_ref[...].astype(F32)) * y_ssd
           + jax.nn.sigmoid(gm_ref[...].astype(F32)) * y_ml).astype(BF16)
    h = jnp.where(is_ctx, ha_ref[...], hb_ref[...]) + g1_ref[0] * _dot(mix, wout_ref[...])
    hout_ref[...] = h
    u2_ref[...] = (_rms(h, nf_ref[...]) * (1.0 + sc2_ref[0]) + sh2_ref[0]).astype(u2_ref.dtype)


def _outproj(y_ctx, y_lat, hm_ctx, hm_lat, proj, rows, wos, wom, wout, ssd_norm, norm_ffn, mod48, lat_len,
             tm=512):
    t, d = proj.shape[0], wos.shape[0]
    n_ctx_rows = rows.n_ctx_rows
    nct = n_ctx_rows // tm
    mi = functools.partial(_mod_index, tm=tm, n_ctx_rows=n_ctx_rows, lat_len=lat_len)
    row = lambda i: (i, 0)
    ctx_row = lambda i: (jnp.minimum(i, nct - 1), 0)
    lat_row = lambda i: (jnp.maximum(i - nct, 0), 0)
    const = lambda i: (0, 0)
    col = lambda k: (lambda i: (i, k))
    modspec = lambda which: pl.BlockSpec((1, 1, d), lambda i: (mi(i, which=which), 0, 0))
    return pl.pallas_call(
        functools.partial(_outproj_kernel, n_ctx_tiles=nct),
        grid=(t // tm,),
        in_specs=[pl.BlockSpec((tm, d), ctx_row), pl.BlockSpec((tm, d), lat_row),
                  pl.BlockSpec((tm, d), col(COL_Z // d)),
                  pl.BlockSpec((tm, d), ctx_row), pl.BlockSpec((tm, d), lat_row),
                  pl.BlockSpec((tm, d), col(COL_O // d)),
                  pl.BlockSpec((tm, d), col(COL_GS // d)), pl.BlockSpec((tm, d), col(COL_GM // d))]
        + rows.specs(tm, d) + [
                  pl.BlockSpec((d, d), const), pl.BlockSpec((d, d), const), pl.BlockSpec((d, d), const),
                  pl.BlockSpec((1, d), const), pl.BlockSpec((1, d), const),
                  modspec(2), modspec(4), modspec(3)],
        out_specs=[pl.BlockSpec((tm, d), row), pl.BlockSpec((tm, d), row)],
        out_shape=[jax.ShapeDtypeStruct((t, d), F32), jax.ShapeDtypeStruct((t, d), BF16)],
        compiler_params=_cparams(("parallel",)),
        name="merge_outproj",
    )(y_ctx, y_lat, proj, hm_ctx, hm_lat, proj, proj, proj, rows.ctx, rows.lat, wos, wom, wout,
      ssd_norm.reshape(1, d), norm_ffn.reshape(1, d), mod48, mod48, mod48)


FFN_TM = 512
FFN_TF = 256


def _ffn_kernel(u_ref, ut_ref, ub_ref, wup_ref, cw_ref, wd_ref, h_ref, g2_ref, *rest,
                n_ctx_tiles, ctx_len, tiles_per_seq, final):
    if final:
        fw_ref, yc_ref, yl_ref, act_s = rest
    else:
        o_ref, act_s = rest
    i = pl.program_id(0)
    tm, tf = FFN_TM, FFN_TF
    n_chunks = D_FF // tf
    ext = tm + 2 * GRID_W

    def cw(r, j):
        return cw_ref[r:r + 1, j * tf:(j + 1) * tf]

    @pl.when(i < n_ctx_tiles)
    def _():
        u = u_ref[...]
        pos = lax.broadcasted_iota(jnp.int32, (tm, tf), 0) & (ctx_len - 1)
        for j in range(n_chunks):
            g = _dot(u, wup_ref[:, j * tf:(j + 1) * tf])
            val = _dot(u, wup_ref[:, D_FF + j * tf:D_FF + (j + 1) * tf])
            gl = jnp.where(pos != 0, pltpu.roll(g, 1, 0), 0.0)
            gr = jnp.where(pos != ctx_len - 1, pltpu.roll(g, tm - 1, 0), 0.0)
            conv = gl * cw(3, j) + g * cw(4, j) + gr * cw(5, j) + cw(9, j)
            act_s[:, j * tf:(j + 1) * tf] = (_silu(conv) * val).astype(BF16)

    @pl.when(i >= n_ctx_tiles)
    def _():
        ti = (i - n_ctx_tiles) % tiles_per_seq
        u = u_ref[...]
        top = jnp.where(ti > 0, ut_ref[...], jnp.zeros_like(ut_ref[...]))
        bot = jnp.where(ti < tiles_per_seq - 1, ub_ref[...], jnp.zeros_like(ub_ref[...]))
        uext = jnp.concatenate([top, u, bot], axis=0)
        col = lax.broadcasted_iota(jnp.int32, (ext, tf), 0) & (GRID_W - 1)
        for j in range(n_chunks):
            gx = _dot(uext, wup_ref[:, j * tf:(j + 1) * tf])
            val = _dot(u, wup_ref[:, D_FF + j * tf:D_FF + (j + 1) * tf])
            gl = jnp.where(col != 0, pltpu.roll(gx, 1, 0), 0.0)
            gr = jnp.where(col != GRID_W - 1, pltpu.roll(gx, ext - 1, 0), 0.0)

            def taps(r, lo):
                return (gl[lo:lo + tm] * cw(3 * r, j) + gx[lo:lo + tm] * cw(3 * r + 1, j)
                        + gr[lo:lo + tm] * cw(3 * r + 2, j))

            conv = taps(1, GRID_W) + taps(0, 0) + taps(2, 2 * GRID_W) + cw(9, j)
            act_s[:, j * tf:(j + 1) * tf] = (_silu(conv) * val).astype(BF16)

    h_new = h_ref[...] + g2_ref[0] * _dot(act_s[...], wd_ref[...])
    if final:
        y = _rms(h_new, fw_ref[...])

        @pl.when(i < n_ctx_tiles)
        def _():
            yc_ref[...] = y

        @pl.when(i >= n_ctx_tiles)
        def _():
            yl_ref[...] = y
    else:
        o_ref[...] = h_new


def _ffn(u2, wup, cw16, wd, h, mod48, n_ctx_rows, ctx_len, lat_len, final_w=None):
    t, d = h.shape
    tm = FFN_TM
    nct = n_ctx_rows // tm
    hb = tm // GRID_W
    n_hblocks = t // GRID_W
    final = final_w is not None
    mi = functools.partial(_mod_index, tm=tm, n_ctx_rows=n_ctx_rows, lat_len=lat_len)
    resident = lambda shape: pl.BlockSpec(shape, lambda i: (0, 0), pipeline_mode=pl.Buffered(1))
    in_specs = [pl.BlockSpec((tm, d), lambda i: (i, 0)),
                pl.BlockSpec((GRID_W, d), lambda i: (jnp.maximum(i * hb - 1, 0), 0)),
                pl.BlockSpec((GRID_W, d), lambda i: (jnp.minimum((i + 1) * hb, n_hblocks - 1), 0)),
                resident((d, 2 * D_FF)),
                resident((16, D_FF)),
                resident((D_FF, d)),
                pl.BlockSpec((tm, d), lambda i: (i, 0)),
                pl.BlockSpec((1, 1, d), lambda i: (mi(i, which=5), 0, 0))]
    args = [u2, u2, u2, wup, cw16, wd, h, mod48]
    if final:
        in_specs.append(pl.BlockSpec((1, d), lambda i: (0, 0)))
        args.append(final_w.reshape(1, d))
        out_specs = [pl.BlockSpec((tm, d), lambda i: (jnp.minimum(i, nct - 1), 0)),
                     pl.BlockSpec((tm, d), lambda i: (jnp.maximum(i - nct, 0), 0))]
        out_shape = [jax.ShapeDtypeStruct((n_ctx_rows, d), F32), jax.ShapeDtypeStruct((t - n_ctx_rows, d), F32)]
    else:
        out_specs = pl.BlockSpec((tm, d), lambda i: (i, 0))
        out_shape = jax.ShapeDtypeStruct((t, d), F32)
    return pl.pallas_call(
        functools.partial(_ffn_kernel, n_ctx_tiles=nct, ctx_len=ctx_len, tiles_per_seq=lat_len // tm,
                          final=final),
        grid=(t // tm,),
        in_specs=in_specs,
        out_specs=out_specs,
        out_shape=out_shape,
        scratch_shapes=[pltpu.VMEM((tm, D_FF), BF16)],
        compiler_params=_cparams(("arbitrary",) if final else ("parallel",)),
        name="convffn",
    )(*args)


def _arrange_w_in(w_all):
    n_layers, d, n = w_all.shape
    w = w_all.reshape(n_layers * d, n)
    z, xbc, dt, q, k, v, o, ig, fg, gates = (
        w[:, 0:1024], w[:, 1024:2560], w[:, 2560:2592], w[:, 2592:3104], w[:, 3104:3616],
        w[:, 3616:4640], w[:, 4640:5664], w[:, 5664:5680], w[:, 5680:5696], w[:, 5696:7744])
    big = jnp.concatenate([z, o, gates, xbc, q, k, v], axis=1).astype(BF16)
    blocks = []
    for s in (0, 1):
        blocks += [dt[:, s * 8:s * 8 + 8], dt[:, 16 + s * 8:16 + s * 8 + 8],
                   ig[:, s * 4:s * 4 + 4], ig[:, 8 + s * 4:8 + s * 4 + 4],
                   fg[:, s * 4:s * 4 + 4], fg[:, 8 + s * 4:8 + s * 4 + 4],
                   jnp.zeros((w.shape[0], LANES - 32), w.dtype)]
    small = jnp.concatenate(blocks, axis=1).astype(BF16)
    return big.reshape(n_layers, d, PROJ_COLS), small.reshape(n_layers, d, GATE_COLS)


def _slab_rows(rows, n_layers, n_rows=8):
    ns = 2 * n_layers
    out = []
    for pieces in rows:
        used = sum(p.shape[1] for p in pieces)
        out.append(jnp.concatenate(list(pieces) + [jnp.zeros((ns, LANES - used), F32)], axis=1)[:, None, :])
    out.append(jnp.zeros((ns, n_rows - len(rows), LANES), F32))
    return jnp.concatenate(out, axis=1).reshape(n_layers, 2, n_rows, LANES)


def _ssd_params(dt_bias, a_log, d_skip, conv_w, conv_b):
    nl = dt_bias.shape[0]
    slab = lambda x: x.reshape(2 * nl, 8)
    par = _slab_rows([[slab(dt_bias[:, 0]), slab(dt_bias[:, 1])], [slab(a_log[:, 0]), slab(a_log[:, 1])]], nl)
    dvec = jnp.repeat(d_skip, SSD_HEADDIM, axis=-1).reshape(nl, 2, 1, 512)
    cw = jnp.concatenate([conv_w, conv_b[:, None], jnp.zeros((nl, 4, conv_w.shape[2]), F32)], axis=1)
    cwx = cw[:, :, 0:1024].reshape(nl, 8, 2, 512).transpose(0, 2, 1, 3)
    cwb = cw[:, :, 1024:1280].reshape(nl, 8, 2, LANES).transpose(0, 2, 1, 3)
    cwc = cw[:, :, 1280:1536].reshape(nl, 8, 2, LANES).transpose(0, 2, 1, 3)
    return par, dvec, cwx, cwb, cwc


def _ml_params(i_bias, f_bias, ml_norm):
    nl = i_bias.shape[0]
    slab = lambda x: x.reshape(2 * nl, 4)
    par = _slab_rows([[jnp.zeros((2 * nl, LANE_I), F32), slab(i_bias[:, 0]), slab(i_bias[:, 1]),
                       slab(f_bias[:, 0]), slab(f_bias[:, 1])]], nl)
    return par, ml_norm.reshape(nl, 2, 1, 512)


def _ssd_state_to_kernel(s):
    b, nl = s.shape[0], s.shape[1]
    s = s.reshape(b, nl, 2, 2, 2, 4, 64, 64).transpose(1, 0, 3, 2, 4, 7, 5, 6)
    return s.reshape(nl, b, 2, 2, 128, 256)


def _ml_state_to_kernel(c0, n0, m0):
    b, nl = c0.shape[0], c0.shape[1]
    aug = jnp.concatenate([jnp.swapaxes(c0, -1, -2),
                           jnp.broadcast_to(n0[..., None], n0.shape + (LANES,))], axis=-1)
    aug = aug.reshape(b, nl, 2, 2, 2, 2, 64, 256).transpose(1, 0, 3, 2, 4, 5, 6, 7).reshape(nl, b, 2, 2, 2, 128, 256)
    m = m0.reshape(b, nl, 2, 2, 4).transpose(1, 0, 3, 2, 4).reshape(nl, b, 2, 1, 8)
    mk = jnp.concatenate([m, jnp.zeros((nl, b, 2, 1, LANES - 8), F32)], axis=-1)
    return aug, mk


def _ml_nm_from_kernel(nfin, mfin):
    nl, b = nfin.shape[0], nfin.shape[1]
    n = nfin[:, :, :, :, 0:2, :].reshape(nl, b, 2, 2, 2, 2, ML_DK).transpose(1, 0, 3, 2, 4, 5, 6)
    m = jnp.stack([mfin[:, :, :, 0, 0:4], mfin[:, :, :, 1, 4:8]], axis=2).transpose(1, 0, 2, 3, 4)
    return n.reshape(b, nl, 2, ML_HEADS, ML_DK), m.reshape(b, nl, 2, ML_HEADS)


def kernel(x_prompt, x_sample, state_ssd, state_mlstm_C, state_mlstm_n, state_mlstm_m, c, c_ctx, w_ada, b_ada,
           norm_mix, w_in, ssd_conv_w, ssd_conv_b, ssd_dt_bias, ssd_a_log, ssd_d, ssd_norm, w_o_ssd, ml_i_bias,
           ml_f_bias, ml_norm, w_o_ml, w_out, norm_ffn, w_up, ffn_conv_w, ffn_conv_b, w_down, final_norm):
    nb, ctx_len, d = x_prompt.shape
    nlat, lat_len, _ = x_sample.shape
    depth = w_in.shape[0]
    n_ctx_rows = nb * ctx_len
    n_lat_rows = nlat * lat_len

    n_rows = n_ctx_rows + n_lat_rows
    rows = _Rows(x_prompt.reshape(n_ctx_rows, d), x_sample.reshape(n_lat_rows, d), n_ctx_rows)
    cond8 = jnp.concatenate([c_ctx[None], c, jnp.zeros((8 - 1 - nlat, d), F32)], axis=0)
    mod = _ada(cond8, w_ada, b_ada)

    mod48 = mod.reshape(depth, 48, 1, d)
    w_big, w_small = _arrange_w_in(w_in)
    par_s, dvec, cwx, cwb, cwc = _ssd_params(ssd_dt_bias, ssd_a_log, ssd_d, ssd_conv_w, ssd_conv_b)
    par_m, normw = _ml_params(ml_i_bias, ml_f_bias, ml_norm)
    s0s = _ssd_state_to_kernel(state_ssd)
    s0m, m0m = _ml_state_to_kernel(state_mlstm_C, state_mlstm_n, state_mlstm_m)
    cw16 = jnp.concatenate([ffn_conv_w.reshape(depth, 9, D_FF), ffn_conv_b[:, None],
                            jnp.zeros((depth, 6, D_FF), F32)], axis=1)
    wos, wom, wout = w_o_ssd.astype(BF16), w_o_ml.astype(BF16), w_out.astype(BF16)
    wup, wdn = w_up.astype(BF16), w_down.astype(BF16)

    new_ssd, new_c, n_l, m_l = None, None, [], []
    for l in range(depth):
        proj, gates = _in_proj(rows, norm_mix[l], mod48[l], w_big[l], w_small[l], n_rows, lat_len)

        ssd_par = (par_s[l], dvec[l], cwx[l], cwb[l], cwc[l])
        y_ctx, new_ssd = _ssd(proj, gates, *ssd_par, None, row0=0, n_seq=nb, seq_len=ctx_len,
                              n_sub=CTX_SEQS_PER_STEP, state_layer=l, state_prev=new_ssd, depth=depth)
        y_lat, _ = _ssd(proj, gates, *ssd_par, s0s[l], row0=n_ctx_rows, n_seq=nlat, seq_len=lat_len)
        hm_ctx, new_c, n_ctx, m_ctx = _mlstm(proj, gates, par_m[l], normw[l], None, None, row0=0, n_seq=nb,
                                             seq_len=ctx_len, n_sub=CTX_SEQS_PER_STEP, state_layer=l,
                                             state_prev=new_c, depth=depth)
        hm_lat, _, _, _ = _mlstm(proj, gates, par_m[l], normw[l], s0m[l], m0m[l],
                                 row0=n_ctx_rows, n_seq=nlat, seq_len=lat_len)
        h, u2 = _outproj(y_ctx, y_lat, hm_ctx, hm_lat, proj, rows, wos[l], wom[l], wout[l], ssd_norm[l],
                         norm_ffn[l], mod48[l], lat_len)
        n_l.append(n_ctx)
        m_l.append(m_ctx)
        if l < depth - 1:
            h = _ffn(u2, wup[l], cw16[l], wdn[l], h, mod48[l], n_ctx_rows, ctx_len, lat_len)
            rows = _Rows(h, h, n_ctx_rows)
        else:
            y_ctx_rows, y_lat_rows = _ffn(u2, wup[l], cw16[l], wdn[l], h, mod48[l], n_ctx_rows, ctx_len, lat_len,
                                          final_w=final_norm)

    n_new, m_new = _ml_nm_from_kernel(jnp.stack(n_l), jnp.stack(m_l))
    return (y_ctx_rows.reshape(nb, ctx_len, d), y_lat_rows.reshape(nlat, lat_len, d), new_ssd, new_c, n_new, m_new)
```

```python
import functools

import jax
import jax.numpy as jnp
import numpy as np
from jax import lax
from jax.experimental import pallas as pl
from jax.experimental.pallas import tpu as pltpu

F32 = jnp.float32
BF16 = jnp.bfloat16

D_MODEL = 1024
CHUNK = 128
EPS = 1e-6
GRID_W = 64
SSD_HEADS = 16
SSD_HEADDIM = 64
SSD_STATE = 64
SSD_GROUPS = 4
ML_HEADS = 8
ML_DV = 128
ML_DK = 64
D_FF = 2816

LANES = 128
BF16_SUBLANES = 16
VMEM_LIMIT = 56 * 1024 * 1024

COL_Z, COL_O, COL_GS, COL_GM = 0, 1024, 2048, 3072
COL_X, COL_B, COL_C = 4096, 5120, 5376
COL_Q, COL_K, COL_V = 5632, 6144, 6656
PROJ_COLS = 7680
GATE_COLS = 256
LANE_I = 16
LANE_F = 24
GROWS = 16
LOG2E = 1.4426950408889634
CTX_SEQS_PER_STEP = 8


def _cparams(sem):
    return pltpu.CompilerParams(dimension_semantics=sem, vmem_limit_bytes=VMEM_LIMIT)


def _silu(x):
    return x * jax.nn.sigmoid(x)


def _softplus(x):
    return jnp.maximum(x, 0.0) + jnp.log1p(jnp.exp(-jnp.abs(x)))


def _dot(a, b):
    return jnp.dot(a, b, preferred_element_type=F32)


def _dot_nt(a, b):
    return lax.dot_general(a, b, (((1,), (1,)), ((), ())), preferred_element_type=F32)


def _pair(lane_lo, col0, col1):
    return jnp.where(lane_lo, col0, col1)


def _tri2():
    s = lax.broadcasted_iota(jnp.int32, (CHUNK, 2 * CHUNK), 0)
    t = lax.broadcasted_iota(jnp.int32, (CHUNK, 2 * CHUNK), 1)
    keep = ((t < CHUNK) & (s <= t)) | ((t >= CHUNK) & (s >= t - CHUNK))
    return jnp.where(keep, 1.0, 0.0).astype(BF16)


def _lane_cumsums(x, tri2):
    hi = x.astype(BF16)
    r1 = x - hi.astype(F32)
    mid = r1.astype(BF16)
    lo = (r1 - mid.astype(F32)).astype(BF16)
    cs = _dot(hi, tri2) + _dot(mid, tri2) + _dot(lo, tri2)
    return cs[:, 0:CHUNK], cs[:, CHUNK:2 * CHUNK]


def _pad_rows_t(x):
    return jnp.concatenate([x, jnp.zeros((CHUNK - x.shape[0], LANES), F32)], axis=0).T


def _ada_kernel(c_ref, w_ref, b_ref, o_ref):
    cond = _silu(c_ref[...]).astype(BF16)
    o_ref[0] = _dot(cond, w_ref[0].astype(BF16)) + b_ref[0]


def _ada(cond8, w_ada, b_ada):
    depth, d, n = w_ada.shape
    tn = 1536
    return pl.pallas_call(
        _ada_kernel,
        grid=(depth, n // tn),
        in_specs=[pl.BlockSpec((8, d), lambda l, j: (0, 0)),
                  pl.BlockSpec((1, d, tn), lambda l, j: (l, 0, j)),
                  pl.BlockSpec((1, 1, tn), lambda l, j: (l, 0, j))],
        out_specs=pl.BlockSpec((1, 8, tn), lambda l, j: (l, 0, j)),
        out_shape=jax.ShapeDtypeStruct((depth, 8, n), F32),
        compiler_params=_cparams(("arbitrary", "arbitrary")),
        name="ada_mod",
    )(cond8, w_ada, b_ada.reshape(depth, 1, n))


def _mod_index(i, tm, n_ctx_rows, lat_len, which):
    n_ctx_tiles = n_ctx_rows // tm
    tiles_per_seq = lat_len // tm
    row = jnp.where(i < n_ctx_tiles, 0, 1 + (i - n_ctx_tiles) // tiles_per_seq)
    return row * 6 + which


def _rms(x, w):
    return (x * lax.rsqrt(jnp.mean(x * x, axis=-1, keepdims=True) + EPS)) * w


class _Rows:
    def __init__(self, ctx, lat, n_ctx_rows):
        self.ctx, self.lat, self.n_ctx_rows = ctx, lat, n_ctx_rows
        self.lat_row0 = n_ctx_rows if lat is ctx else 0

    def specs(self, tm, d):
        nct = self.n_ctx_rows // tm
        off = self.lat_row0 // tm
        return [pl.BlockSpec((tm, d), lambda i, *_: (jnp.minimum(i, nct - 1), 0)),
                pl.BlockSpec((tm, d), lambda i, *_: (jnp.maximum(i - nct, 0) + off, 0))]


IN_TM = 1024
IN_TN = 1536


def _in_proj_kernel(ha_ref, hb_ref, nw_ref, sc_ref, sh_ref, w_ref, wg_ref, proj_ref, gate_ref, u_s, *,
                    n_ctx_tiles, n_col_tiles):
    i = pl.program_id(0)
    j = pl.program_id(1)

    @pl.when(j == 0)
    def _():
        x = jnp.where(i < n_ctx_tiles, ha_ref[...], hb_ref[...])
        u_s[...] = (_rms(x, nw_ref[...]) * (1.0 + sc_ref[0]) + sh_ref[0]).astype(u_s.dtype)

    @pl.when(j < n_col_tiles)
    def _():
        proj_ref[...] = _dot(u_s[...], w_ref[...]).astype(proj_ref.dtype)

    @pl.when(j == n_col_tiles)
    def _():
        gate_ref[...] = _dot(u_s[...], wg_ref[...])


def _in_proj(rows, w, mod48, w_big, w_small, n_rows, lat_len):
    d = w.shape[0]
    tm, tn = IN_TM, IN_TN
    nj = PROJ_COLS // tn
    mi = functools.partial(_mod_index, tm=tm, n_ctx_rows=rows.n_ctx_rows, lat_len=lat_len)
    return pl.pallas_call(
        functools.partial(_in_proj_kernel, n_ctx_tiles=rows.n_ctx_rows // tm, n_col_tiles=nj),
        grid=(n_rows // tm, nj + 1),
        in_specs=rows.specs(tm, d) + [
            pl.BlockSpec((1, d), lambda i, j: (0, 0)),
            pl.BlockSpec((1, 1, d), lambda i, j: (mi(i, which=1), 0, 0)),
            pl.BlockSpec((1, 1, d), lambda i, j: (mi(i, which=0), 0, 0)),
            pl.BlockSpec((d, tn), lambda i, j: (0, jnp.minimum(j, nj - 1))),
            pl.BlockSpec((d, GATE_COLS), lambda i, j: (0, 0))],
        out_specs=[pl.BlockSpec((tm, tn), lambda i, j: (i, jnp.minimum(j, nj - 1))),
                   pl.BlockSpec((tm, GATE_COLS), lambda i, j: (i, 0))],
        out_shape=[jax.ShapeDtypeStruct((n_rows, PROJ_COLS), BF16),
                   jax.ShapeDtypeStruct((n_rows, GATE_COLS), F32)],
        scratch_shapes=[pltpu.VMEM((tm, d), BF16)],
        compiler_params=_cparams(("parallel", "arbitrary")),
        name="in_proj",
    )(rows.ctx, rows.lat, w.reshape(1, d), mod48, mod48, w_big, w_small)


def _conv_silu_chunk(src_ref, cw_ref, c, cps, blk_len):
    r0 = pl.multiple_of(c * CHUNK, CHUNK)
    cur = src_ref[pl.ds(r0, CHUNK), :].astype(F32)
    p0 = pl.multiple_of(jnp.maximum(r0 - BF16_SUBLANES, 0), BF16_SUBLANES)
    n0 = pl.multiple_of(jnp.minimum(r0 + CHUNK, blk_len - BF16_SUBLANES), BF16_SUBLANES)
    prv = src_ref[pl.ds(p0, BF16_SUBLANES), :].astype(F32)[BF16_SUBLANES - 1:BF16_SUBLANES, :]
    nxt = src_ref[pl.ds(n0, BF16_SUBLANES), :].astype(F32)[0:1, :]
    cs = c % cps
    prv = jnp.where(cs > 0, prv, 0.0)
    nxt = jnp.where(cs < cps - 1, nxt, 0.0)
    row = lax.broadcasted_iota(jnp.int32, cur.shape, 0)
    up = jnp.where(row == 0, prv, pltpu.roll(cur, 1, 0))
    dn = jnp.where(row == CHUNK - 1, nxt, pltpu.roll(cur, CHUNK - 1, 0))
    y = up * cw_ref[0, 0:1, :] + cur * cw_ref[0, 1:2, :] + dn * cw_ref[0, 2:3, :] + cw_ref[0, 3:4, :]
    return _silu(y)


def _zero_other_layers(ref, n_sub):
    z = jnp.zeros(ref.shape[-2:], F32)
    for sub in range(n_sub):
        for layer in range(1, ref.shape[1]):
            for d in range(ref.shape[2]):
                for h in range(ref.shape[3]):
                    ref[sub, layer, d, h] = z


def _ssd_kernel(*refs, blk_len, seq_len, zero_init, state_out):
    it = iter(refs)
    x_ref, b_ref, c_ref, g_ref = next(it), next(it), next(it), next(it)
    par_ref, d_ref, cwx_ref, cwb_ref, cwc_ref = next(it), next(it), next(it), next(it), next(it)
    s0_ref = None if zero_init else next(it)
    if state_out == 'next':
        next(it)
    y_ref = next(it)
    sfin_ref = next(it) if state_out else None
    (xbd_s, cm_s, g_s, bt_s, yacc_s, cumc_s, at_s, dtt_s, cumt_s, dtet_s, cdt_s, st_s) = (
        next(it) for _ in range(12))

    n_chunks = blk_len // CHUNK
    cps = seq_len // CHUNK
    n_sub = blk_len // seq_len
    lane = lax.broadcasted_iota(jnp.int32, (CHUNK, LANES), 1)
    lane_lo = lane < 64
    lane_row_lo = lax.broadcasted_iota(jnp.int32, (1, LANES), 1) < 64
    trow = lax.broadcasted_iota(jnp.int32, (CHUNK, CHUNK), 0)
    scol = lax.broadcasted_iota(jnp.int32, (CHUNK, CHUNK), 1)
    bias_row = par_ref[0, 0:1, :]
    a_row = -jnp.exp(par_ref[0, 1:2, :])

    def phase_a(c, carry):
        r0 = pl.multiple_of(c * CHUNK, CHUNK)
        xck = _conv_silu_chunk(x_ref, cwx_ref, c, cps, blk_len)
        bck = _conv_silu_chunk(b_ref, cwb_ref, c, cps, blk_len)
        cck = _conv_silu_chunk(c_ref, cwc_ref, c, cps, blk_len)
        yacc_s[pl.ds(r0, CHUNK), :] = d_ref[0] * xck
        for p in range(4):
            xp = xck[:, p * LANES:(p + 1) * LANES]
            xbd_s[c * 4 + p] = jnp.concatenate(
                [jnp.where(lane_lo, xp, 0.0), jnp.where(lane_lo, 0.0, xp)], axis=0).astype(BF16)
        bbf = bck.astype(BF16)
        for g2 in (0, 1):
            in_g = lane_lo if g2 == 0 else jnp.logical_not(lane_lo)
            cm = jnp.where(in_g, cck, 0.0).astype(BF16)
            cm_s[c * 2 + g2] = cm
            g_s[c * 2 + g2] = _dot_nt(cm, bbf)
        bt_s[c] = bck.T
        dt = _softplus(g_ref[pl.ds(r0, CHUNK), :] + bias_row)
        g0 = pl.multiple_of(c * GROWS, GROWS)
        at_s[pl.ds(g0, GROWS), :] = (dt * a_row).T[0:GROWS, :]
        dtt_s[pl.ds(g0, GROWS), :] = dt.T[0:GROWS, :]
        return carry

    lax.fori_loop(0, n_chunks, phase_a, 0)

    cf, cr = _lane_cumsums(at_s[...], _tri2())
    fwd_row = (lax.broadcasted_iota(jnp.int32, (n_chunks * GROWS, 1), 0) & (GROWS - 1)) < 8
    cum_t = jnp.where(fwd_row, cf, cr)
    tot_t = jnp.where(fwd_row, cum_t[:, CHUNK - 1:CHUNK], cum_t[:, 0:1])
    cumt_s[...] = cum_t * LOG2E
    dtet_s[...] = dtt_s[...] * jnp.exp(tot_t - cum_t)
    cdt_s[...] = jnp.broadcast_to(jnp.exp(tot_t), cum_t.shape)

    def phase_c(c, carry):
        g0 = pl.multiple_of(c * GROWS, GROWS)
        cumc_s[c] = _pad_rows_t(cumt_s[pl.ds(g0, GROWS), :])
        return carry

    lax.fori_loop(0, n_chunks, phase_c, 0)

    def scan_direction(sub, d):
        mask = (scol <= trow) if d == 0 else (scol >= trow)
        if zero_init:
            st_s[...] = jnp.zeros(st_s.shape, F32)
        else:
            st_s[...] = s0_ref[sub, 0, d]

        def body(ci, carry):
            c = sub * cps + (ci if d == 0 else cps - 1 - ci)
            r0 = pl.multiple_of(c * CHUNK, CHUNK)
            cumc = cumc_s[c]
            st = st_s[...]
            st_bf = st.astype(BF16)

            def grow(ref, j):
                return ref[pl.ds(c * GROWS + j, 1), :]

            for g2 in (0, 1):
                yoff = _dot(cm_s[c * 2 + g2], st_bf)
                gmat = g_s[c * 2 + g2]
                bt_g = bt_s[c, g2 * 64:(g2 + 1) * 64, :]
                for pr in (0, 1):
                    p = g2 * 2 + pr
                    j0 = d * 8 + g2 * 4 + pr * 2
                    j1 = j0 + 1
                    cb0 = jnp.broadcast_to(cumc[:, j0:j0 + 1], (CHUNK, LANES))
                    cb1 = jnp.broadcast_to(cumc[:, j1:j1 + 1], (CHUNK, LANES))
                    m0 = gmat * jnp.exp2(jnp.where(mask, cb0 - grow(cumt_s, j0), -jnp.inf))
                    m1 = gmat * jnp.exp2(jnp.where(mask, cb1 - grow(cumt_s, j1), -jnp.inf))
                    mcat = jnp.concatenate([(m0 * grow(dtt_s, j0)).astype(BF16),
                                            (m1 * grow(dtt_s, j1)).astype(BF16)], axis=1)
                    xbd = xbd_s[c * 4 + p]
                    yp = _dot(mcat, xbd) + (jnp.exp2(jnp.where(lane_lo, cb0, cb1))
                                            * yoff[:, pr * LANES:(pr + 1) * LANES])
                    c0 = p * LANES
                    if d == 0:
                        yacc_s[pl.ds(r0, CHUNK), c0:c0 + LANES] += yp
                    else:
                        y_ref[pl.ds(r0, CHUNK), c0:c0 + LANES] = (
                            yacc_s[pl.ds(r0, CHUNK), c0:c0 + LANES] + yp).astype(y_ref.dtype)
                    lhs = jnp.concatenate([(bt_g * grow(dtet_s, j0)).astype(BF16),
                                           (bt_g * grow(dtet_s, j1)).astype(BF16)], axis=1)
                    cd = jnp.where(lane_row_lo, grow(cdt_s, j0), grow(cdt_s, j1))
                    st_s[g2 * 64:(g2 + 1) * 64, pr * LANES:(pr + 1) * LANES] = (
                        cd * st[g2 * 64:(g2 + 1) * 64, pr * LANES:(pr + 1) * LANES] + _dot(lhs, xbd))
            return carry

        lax.fori_loop(0, cps, body, 0, unroll=min(4, cps))
        if state_out:
            for half in (0, 1):
                t = st_s[:, half * LANES:(half + 1) * LANES].T
                t_hi = pltpu.roll(t, 64, 1)
                for hloc in (0, 1):
                    hl = half * 2 + hloc
                    sfin_ref[sub, 0, d, hl] = t[hloc * 64:(hloc + 1) * 64, 0:64]
                    sfin_ref[sub, 0, d, 4 + hl] = t_hi[hloc * 64:(hloc + 1) * 64, 0:64]

    if state_out == 'first':
        _zero_other_layers(sfin_ref, n_sub)

    def scan_sequence(sub, carry):
        scan_direction(sub, 0)
        scan_direction(sub, 1)
        return carry

    lax.fori_loop(0, n_sub, scan_sequence, 0)


def _ssd(proj, gates, par, dvec, cwx, cwb, cwc, s0, *, row0, n_seq, seq_len, n_sub=1, state_layer=None,
         state_prev=None, depth=1):
    zero_init = s0 is None
    state_out = None if state_layer is None else ('first' if state_layer == 0 else 'next')
    blk_len = n_sub * seq_len
    rb = row0 // blk_len
    nc = blk_len // CHUNK
    in_specs = [
        pl.BlockSpec((blk_len, 512), lambda b, s: (rb + b, COL_X // 512 + s)),
        pl.BlockSpec((blk_len, LANES), lambda b, s: (rb + b, COL_B // LANES + s)),
        pl.BlockSpec((blk_len, LANES), lambda b, s: (rb + b, COL_C // LANES + s)),
        pl.BlockSpec((blk_len, LANES), lambda b, s: (rb + b, s)),
        pl.BlockSpec((1, 8, LANES), lambda b, s: (s, 0, 0)),
        pl.BlockSpec((1, 1, 512), lambda b, s: (s, 0, 0)),
        pl.BlockSpec((1, 8, 512), lambda b, s: (s, 0, 0)),
        pl.BlockSpec((1, 8, LANES), lambda b, s: (s, 0, 0)),
        pl.BlockSpec((1, 8, LANES), lambda b, s: (s, 0, 0)),
    ]
    args = [proj, proj, proj, gates, par, dvec, cwx, cwb, cwc]
    if not zero_init:
        in_specs.append(pl.BlockSpec((n_sub, 1, 2, 128, 256), lambda b, s: (b, s, 0, 0, 0)))
        args.append(s0)
    out_specs = [pl.BlockSpec((blk_len, 512), lambda b, s: (b, s))]
    out_shape = [jax.ShapeDtypeStruct((n_seq * seq_len, 1024), BF16)]
    aliases = {}
    if state_out:
        layers = depth if state_out == 'first' else 1
        layer0 = state_layer
        out_specs.append(pl.BlockSpec((n_sub, layers, 2, 8, SSD_HEADDIM, SSD_STATE),
                                      lambda b, s: (b, layer0, 0, s, 0, 0)))
        out_shape.append(jax.ShapeDtypeStruct((n_seq, depth, 2, SSD_HEADS, SSD_HEADDIM, SSD_STATE), F32))
        if state_out == 'next':
            aliases = {len(args): 1}
            in_specs.append(pl.BlockSpec(memory_space=pl.ANY))
            args.append(state_prev)
    gate_rows = pltpu.VMEM((nc * GROWS, LANES), F32)
    res = pl.pallas_call(
        functools.partial(_ssd_kernel, blk_len=blk_len, seq_len=seq_len, zero_init=zero_init,
                          state_out=state_out),
        grid=(n_seq // n_sub, 2),
        in_specs=in_specs,
        out_specs=out_specs,
        out_shape=out_shape,
        input_output_aliases=aliases,
        scratch_shapes=[pltpu.VMEM((nc * 4, 2 * CHUNK, LANES), BF16),
                        pltpu.VMEM((nc * 2, CHUNK, LANES), BF16),
                        pltpu.VMEM((nc * 2, CHUNK, CHUNK), F32),
                        pltpu.VMEM((nc, LANES, CHUNK), F32),
                        pltpu.VMEM((blk_len, 512), F32),
                        pltpu.VMEM((nc, CHUNK, LANES), F32),
                        gate_rows, gate_rows, gate_rows, gate_rows, gate_rows,
                        pltpu.VMEM((128, 256), F32)],
        compiler_params=_cparams(("parallel", "arbitrary")),
        name="ssd_scan",
    )(*args)
    return res if state_out else (res[0], None)


def _mlstm_kernel(*refs, blk_len, seq_len, zero_init, state_out):
    it = iter(refs)
    q_ref, k_ref, v_ref, g_ref, par_ref, nw_ref = (next(it) for _ in range(6))
    s0_ref, m0_ref = (None, None) if zero_init else (next(it), next(it))
    if state_out == 'next':
        next(it)
    hm_ref = next(it)
    cfin_ref, nfin_ref, mfin_ref = (next(it), next(it), next(it)) if state_out else (None, None, None)
    (qm_s, s_s, kt_s, hacc_s, gt_s, cumt_s, et_s, wendt_s, pmaxt_s, mloct_s, cumc_s, pmaxc_s, st_s, m_s) = (
        next(it) for _ in range(14))

    n_chunks = blk_len // CHUNK
    cps = seq_len // CHUNK
    n_sub = blk_len // seq_len
    n_rows = n_chunks * GROWS
    lane = lax.broadcasted_iota(jnp.int32, (CHUNK, LANES), 1)
    lane_lo = lane < 64
    trow = lax.broadcasted_iota(jnp.int32, (CHUNK, CHUNK), 0)
    scol = lax.broadcasted_iota(jnp.int32, (CHUNK, CHUNK), 1)
    ones_blk = jnp.ones((CHUNK, LANES), BF16)
    bias_row = par_ref[0, 0:1, :]
    qscale = ML_DK ** -0.5

    def phase_a(c, carry):
        r0 = pl.multiple_of(c * CHUNK, CHUNK)
        for pr in (0, 1):
            qp = q_ref[pl.ds(r0, CHUNK), pr * LANES:(pr + 1) * LANES].astype(F32)
            kp = k_ref[pl.ds(r0, CHUNK), pr * LANES:(pr + 1) * LANES]
            kt_s[c * 2 + pr] = kp.astype(F32).T
            for hh in (0, 1):
                in_h = lane_lo if hh == 0 else jnp.logical_not(lane_lo)
                qm = jnp.where(in_h, qp * qscale, 0.0).astype(BF16)
                qm_s[c * 4 + pr * 2 + hh] = qm
                s_s[c * 4 + pr * 2 + hh] = _dot_nt(qm, kp)
        gk = g_ref[pl.ds(r0, CHUNK), :] + bias_row
        comb = jnp.where(lane < LANE_F, gk, -_softplus(-gk))
        g0 = pl.multiple_of(c * GROWS, GROWS)
        gt_s[pl.ds(g0, GROWS), :] = comb.T[LANE_I:LANE_I + GROWS, :]
        return carry

    lax.fori_loop(0, n_chunks, phase_a, 0)

    g_all = gt_s[...]
    cf, cr = _lane_cumsums(g_all, _tri2())
    rr = lax.broadcasted_iota(jnp.int32, (n_rows, 1), 0) & (GROWS - 1)
    fwd_row = rr < 12
    cum_t = jnp.where(fwd_row, cf, cr)
    tot_t = jnp.where(fwd_row, cum_t[:, CHUNK - 1:CHUNK], cum_t[:, 0:1])
    e_t = pltpu.roll(g_all, 8, 0) - cum_t
    mloc_t = jnp.max(e_t, axis=1, keepdims=True) + tot_t
    lane_b = lax.broadcasted_iota(jnp.int32, (n_rows, LANES), 1)
    pf, pb = e_t, e_t
    k = 1
    while k < CHUNK:
        pf = jnp.maximum(pf, jnp.where(lane_b >= k, pltpu.roll(pf, k, 1), -jnp.inf))
        pb = jnp.maximum(pb, jnp.where(lane_b < CHUNK - k, pltpu.roll(pb, CHUNK - k, 1), -jnp.inf))
        k *= 2
    cumt_s[...] = cum_t
    et_s[...] = e_t * LOG2E
    wendt_s[...] = jnp.exp(e_t + tot_t - mloc_t)
    pmaxt_s[...] = jnp.where(fwd_row, pf, pb) * LOG2E
    mloct_s[...] = jnp.broadcast_to(mloc_t, e_t.shape)

    def phase_c(c, carry):
        g8 = pl.multiple_of(c * GROWS + 8, 8)
        cumc_s[c] = _pad_rows_t(jnp.concatenate([cumt_s[pl.ds(g8, 8), :], mloct_s[pl.ds(g8, 8), :]], axis=0))
        pmaxc_s[c] = _pad_rows_t(pmaxt_s[pl.ds(g8, 8), :])
        return carry

    lax.fori_loop(0, n_chunks, phase_c, 0)

    def scan_direction(sub, d):
        mask = (scol <= trow) if d == 0 else (scol >= trow)
        if zero_init:
            st_s[...] = jnp.zeros(st_s.shape, F32)
            m_s[...] = jnp.zeros(m_s.shape, F32)
        else:
            st_s[...] = s0_ref[sub, 0, d]
            m_s[...] = m0_ref[sub, 0]

        def body(ci, carry):
            c = sub * cps + (ci if d == 0 else cps - 1 - ci)
            r0 = pl.multiple_of(c * CHUNK, CHUNK)
            cumc = cumc_s[c]
            m_prev = m_s[...]
            m_prev2 = m_prev * LOG2E
            mx = jnp.maximum(m_prev2, pmaxc_s[c])
            negmt = -(cumc * LOG2E + mx)
            tot = cumc[CHUNK - 1:CHUNK, :] if d == 0 else cumc[0:1, :]
            mloc = pltpu.roll(cumc[0:1, :], LANES - 8, 1)
            m_new = jnp.maximum(tot + m_prev, mloc)
            a_old = jnp.exp(tot + m_prev - m_new)
            a_loc = jnp.exp(mloc - m_new)
            for pr in (0, 1):
                stp = st_s[pr]
                stp_bf = stp.astype(BF16)
                kt = kt_s[c * 2 + pr]
                for hh in (0, 1):
                    hl = pr * 2 + hh
                    j = d * 4 + hl
                    c0 = hl * LANES
                    e_row = et_s[pl.ds(c * GROWS + 8 + j, 1), :]
                    wend_row = wendt_s[pl.ds(c * GROWS + 8 + j, 1), :]
                    mxb = jnp.broadcast_to(mx[:, j:j + 1], (CHUNK, LANES))
                    sqk = (s_s[c * 4 + hl] * jnp.exp2(jnp.where(mask, e_row - mxb, -jnp.inf))).astype(BF16)
                    qw = qm_s[c * 4 + hl] * jnp.exp2(m_prev2[:, j:j + 1] - mxb).astype(BF16)
                    vaug = jnp.concatenate([v_ref[pl.ds(r0, CHUNK), c0:c0 + LANES], ones_blk], axis=1)
                    nd = _dot(jnp.concatenate([sqk, qw], axis=1), jnp.concatenate([vaug, stp_bf], axis=0))
                    emt = jnp.exp2(jnp.broadcast_to(negmt[:, j:j + 1], (CHUNK, LANES)))
                    den = jnp.maximum(jnp.abs(nd[:, LANES:2 * LANES]), emt)
                    hout = nd[:, 0:LANES] / den
                    if d == 0:
                        hacc_s[pl.ds(r0, CHUNK), c0:c0 + LANES] = hout
                    else:
                        hacc_s[pl.ds(r0, CHUNK), c0:c0 + LANES] += hout
                    ktw = (kt[hh * 64:(hh + 1) * 64, :] * (wend_row * a_loc[:, j:j + 1])).astype(BF16)
                    st_s[pr, hh * 64:(hh + 1) * 64, :] = (a_old[:, j:j + 1] * stp[hh * 64:(hh + 1) * 64, :]
                                                          + _dot(ktw, vaug))
            m_s[...] = m_new
            return carry

        lax.fori_loop(0, cps, body, 0, unroll=min(4, cps))
        if state_out:
            for pr in (0, 1):
                t = st_s[pr, :, 0:LANES].T
                t_hi = pltpu.roll(t, 64, 1)
                cfin_ref[sub, 0, d, pr * 2] = t[:, 0:64]
                cfin_ref[sub, 0, d, pr * 2 + 1] = t_hi[:, 0:64]
                nfin_ref[sub, 0, d, pr:pr + 1, :] = st_s[pr, :, LANES:2 * LANES].T[0:1, :]
            mfin_ref[sub, 0, d:d + 1, :] = m_s[...]

    if state_out == 'first':
        _zero_other_layers(cfin_ref, n_sub)
    if state_out:
        nfin_ref[...] = jnp.zeros(nfin_ref.shape, F32)

    def scan_sequence(sub, carry):
        scan_direction(sub, 0)
        scan_direction(sub, 1)
        return carry

    lax.fori_loop(0, n_sub, scan_sequence, 0)

    def phase_e(c, carry):
        r0 = pl.multiple_of(c * CHUNK, CHUNK)
        for hl in range(4):
            c0 = hl * LANES
            hm_ref[pl.ds(r0, CHUNK), c0:c0 + LANES] = _rms(
                hacc_s[pl.ds(r0, CHUNK), c0:c0 + LANES], nw_ref[0, :, c0:c0 + LANES]).astype(hm_ref.dtype)
        return carry

    lax.fori_loop(0, n_chunks, phase_e, 0)


def _mlstm(proj, gates, par, normw, s0, m0, *, row0, n_seq, seq_len, n_sub=1, state_layer=None,
           state_prev=None, depth=1):
    zero_init = s0 is None
    state_out = None if state_layer is None else ('first' if state_layer == 0 else 'next')
    blk_len = n_sub * seq_len
    rb = row0 // blk_len
    nc = blk_len // CHUNK
    in_specs = [
        pl.BlockSpec((blk_len, 256), lambda b, s: (rb + b, COL_Q // 256 + s)),
        pl.BlockSpec((blk_len, 256), lambda b, s: (rb + b, COL_K // 256 + s)),
        pl.BlockSpec((blk_len, 512), lambda b, s: (rb + b, COL_V // 512 + s)),
        pl.BlockSpec((blk_len, LANES), lambda b, s: (rb + b, s)),
        pl.BlockSpec((1, 8, LANES), lambda b, s: (s, 0, 0)),
        pl.BlockSpec((1, 1, 512), lambda b, s: (s, 0, 0)),
    ]
    args = [proj, proj, proj, gates, par, normw]
    if not zero_init:
        in_specs.append(pl.BlockSpec((n_sub, 1, 2, 2, 128, 256), lambda b, s: (b, s, 0, 0, 0, 0)))
        in_specs.append(pl.BlockSpec((n_sub, 1, 1, LANES), lambda b, s: (b, s, 0, 0)))
        args += [s0, m0]
    out_specs = [pl.BlockSpec((blk_len, 512), lambda b, s: (b, s))]
    out_shape = [jax.ShapeDtypeStruct((n_seq * seq_len, 1024), BF16)]
    aliases = {}
    if state_out:
        layers = depth if state_out == 'first' else 1
        layer0 = state_layer
        out_specs.append(pl.BlockSpec((n_sub, layers, 2, 4, ML_DV, ML_DK), lambda b, s: (b, layer0, 0, s, 0, 0)))
        out_shape.append(jax.ShapeDtypeStruct((n_seq, depth, 2, ML_HEADS, ML_DV, ML_DK), F32))
        out_specs.append(pl.BlockSpec((n_sub, 1, 2, 8, LANES), lambda b, s: (b, s, 0, 0, 0)))
        out_shape.append(jax.ShapeDtypeStruct((n_seq, 2, 2, 8, LANES), F32))
        out_specs.append(pl.BlockSpec((n_sub, 1, 2, LANES), lambda b, s: (b, s, 0, 0)))
        out_shape.append(jax.ShapeDtypeStruct((n_seq, 2, 2, LANES), F32))
        if state_out == 'next':
            aliases = {len(args): 1}
            in_specs.append(pl.BlockSpec(memory_space=pl.ANY))
            args.append(state_prev)
    gate_rows = pltpu.VMEM((nc * GROWS, LANES), F32)
    res = pl.pallas_call(
        functools.partial(_mlstm_kernel, blk_len=blk_len, seq_len=seq_len, zero_init=zero_init,
                          state_out=state_out),
        grid=(n_seq // n_sub, 2),
        in_specs=in_specs,
        out_specs=out_specs,
        out_shape=out_shape,
        input_output_aliases=aliases,
        scratch_shapes=[pltpu.VMEM((nc * 4, CHUNK, LANES), BF16),
                        pltpu.VMEM((nc * 4, CHUNK, CHUNK), F32),
                        pltpu.VMEM((nc * 2, LANES, CHUNK), F32),
                        pltpu.VMEM((blk_len, 512), F32),
                        gate_rows, gate_rows, gate_rows, gate_rows, gate_rows, gate_rows,
                        pltpu.VMEM((nc, CHUNK, LANES), F32),
                        pltpu.VMEM((nc, CHUNK, LANES), F32),
                        pltpu.VMEM((2, 128, 256), F32),
                        pltpu.VMEM((1, LANES), F32)],
        compiler_params=_cparams(("parallel", "arbitrary")),
        name="mlstm_scan",
    )(*args)
    return res if state_out else (res[0], None, None, None)


def _outproj_kernel(yc_ref, yl_ref, z_ref, hc_ref, hl_ref, o_ref, gs_ref, gm_ref, ha_ref, hb_ref, wos_ref,
                    wom_ref, wout_ref, sn_ref, nf_ref, g1_ref, sc2_ref, sh2_ref, hout_ref, u2_ref, *,
                    n_ctx_tiles):
    is_ctx = pl.program_id(0) < n_ctx_tiles
    y_in = jnp.where(is_ctx, yc_ref[...], yl_ref[...]).astype(F32)
    hm_in = jnp.where(is_ctx, hc_ref[...], hl_ref[...]).astype(F32)
    y = y_in * _silu(z_ref[...].astype(F32))
    y = _rms(y, sn_ref[...]).astype(BF16)
    y_ssd = _dot(y, wos_ref[...])
    hm = (hm_in * jax.nn.sigmoid(o_ref[...].astype(F32))).astype(BF16)
    y_ml = _dot(hm, wom_ref[...])
    mix = (jax.nn.sigmoid(gs_ref[...].astype(F32)) * y_ssd
           + jax.nn.sigmoid(gm_ref[...].astype(F32)) * y_ml).astype(BF16)
    h = jnp.where(is_ctx, ha_ref[...], hb_ref[...]) + g1_ref[0] * _dot(mix, wout_ref[...])
    hout_ref[...] = h
    u2_ref[...] = (_rms(h, nf_ref[...]) * (1.0 + sc2_ref[0]) + sh2_ref[0]).astype(u2_ref.dtype)


def _outproj(y_ctx, y_lat, hm_ctx, hm_lat, proj, rows, wos, wom, wout, ssd_norm, norm_ffn, mod48, lat_len,
             tm=512):
    t, d = proj.shape[0], wos.shape[0]
    n_ctx_rows = rows.n_ctx_rows
    nct = n_ctx_rows // tm
    mi = functools.partial(_mod_index, tm=tm, n_ctx_rows=n_ctx_rows, lat_len=lat_len)
    row = lambda i: (i, 0)
    ctx_row = lambda i: (jnp.minimum(i, nct - 1), 0)
    lat_row = lambda i: (jnp.maximum(i - nct, 0), 0)
    const = lambda i: (0, 0)
    col = lambda k: (lambda i: (i, k))
    modspec = lambda which: pl.BlockSpec((1, 1, d), lambda i: (mi(i, which=which), 0, 0))
    return pl.pallas_call(
        functools.partial(_outproj_kernel, n_ctx_tiles=nct),
        grid=(t // tm,),
        in_specs=[pl.BlockSpec((tm, d), ctx_row), pl.BlockSpec((tm, d), lat_row),
                  pl.BlockSpec((tm, d), col(COL_Z // d)),
                  pl.BlockSpec((tm, d), ctx_row), pl.BlockSpec((tm, d), lat_row),
                  pl.BlockSpec((tm, d), col(COL_O // d)),
                  pl.BlockSpec((tm, d), col(COL_GS // d)), pl.BlockSpec((tm, d), col(COL_GM // d))]
        + rows.specs(tm, d) + [
                  pl.BlockSpec((d, d), const), pl.BlockSpec((d, d), const), pl.BlockSpec((d, d), const),
                  pl.BlockSpec((1, d), const), pl.BlockSpec((1, d), const),
                  modspec(2), modspec(4), modspec(3)],
        out_specs=[pl.BlockSpec((tm, d), row), pl.BlockSpec((tm, d), row)],
        out_shape=[jax.ShapeDtypeStruct((t, d), F32), jax.ShapeDtypeStruct((t, d), BF16)],
        compiler_params=_cparams(("parallel",)),
        name="merge_outproj",
    )(y_ctx, y_lat, proj, hm_ctx, hm_lat, proj, proj, proj, rows.ctx, rows.lat, wos, wom, wout,
      ssd_norm.reshape(1, d), norm_ffn.reshape(1, d), mod48, mod48, mod48)


FFN_TM = 512
FFN_TF = 256


def _ffn_kernel(u_ref, ut_ref, ub_ref, wup_ref, cw_ref, wd_ref, h_ref, g2_ref, *rest,
                n_ctx_tiles, ctx_len, tiles_per_seq, final):
    if final:
        fw_ref, yc_ref, yl_ref, act_s = rest
    else:
        o_ref, act_s = rest
    i = pl.program_id(0)
    tm, tf = FFN_TM, FFN_TF
    n_chunks = D_FF // tf
    ext = tm + 2 * GRID_W

    def cw(r, j):
        return cw_ref[r:r + 1, j * tf:(j + 1) * tf]

    @pl.when(i < n_ctx_tiles)
    def _():
        u = u_ref[...]
        pos = lax.broadcasted_iota(jnp.int32, (tm, tf), 0) & (ctx_len - 1)
        for j in range(n_chunks):
            g = _dot(u, wup_ref[:, j * tf:(j + 1) * tf])
            val = _dot(u, wup_ref[:, D_FF + j * tf:D_FF + (j + 1) * tf])
            gl = jnp.where(pos != 0, pltpu.roll(g, 1, 0), 0.0)
            gr = jnp.where(pos != ctx_len - 1, pltpu.roll(g, tm - 1, 0), 0.0)
            conv = gl * cw(3, j) + g * cw(4, j) + gr * cw(5, j) + cw(9, j)
            act_s[:, j * tf:(j + 1) * tf] = (_silu(conv) * val).astype(BF16)

    @pl.when(i >= n_ctx_tiles)
    def _():
        ti = (i - n_ctx_tiles) % tiles_per_seq
        u = u_ref[...]
        top = jnp.where(ti > 0, ut_ref[...], jnp.zeros_like(ut_ref[...]))
        bot = jnp.where(ti < tiles_per_seq - 1, ub_ref[...], jnp.zeros_like(ub_ref[...]))
        uext = jnp.concatenate([top, u, bot], axis=0)
        col = lax.broadcasted_iota(jnp.int32, (ext, tf), 0) & (GRID_W - 1)
        for j in range(n_chunks):
            gx = _dot(uext, wup_ref[:, j * tf:(j + 1) * tf])
            val = _dot(u, wup_ref[:, D_FF + j * tf:D_FF + (j + 1) * tf])
            gl = jnp.where(col != 0, pltpu.roll(gx, 1, 0), 0.0)
            gr = jnp.where(col != GRID_W - 1, pltpu.roll(gx, ext - 1, 0), 0.0)

            def taps(r, lo):
                return (gl[lo:lo + tm] * cw(3 * r, j) + gx[lo:lo + tm] * cw(3 * r + 1, j)
                        + gr[lo:lo + tm] * cw(3 * r + 2, j))

            conv = taps(1, GRID_W) + taps(0, 0) + taps(2, 2 * GRID_W) + cw(9, j)
            act_s[:, j * tf:(j + 1) * tf] = (_silu(conv) * val).astype(BF16)

    h_new = h_ref[...] + g2_ref[0] * _dot(act_s[...], wd_ref[...])
    if final:
        y = _rms(h_new, fw_ref[...])

        @pl.when(i < n_ctx_tiles)
        def _():
            yc_ref[...] = y

        @pl.when(i >= n_ctx_tiles)
        def _():
            yl_ref[...] = y
    else:
        o_ref[...] = h_new


def _ffn(u2, wup, cw16, wd, h, mod48, n_ctx_rows, ctx_len, lat_len, final_w=None):
    t, d = h.shape
    tm = FFN_TM
    nct = n_ctx_rows // tm
    hb = tm // GRID_W
    n_hblocks = t // GRID_W
    final = final_w is not None
    mi = functools.partial(_mod_index, tm=tm, n_ctx_rows=n_ctx_rows, lat_len=lat_len)
    resident = lambda shape: pl.BlockSpec(shape, lambda i: (0, 0), pipeline_mode=pl.Buffered(1))
    in_specs = [pl.BlockSpec((tm, d), lambda i: (i, 0)),
                pl.BlockSpec((GRID_W, d), lambda i: (jnp.maximum(i * hb - 1, 0), 0)),
                pl.BlockSpec((GRID_W, d), lambda i: (jnp.minimum((i + 1) * hb, n_hblocks - 1), 0)),
                resident((d, 2 * D_FF)),
                resident((16, D_FF)),
                resident((D_FF, d)),
                pl.BlockSpec((tm, d), lambda i: (i, 0)),
                pl.BlockSpec((1, 1, d), lambda i: (mi(i, which=5), 0, 0))]
    args = [u2, u2, u2, wup, cw16, wd, h, mod48]
    if final:
        in_specs.append(pl.BlockSpec((1, d), lambda i: (0, 0)))
        args.append(final_w.reshape(1, d))
        out_specs = [pl.BlockSpec((tm, d), lambda i: (jnp.minimum(i, nct - 1), 0)),
                     pl.BlockSpec((tm, d), lambda i: (jnp.maximum(i - nct, 0), 0))]
        out_shape = [jax.ShapeDtypeStruct((n_ctx_rows, d), F32), jax.ShapeDtypeStruct((t - n_ctx_rows, d), F32)]
    else:
        out_specs = pl.BlockSpec((tm, d), lambda i: (i, 0))
        out_shape = jax.ShapeDtypeStruct((t, d), F32)
    return pl.pallas_call(
        functools.partial(_ffn_kernel, n_ctx_tiles=nct, ctx_len=ctx_len, tiles_per_seq=lat_len // tm,
                          final=final),
        grid=(t // tm,),
        in_specs=in_specs,
        out_specs=out_specs,
        out_shape=out_shape,
        scratch_shapes=[pltpu.VMEM((tm, D_FF), BF16)],
        compiler_params=_cparams(("arbitrary",) if final else ("parallel",)),
        name="convffn",
    )(*args)


def _arrange_w_in(w):
    z, xbc, dt, q, k, v, o, ig, fg, gates = (
        w[:, 0:1024], w[:, 1024:2560], w[:, 2560:2592], w[:, 2592:3104], w[:, 3104:3616],
        w[:, 3616:4640], w[:, 4640:5664], w[:, 5664:5680], w[:, 5680:5696], w[:, 5696:7744])
    big = jnp.concatenate([z, o, gates, xbc, q, k, v], axis=1).astype(BF16)
    blocks = []
    for s in (0, 1):
        blocks += [dt[:, s * 8:s * 8 + 8], dt[:, 16 + s * 8:16 + s * 8 + 8],
                   ig[:, s * 4:s * 4 + 4], ig[:, 8 + s * 4:8 + s * 4 + 4],
                   fg[:, s * 4:s * 4 + 4], fg[:, 8 + s * 4:8 + s * 4 + 4],
                   jnp.zeros((w.shape[0], LANES - 32), w.dtype)]
    small = jnp.concatenate(blocks, axis=1).astype(BF16)
    return big, small


def _slab_rows(rows, n_layers, n_rows=8):
    ns = 2 * n_layers
    out = []
    for pieces in rows:
        used = sum(p.shape[1] for p in pieces)
        out.append(jnp.concatenate(list(pieces) + [jnp.zeros((ns, LANES - used), F32)], axis=1)[:, None, :])
    out.append(jnp.zeros((ns, n_rows - len(rows), LANES), F32))
    return jnp.concatenate(out, axis=1).reshape(n_layers, 2, n_rows, LANES)


def _ssd_params(dt_bias, a_log, d_skip, conv_w, conv_b):
    nl = dt_bias.shape[0]
    slab = lambda x: x.reshape(2 * nl, 8)
    par = _slab_rows([[slab(dt_bias[:, 0]), slab(dt_bias[:, 1])], [slab(a_log[:, 0]), slab(a_log[:, 1])]], nl)
    dvec = jnp.repeat(d_skip, SSD_HEADDIM, axis=-1).reshape(nl, 2, 1, 512)
    cw = jnp.concatenate([conv_w, conv_b[:, None], jnp.zeros((nl, 4, conv_w.shape[2]), F32)], axis=1)
    cwx = cw[:, :, 0:1024].reshape(nl, 8, 2, 512).transpose(0, 2, 1, 3)
    cwb = cw[:, :, 1024:1280].reshape(nl, 8, 2, LANES).transpose(0, 2, 1, 3)
    cwc = cw[:, :, 1280:1536].reshape(nl, 8, 2, LANES).transpose(0, 2, 1, 3)
    return par, dvec, cwx, cwb, cwc


def _ml_params(i_bias, f_bias, ml_norm):
    nl = i_bias.shape[0]
    slab = lambda x: x.reshape(2 * nl, 4)
    par = _slab_rows([[jnp.zeros((2 * nl, LANE_I), F32), slab(i_bias[:, 0]), slab(i_bias[:, 1]),
                       slab(f_bias[:, 0]), slab(f_bias[:, 1])]], nl)
    return par, ml_norm.reshape(nl, 2, 1, 512)


def _ssd_state_to_kernel(s):
    b, nl = s.shape[0], s.shape[1]
    s = s.reshape(b, nl, 2, 2, 2, 4, 64, 64).transpose(1, 0, 3, 2, 4, 7, 5, 6)
    return s.reshape(nl, b, 2, 2, 128, 256)


def _ml_state_to_kernel(c0, n0, m0):
    b, nl = c0.shape[0], c0.shape[1]
    aug = jnp.concatenate([jnp.swapaxes(c0, -1, -2),
                           jnp.broadcast_to(n0[..., None], n0.shape + (LANES,))], axis=-1)
    aug = aug.reshape(b, nl, 2, 2, 2, 2, 64, 256).transpose(1, 0, 3, 2, 4, 5, 6, 7).reshape(nl, b, 2, 2, 2, 128, 256)
    m = m0.reshape(b, nl, 2, 2, 4).transpose(1, 0, 3, 2, 4).reshape(nl, b, 2, 1, 8)
    mk = jnp.concatenate([m, jnp.zeros((nl, b, 2, 1, LANES - 8), F32)], axis=-1)
    return aug, mk


def _ml_nm_from_kernel(nfin, mfin):
    nl, b = nfin.shape[0], nfin.shape[1]
    n = nfin[:, :, :, :, 0:2, :].reshape(nl, b, 2, 2, 2, 2, ML_DK).transpose(1, 0, 3, 2, 4, 5, 6)
    m = jnp.stack([mfin[:, :, :, 0, 0:4], mfin[:, :, :, 1, 4:8]], axis=2).transpose(1, 0, 2, 3, 4)
    return n.reshape(b, nl, 2, ML_HEADS, ML_DK), m.reshape(b, nl, 2, ML_HEADS)


def kernel(x_prompt, x_sample, state_ssd, state_mlstm_C, state_mlstm_n, state_mlstm_m, c, c_ctx, w_ada, b_ada,
           norm_mix, w_in, ssd_conv_w, ssd_conv_b, ssd_dt_bias, ssd_a_log, ssd_d, ssd_norm, w_o_ssd, ml_i_bias,
           ml_f_bias, ml_norm, w_o_ml, w_out, norm_ffn, w_up, ffn_conv_w, ffn_conv_b, w_down, final_norm):
    nb, ctx_len, d = x_prompt.shape
    nlat, lat_len, _ = x_sample.shape
    depth = w_in.shape[0]
    n_ctx_rows = nb * ctx_len
    n_lat_rows = nlat * lat_len

    n_rows = n_ctx_rows + n_lat_rows
    rows = _Rows(x_prompt.reshape(n_ctx_rows, d), x_sample.reshape(n_lat_rows, d), n_ctx_rows)
    cond8 = jnp.concatenate([c_ctx[None], c, jnp.zeros((8 - 1 - nlat, d), F32)], axis=0)
    mod = _ada(cond8, w_ada, b_ada)

    mod48 = mod.reshape(depth, 48, 1, d)
    w_big, w_small = zip(*[_arrange_w_in(w_in[l]) for l in range(depth)])
    par_s, dvec, cwx, cwb, cwc = _ssd_params(ssd_dt_bias, ssd_a_log, ssd_d, ssd_conv_w, ssd_conv_b)
    par_m, normw = _ml_params(ml_i_bias, ml_f_bias, ml_norm)
    s0s = _ssd_state_to_kernel(state_ssd)
    s0m, m0m = _ml_state_to_kernel(state_mlstm_C, state_mlstm_n, state_mlstm_m)
    cw16 = jnp.concatenate([ffn_conv_w.reshape(depth, 9, D_FF), ffn_conv_b[:, None],
                            jnp.zeros((depth, 6, D_FF), F32)], axis=1)
    wos, wom, wout = w_o_ssd.astype(BF16), w_o_ml.astype(BF16), w_out.astype(BF16)
    wup, wdn = w_up.astype(BF16), w_down.astype(BF16)

    new_ssd, new_c, n_l, m_l = None, None, [], []
    for l in range(depth):
        proj, gates = _in_proj(rows, norm_mix[l], mod48[l], w_big[l], w_small[l], n_rows, lat_len)

        ssd_par = (par_s[l], dvec[l], cwx[l], cwb[l], cwc[l])
        y_ctx, new_ssd = _ssd(proj, gates, *ssd_par, None, row0=0, n_seq=nb, seq_len=ctx_len,
                              n_sub=CTX_SEQS_PER_STEP, state_layer=l, state_prev=new_ssd, depth=depth)
        y_lat, _ = _ssd(proj, gates, *ssd_par, s0s[l], row0=n_ctx_rows, n_seq=nlat, seq_len=lat_len)
        hm_ctx, new_c, n_ctx, m_ctx = _mlstm(proj, gates, par_m[l], normw[l], None, None, row0=0, n_seq=nb,
                                             seq_len=ctx_len, n_sub=CTX_SEQS_PER_STEP, state_layer=l,
                                             state_prev=new_c, depth=depth)
        hm_lat, _, _, _ = _mlstm(proj, gates, par_m[l], normw[l], s0m[l], m0m[l],
                                 row0=n_ctx_rows, n_seq=nlat, seq_len=lat_len)
        h, u2 = _outproj(y_ctx, y_lat, hm_ctx, hm_lat, proj, rows, wos[l], wom[l], wout[l], ssd_norm[l],
                         norm_ffn[l], mod48[l], lat_len)
        n_l.append(n_ctx)
        m_l.append(m_ctx)
        if l < depth - 1:
            h = _ffn(u2, wup[l], cw16[l], wdn[l], h, mod48[l], n_ctx_rows, ctx_len, lat_len)
            rows = _Rows(h, h, n_ctx_rows)
        else:
            y_ctx_rows, y_lat_rows = _ffn(u2, wup[l], cw16[l], wdn[l], h, mod48[l], n_ctx_rows, ctx_len, lat_len,
                                          final_w=final_norm)

    n_new, m_new = _ml_nm_from_kernel(jnp.stack(n_l), jnp.stack(m_l))
    return (y_ctx_rows.reshape(nb, ctx_len, d), y_lat_rows.reshape(nlat, lat_len, d), new_ssd, new_c, n_new, m_new)
```

```python
import functools

import jax
import jax.numpy as jnp
import numpy as np
from jax import lax
from jax.experimental import pallas as pl
from jax.experimental.pallas import tpu as pltpu

F32 = jnp.float32
BF16 = jnp.bfloat16

D_MODEL = 1024
CHUNK = 128
EPS = 1e-6
GRID_W = 64
SSD_HEADS = 16
SSD_HEADDIM = 64
SSD_STATE = 64
SSD_GROUPS = 4
ML_HEADS = 8
ML_DV = 128
ML_DK = 64
D_FF = 2816

LANES = 128
BF16_SUBLANES = 16
VMEM_LIMIT = 56 * 1024 * 1024

COL_Z, COL_O, COL_GS, COL_GM = 0, 1024, 2048, 3072
COL_X, COL_B, COL_C = 4096, 5120, 5376
COL_Q, COL_K, COL_V = 5632, 6144, 6656
PROJ_COLS = 7680
GATE_COLS = 256
LANE_I = 16
LANE_F = 24
GROWS = 16
LOG2E = 1.4426950408889634
CTX_SEQS_PER_STEP = 8


def _cparams(sem):
    return pltpu.CompilerParams(dimension_semantics=sem, vmem_limit_bytes=VMEM_LIMIT)


def _silu(x):
    return x * jax.nn.sigmoid(x)


def _softplus(x):
    return jnp.maximum(x, 0.0) + jnp.log1p(jnp.exp(-jnp.abs(x)))


def _dot(a, b):
    return jnp.dot(a, b, preferred_element_type=F32)


def _dot_nt(a, b):
    return lax.dot_general(a, b, (((1,), (1,)), ((), ())), preferred_element_type=F32)


def _pair(lane_lo, col0, col1):
    return jnp.where(lane_lo, col0, col1)


def _tri2():
    s = lax.broadcasted_iota(jnp.int32, (CHUNK, 2 * CHUNK), 0)
    t = lax.broadcasted_iota(jnp.int32, (CHUNK, 2 * CHUNK), 1)
    keep = ((t < CHUNK) & (s <= t)) | ((t >= CHUNK) & (s >= t - CHUNK))
    return jnp.where(keep, 1.0, 0.0).astype(BF16)


def _lane_cumsums(x, tri2):
    hi = x.astype(BF16)
    r1 = x - hi.astype(F32)
    mid = r1.astype(BF16)
    lo = (r1 - mid.astype(F32)).astype(BF16)
    cs = _dot(hi, tri2) + _dot(mid, tri2) + _dot(lo, tri2)
    return cs[:, 0:CHUNK], cs[:, CHUNK:2 * CHUNK]


def _pad_rows_t(x):
    return jnp.concatenate([x, jnp.zeros((CHUNK - x.shape[0], LANES), F32)], axis=0).T


def _ada_kernel(c_ref, w_ref, b_ref, o_ref):
    cond = _silu(c_ref[...]).astype(BF16)
    o_ref[0] = _dot(cond, w_ref[0].astype(BF16)) + b_ref[0]


def _ada(cond8, w_ada, b_ada):
    depth, d, n = w_ada.shape
    tn = 1536
    return pl.pallas_call(
        _ada_kernel,
        grid=(depth, n // tn),
        in_specs=[pl.BlockSpec((8, d), lambda l, j: (0, 0)),
                  pl.BlockSpec((1, d, tn), lambda l, j: (l, 0, j)),
                  pl.BlockSpec((1, 1, tn), lambda l, j: (l, 0, j))],
        out_specs=pl.BlockSpec((1, 8, tn), lambda l, j: (l, 0, j)),
        out_shape=jax.ShapeDtypeStruct((depth, 8, n), F32),
        compiler_params=_cparams(("arbitrary", "arbitrary")),
        name="ada_mod",
    )(cond8, w_ada, b_ada.reshape(depth, 1, n))


def _mod_index(i, tm, n_ctx_rows, lat_len, which):
    n_ctx_tiles = n_ctx_rows // tm
    tiles_per_seq = lat_len // tm
    row = jnp.where(i < n_ctx_tiles, 0, 1 + (i - n_ctx_tiles) // tiles_per_seq)
    return row * 6 + which


def _rms(x, w):
    return (x * lax.rsqrt(jnp.mean(x * x, axis=-1, keepdims=True) + EPS)) * w


class _Rows:
    def __init__(self, ctx, lat, n_ctx_rows):
        self.ctx, self.lat, self.n_ctx_rows = ctx, lat, n_ctx_rows
        self.lat_row0 = n_ctx_rows if lat is ctx else 0

    def specs(self, tm, d):
        nct = self.n_ctx_rows // tm
        off = self.lat_row0 // tm
        return [pl.BlockSpec((tm, d), lambda i, *_: (jnp.minimum(i, nct - 1), 0)),
                pl.BlockSpec((tm, d), lambda i, *_: (jnp.maximum(i - nct, 0) + off, 0))]


IN_TM = 1024
IN_TN = 2560


def _in_proj_kernel(ha_ref, hb_ref, nw_ref, sc_ref, sh_ref, w_ref, wg_ref, proj_ref, gate_ref, u_s, *,
                    n_ctx_tiles, n_col_tiles):
    i = pl.program_id(0)
    j = pl.program_id(1)

    @pl.when(j == 0)
    def _():
        x = jnp.where(i < n_ctx_tiles, ha_ref[...], hb_ref[...])
        u_s[...] = (_rms(x, nw_ref[...]) * (1.0 + sc_ref[0]) + sh_ref[0]).astype(u_s.dtype)

    @pl.when(j < n_col_tiles)
    def _():
        proj_ref[...] = _dot(u_s[...], w_ref[...]).astype(proj_ref.dtype)

    @pl.when(j == n_col_tiles)
    def _():
        gate_ref[...] = _dot(u_s[...], wg_ref[...])


def _in_proj(rows, w, mod48, w_big, w_small, n_rows, lat_len):
    d = w.shape[0]
    tm, tn = IN_TM, IN_TN
    nj = PROJ_COLS // tn
    mi = functools.partial(_mod_index, tm=tm, n_ctx_rows=rows.n_ctx_rows, lat_len=lat_len)
    return pl.pallas_call(
        functools.partial(_in_proj_kernel, n_ctx_tiles=rows.n_ctx_rows // tm, n_col_tiles=nj),
        grid=(n_rows // tm, nj + 1),
        in_specs=rows.specs(tm, d) + [
            pl.BlockSpec((1, d), lambda i, j: (0, 0)),
            pl.BlockSpec((1, 1, d), lambda i, j: (mi(i, which=1), 0, 0)),
            pl.BlockSpec((1, 1, d), lambda i, j: (mi(i, which=0), 0, 0)),
            pl.BlockSpec((d, tn), lambda i, j: (0, jnp.minimum(j, nj - 1))),
            pl.BlockSpec((d, GATE_COLS), lambda i, j: (0, 0))],
        out_specs=[pl.BlockSpec((tm, tn), lambda i, j: (i, jnp.minimum(j, nj - 1))),
                   pl.BlockSpec((tm, GATE_COLS), lambda i, j: (i, 0))],
        out_shape=[jax.ShapeDtypeStruct((n_rows, PROJ_COLS), BF16),
                   jax.ShapeDtypeStruct((n_rows, GATE_COLS), F32)],
        scratch_shapes=[pltpu.VMEM((tm, d), BF16)],
        compiler_params=_cparams(("parallel", "arbitrary")),
        name="in_proj",
    )(rows.ctx, rows.lat, w.reshape(1, d), mod48, mod48, w_big, w_small)


def _conv_silu_chunk(src_ref, cw_ref, c, cps, blk_len):
    r0 = pl.multiple_of(c * CHUNK, CHUNK)
    cur = src_ref[pl.ds(r0, CHUNK), :].astype(F32)
    p0 = pl.multiple_of(jnp.maximum(r0 - BF16_SUBLANES, 0), BF16_SUBLANES)
    n0 = pl.multiple_of(jnp.minimum(r0 + CHUNK, blk_len - BF16_SUBLANES), BF16_SUBLANES)
    prv = src_ref[pl.ds(p0, BF16_SUBLANES), :].astype(F32)[BF16_SUBLANES - 1:BF16_SUBLANES, :]
    nxt = src_ref[pl.ds(n0, BF16_SUBLANES), :].astype(F32)[0:1, :]
    cs = c % cps
    prv = jnp.where(cs > 0, prv, 0.0)
    nxt = jnp.where(cs < cps - 1, nxt, 0.0)
    row = lax.broadcasted_iota(jnp.int32, cur.shape, 0)
    up = jnp.where(row == 0, prv, pltpu.roll(cur, 1, 0))
    dn = jnp.where(row == CHUNK - 1, nxt, pltpu.roll(cur, CHUNK - 1, 0))
    y = up * cw_ref[0, 0:1, :] + cur * cw_ref[0, 1:2, :] + dn * cw_ref[0, 2:3, :] + cw_ref[0, 3:4, :]
    return _silu(y)


def _zero_other_layers(ref, n_sub):
    z = jnp.zeros(ref.shape[-2:], F32)
    for sub in range(n_sub):
        for layer in range(1, ref.shape[1]):
            for d in range(ref.shape[2]):
                for h in range(ref.shape[3]):
                    ref[sub, layer, d, h] = z


def _ssd_kernel(*refs, blk_len, seq_len, zero_init, state_out):
    it = iter(refs)
    x_ref, b_ref, c_ref, g_ref = next(it), next(it), next(it), next(it)
    par_ref, d_ref, cwx_ref, cwb_ref, cwc_ref = next(it), next(it), next(it), next(it), next(it)
    s0_ref = None if zero_init else next(it)
    if state_out == 'next':
        next(it)
    y_ref = next(it)
    sfin_ref = next(it) if state_out else None
    (xbd_s, cm_s, g_s, bt_s, yacc_s, cumc_s, at_s, dtt_s, cumt_s, dtet_s, cdt_s, st_s) = (
        next(it) for _ in range(12))

    n_chunks = blk_len // CHUNK
    cps = seq_len // CHUNK
    n_sub = blk_len // seq_len
    lane = lax.broadcasted_iota(jnp.int32, (CHUNK, LANES), 1)
    lane_lo = lane < 64
    lane_row_lo = lax.broadcasted_iota(jnp.int32, (1, LANES), 1) < 64
    trow = lax.broadcasted_iota(jnp.int32, (CHUNK, CHUNK), 0)
    scol = lax.broadcasted_iota(jnp.int32, (CHUNK, CHUNK), 1)
    bias_row = par_ref[0, 0:1, :]
    a_row = -jnp.exp(par_ref[0, 1:2, :])

    def phase_a(c, carry):
        r0 = pl.multiple_of(c * CHUNK, CHUNK)
        xck = _conv_silu_chunk(x_ref, cwx_ref, c, cps, blk_len)
        bck = _conv_silu_chunk(b_ref, cwb_ref, c, cps, blk_len)
        cck = _conv_silu_chunk(c_ref, cwc_ref, c, cps, blk_len)
        yacc_s[pl.ds(r0, CHUNK), :] = d_ref[0] * xck
        for p in range(4):
            xp = xck[:, p * LANES:(p + 1) * LANES]
            xbd_s[c * 4 + p] = jnp.concatenate(
                [jnp.where(lane_lo, xp, 0.0), jnp.where(lane_lo, 0.0, xp)], axis=0).astype(BF16)
        bbf = bck.astype(BF16)
        for g2 in (0, 1):
            in_g = lane_lo if g2 == 0 else jnp.logical_not(lane_lo)
            cm = jnp.where(in_g, cck, 0.0).astype(BF16)
            cm_s[c * 2 + g2] = cm
            g_s[c * 2 + g2] = _dot_nt(cm, bbf)
        bt_s[c] = bck.T
        dt = _softplus(g_ref[pl.ds(r0, CHUNK), :] + bias_row)
        g0 = pl.multiple_of(c * GROWS, GROWS)
        at_s[pl.ds(g0, GROWS), :] = (dt * a_row).T[0:GROWS, :]
        dtt_s[pl.ds(g0, GROWS), :] = dt.T[0:GROWS, :]
        return carry

    lax.fori_loop(0, n_chunks, phase_a, 0)

    cf, cr = _lane_cumsums(at_s[...], _tri2())
    fwd_row = (lax.broadcasted_iota(jnp.int32, (n_chunks * GROWS, 1), 0) & (GROWS - 1)) < 8
    cum_t = jnp.where(fwd_row, cf, cr)
    tot_t = jnp.where(fwd_row, cum_t[:, CHUNK - 1:CHUNK], cum_t[:, 0:1])
    cumt_s[...] = cum_t * LOG2E
    dtet_s[...] = dtt_s[...] * jnp.exp(tot_t - cum_t)
    cdt_s[...] = jnp.broadcast_to(jnp.exp(tot_t), cum_t.shape)

    def phase_c(c, carry):
        g0 = pl.multiple_of(c * GROWS, GROWS)
        cumc_s[c] = _pad_rows_t(cumt_s[pl.ds(g0, GROWS), :])
        return carry

    lax.fori_loop(0, n_chunks, phase_c, 0)

    def scan_direction(sub, d):
        mask = (scol <= trow) if d == 0 else (scol >= trow)
        if zero_init:
            st_s[...] = jnp.zeros(st_s.shape, F32)
        else:
            st_s[...] = s0_ref[sub, 0, d]

        def body(ci, carry):
            c = sub * cps + (ci if d == 0 else cps - 1 - ci)
            r0 = pl.multiple_of(c * CHUNK, CHUNK)
            cumc = cumc_s[c]
            st = st_s[...]
            st_bf = st.astype(BF16)

            def grow(ref, j):
                return ref[pl.ds(c * GROWS + j, 1), :]

            for g2 in (0, 1):
                yoff = _dot(cm_s[c * 2 + g2], st_bf)
                gmat = g_s[c * 2 + g2]
                bt_g = bt_s[c, g2 * 64:(g2 + 1) * 64, :]
                for pr in (0, 1):
                    p = g2 * 2 + pr
                    j0 = d * 8 + g2 * 4 + pr * 2
                    j1 = j0 + 1
                    cb0 = jnp.broadcast_to(cumc[:, j0:j0 + 1], (CHUNK, LANES))
                    cb1 = jnp.broadcast_to(cumc[:, j1:j1 + 1], (CHUNK, LANES))
                    m0 = gmat * jnp.exp2(jnp.where(mask, cb0 - grow(cumt_s, j0), -jnp.inf))
                    m1 = gmat * jnp.exp2(jnp.where(mask, cb1 - grow(cumt_s, j1), -jnp.inf))
                    mcat = jnp.concatenate([(m0 * grow(dtt_s, j0)).astype(BF16),
                                            (m1 * grow(dtt_s, j1)).astype(BF16)], axis=1)
                    xbd = xbd_s[c * 4 + p]
                    yp = _dot(mcat, xbd) + (jnp.exp2(jnp.where(lane_lo, cb0, cb1))
                                            * yoff[:, pr * LANES:(pr + 1) * LANES])
                    c0 = p * LANES
                    if d == 0:
                        yacc_s[pl.ds(r0, CHUNK), c0:c0 + LANES] += yp
                    else:
                        y_ref[pl.ds(r0, CHUNK), c0:c0 + LANES] = (
                            yacc_s[pl.ds(r0, CHUNK), c0:c0 + LANES] + yp).astype(y_ref.dtype)
                    lhs = jnp.concatenate([(bt_g * grow(dtet_s, j0)).astype(BF16),
                                           (bt_g * grow(dtet_s, j1)).astype(BF16)], axis=1)
                    cd = jnp.where(lane_row_lo, grow(cdt_s, j0), grow(cdt_s, j1))
                    st_s[g2 * 64:(g2 + 1) * 64, pr * LANES:(pr + 1) * LANES] = (
                        cd * st[g2 * 64:(g2 + 1) * 64, pr * LANES:(pr + 1) * LANES] + _dot(lhs, xbd))
            return carry

        lax.fori_loop(0, cps, body, 0, unroll=min(4, cps))
        if state_out:
            for half in (0, 1):
                t = st_s[:, half * LANES:(half + 1) * LANES].T
                t_hi = pltpu.roll(t, 64, 1)
                for hloc in (0, 1):
                    hl = half * 2 + hloc
                    sfin_ref[sub, 0, d, hl] = t[hloc * 64:(hloc + 1) * 64, 0:64]
                    sfin_ref[sub, 0, d, 4 + hl] = t_hi[hloc * 64:(hloc + 1) * 64, 0:64]

    if state_out == 'first':
        _zero_other_layers(sfin_ref, n_sub)

    def scan_sequence(sub, carry):
        scan_direction(sub, 0)
        scan_direction(sub, 1)
        return carry

    lax.fori_loop(0, n_sub, scan_sequence, 0)


def _ssd(proj, gates, par, dvec, cwx, cwb, cwc, s0, *, row0, n_seq, seq_len, n_sub=1, state_layer=None,
         state_prev=None, depth=1):
    zero_init = s0 is None
    state_out = None if state_layer is None else ('first' if state_layer == 0 else 'next')
    blk_len = n_sub * seq_len
    rb = row0 // blk_len
    nc = blk_len // CHUNK
    in_specs = [
        pl.BlockSpec((blk_len, 512), lambda b, s: (rb + b, COL_X // 512 + s)),
        pl.BlockSpec((blk_len, LANES), lambda b, s: (rb + b, COL_B // LANES + s)),
        pl.BlockSpec((blk_len, LANES), lambda b, s: (rb + b, COL_C // LANES + s)),
        pl.BlockSpec((blk_len, LANES), lambda b, s: (rb + b, s)),
        pl.BlockSpec((1, 8, LANES), lambda b, s: (s, 0, 0)),
        pl.BlockSpec((1, 1, 512), lambda b, s: (s, 0, 0)),
        pl.BlockSpec((1, 8, 512), lambda b, s: (s, 0, 0)),
        pl.BlockSpec((1, 8, LANES), lambda b, s: (s, 0, 0)),
        pl.BlockSpec((1, 8, LANES), lambda b, s: (s, 0, 0)),
    ]
    args = [proj, proj, proj, gates, par, dvec, cwx, cwb, cwc]
    if not zero_init:
        in_specs.append(pl.BlockSpec((n_sub, 1, 2, 128, 256), lambda b, s: (b, s, 0, 0, 0)))
        args.append(s0)
    out_specs = [pl.BlockSpec((blk_len, 512), lambda b, s: (b, s))]
    out_shape = [jax.ShapeDtypeStruct((n_seq * seq_len, 1024), BF16)]
    aliases = {}
    if state_out:
        layers = depth if state_out == 'first' else 1
        layer0 = state_layer
        out_specs.append(pl.BlockSpec((n_sub, layers, 2, 8, SSD_HEADDIM, SSD_STATE),
                                      lambda b, s: (b, layer0, 0, s, 0, 0)))
        out_shape.append(jax.ShapeDtypeStruct((n_seq, depth, 2, SSD_HEADS, SSD_HEADDIM, SSD_STATE), F32))
        if state_out == 'next':
            aliases = {len(args): 1}
            in_specs.append(pl.BlockSpec(memory_space=pl.ANY))
            args.append(state_prev)
    gate_rows = pltpu.VMEM((nc * GROWS, LANES), F32)
    res = pl.pallas_call(
        functools.partial(_ssd_kernel, blk_len=blk_len, seq_len=seq_len, zero_init=zero_init,
                          state_out=state_out),
        grid=(n_seq // n_sub, 2),
        in_specs=in_specs,
        out_specs=out_specs,
        out_shape=out_shape,
        input_output_aliases=aliases,
        scratch_shapes=[pltpu.VMEM((nc * 4, 2 * CHUNK, LANES), BF16),
                        pltpu.VMEM((nc * 2, CHUNK, LANES), BF16),
                        pltpu.VMEM((nc * 2, CHUNK, CHUNK), F32),
                        pltpu.VMEM((nc, LANES, CHUNK), F32),
                        pltpu.VMEM((blk_len, 512), F32),
                        pltpu.VMEM((nc, CHUNK, LANES), F32),
                        gate_rows, gate_rows, gate_rows, gate_rows, gate_rows,
                        pltpu.VMEM((128, 256), F32)],
        compiler_params=_cparams(("parallel", "arbitrary")),
        name="ssd_scan",
    )(*args)
    return res if state_out else (res[0], None)


def _mlstm_kernel(*refs, blk_len, seq_len, zero_init, state_out):
    it = iter(refs)
    q_ref, k_ref, v_ref, g_ref, par_ref, nw_ref = (next(it) for _ in range(6))
    s0_ref, m0_ref = (None, None) if zero_init else (next(it), next(it))
    if state_out == 'next':
        next(it)
    hm_ref = next(it)
    cfin_ref, nfin_ref, mfin_ref = (next(it), next(it), next(it)) if state_out else (None, None, None)
    (qm_s, s_s, kt_s, hacc_s, gt_s, cumt_s, et_s, wendt_s, pmaxt_s, mloct_s, cumc_s, pmaxc_s, st_s, m_s) = (
        next(it) for _ in range(14))

    n_chunks = blk_len // CHUNK
    cps = seq_len // CHUNK
    n_sub = blk_len // seq_len
    n_rows = n_chunks * GROWS
    lane = lax.broadcasted_iota(jnp.int32, (CHUNK, LANES), 1)
    lane_lo = lane < 64
    trow = lax.broadcasted_iota(jnp.int32, (CHUNK, CHUNK), 0)
    scol = lax.broadcasted_iota(jnp.int32, (CHUNK, CHUNK), 1)
    ones_blk = jnp.ones((CHUNK, LANES), BF16)
    bias_row = par_ref[0, 0:1, :]
    qscale = ML_DK ** -0.5

    def phase_a(c, carry):
        r0 = pl.multiple_of(c * CHUNK, CHUNK)
        for pr in (0, 1):
            qp = q_ref[pl.ds(r0, CHUNK), pr * LANES:(pr + 1) * LANES].astype(F32)
            kp = k_ref[pl.ds(r0, CHUNK), pr * LANES:(pr + 1) * LANES]
            kt_s[c * 2 + pr] = kp.astype(F32).T
            for hh in (0, 1):
                in_h = lane_lo if hh == 0 else jnp.logical_not(lane_lo)
                qm = jnp.where(in_h, qp * qscale, 0.0).astype(BF16)
                qm_s[c * 4 + pr * 2 + hh] = qm
                s_s[c * 4 + pr * 2 + hh] = _dot_nt(qm, kp)
        gk = g_ref[pl.ds(r0, CHUNK), :] + bias_row
        comb = jnp.where(lane < LANE_F, gk, -_softplus(-gk))
        g0 = pl.multiple_of(c * GROWS, GROWS)
        gt_s[pl.ds(g0, GROWS), :] = comb.T[LANE_I:LANE_I + GROWS, :]
        return carry

    lax.fori_loop(0, n_chunks, phase_a, 0)

    g_all = gt_s[...]
    cf, cr = _lane_cumsums(g_all, _tri2())
    rr = lax.broadcasted_iota(jnp.int32, (n_rows, 1), 0) & (GROWS - 1)
    fwd_row = rr < 12
    cum_t = jnp.where(fwd_row, cf, cr)
    tot_t = jnp.where(fwd_row, cum_t[:, CHUNK - 1:CHUNK], cum_t[:, 0:1])
    e_t = pltpu.roll(g_all, 8, 0) - cum_t
    mloc_t = jnp.max(e_t, axis=1, keepdims=True) + tot_t
    lane_b = lax.broadcasted_iota(jnp.int32, (n_rows, LANES), 1)
    pf, pb = e_t, e_t
    k = 1
    while k < CHUNK:
        pf = jnp.maximum(pf, jnp.where(lane_b >= k, pltpu.roll(pf, k, 1), -jnp.inf))
        pb = jnp.maximum(pb, jnp.where(lane_b < CHUNK - k, pltpu.roll(pb, CHUNK - k, 1), -jnp.inf))
        k *= 2
    cumt_s[...] = cum_t
    et_s[...] = e_t * LOG2E
    wendt_s[...] = jnp.exp(e_t + tot_t - mloc_t)
    pmaxt_s[...] = jnp.where(fwd_row, pf, pb) * LOG2E
    mloct_s[...] = jnp.broadcast_to(mloc_t, e_t.shape)

    def phase_c(c, carry):
        g8 = pl.multiple_of(c * GROWS + 8, 8)
        cumc_s[c] = _pad_rows_t(jnp.concatenate([cumt_s[pl.ds(g8, 8), :], mloct_s[pl.ds(g8, 8), :]], axis=0))
        pmaxc_s[c] = _pad_rows_t(pmaxt_s[pl.ds(g8, 8), :])
        return carry

    lax.fori_loop(0, n_chunks, phase_c, 0)

    def scan_direction(sub, d):
        mask = (scol <= trow) if d == 0 else (scol >= trow)
        if zero_init:
            st_s[...] = jnp.zeros(st_s.shape, F32)
            m_s[...] = jnp.zeros(m_s.shape, F32)
        else:
            st_s[...] = s0_ref[sub, 0, d]
            m_s[...] = m0_ref[sub, 0]

        def body(ci, carry):
            c = sub * cps + (ci if d == 0 else cps - 1 - ci)
            r0 = pl.multiple_of(c * CHUNK, CHUNK)
            cumc = cumc_s[c]
            m_prev = m_s[...]
            m_prev2 = m_prev * LOG2E
            mx = jnp.maximum(m_prev2, pmaxc_s[c])
            negmt = -(cumc * LOG2E + mx)
            tot = cumc[CHUNK - 1:CHUNK, :] if d == 0 else cumc[0:1, :]
            mloc = pltpu.roll(cumc[0:1, :], LANES - 8, 1)
            m_new = jnp.maximum(tot + m_prev, mloc)
            a_old = jnp.exp(tot + m_prev - m_new)
            a_loc = jnp.exp(mloc - m_new)
            for pr in (0, 1):
                stp = st_s[pr]
                stp_bf = stp.astype(BF16)
                kt = kt_s[c * 2 + pr]
                for hh in (0, 1):
                    hl = pr * 2 + hh
                    j = d * 4 + hl
                    c0 = hl * LANES
                    e_row = et_s[pl.ds(c * GROWS + 8 + j, 1), :]
                    wend_row = wendt_s[pl.ds(c * GROWS + 8 + j, 1), :]
                    mxb = jnp.broadcast_to(mx[:, j:j + 1], (CHUNK, LANES))
                    sqk = (s_s[c * 4 + hl] * jnp.exp2(jnp.where(mask, e_row - mxb, -jnp.inf))).astype(BF16)
                    qw = qm_s[c * 4 + hl] * jnp.exp2(m_prev2[:, j:j + 1] - mxb).astype(BF16)
                    vaug = jnp.concatenate([v_ref[pl.ds(r0, CHUNK), c0:c0 + LANES], ones_blk], axis=1)
                    nd = _dot(jnp.concatenate([sqk, qw], axis=1), jnp.concatenate([vaug, stp_bf], axis=0))
                    emt = jnp.exp2(jnp.broadcast_to(negmt[:, j:j + 1], (CHUNK, LANES)))
                    den = jnp.maximum(jnp.abs(nd[:, LANES:2 * LANES]), emt)
                    hout = nd[:, 0:LANES] / den
                    if d == 0:
                        hacc_s[pl.ds(r0, CHUNK), c0:c0 + LANES] = hout
                    else:
                        hacc_s[pl.ds(r0, CHUNK), c0:c0 + LANES] += hout
                    ktw = (kt[hh * 64:(hh + 1) * 64, :] * (wend_row * a_loc[:, j:j + 1])).astype(BF16)
                    st_s[pr, hh * 64:(hh + 1) * 64, :] = (a_old[:, j:j + 1] * stp[hh * 64:(hh + 1) * 64, :]
                                                          + _dot(ktw, vaug))
            m_s[...] = m_new
            return carry

        lax.fori_loop(0, cps, body, 0, unroll=min(4, cps))
        if state_out:
            for pr in (0, 1):
                t = st_s[pr, :, 0:LANES].T
                t_hi = pltpu.roll(t, 64, 1)
                cfin_ref[sub, 0, d, pr * 2] = t[:, 0:64]
                cfin_ref[sub, 0, d, pr * 2 + 1] = t_hi[:, 0:64]
                nfin_ref[sub, 0, d, pr:pr + 1, :] = st_s[pr, :, LANES:2 * LANES].T[0:1, :]
            mfin_ref[sub, 0, d:d + 1, :] = m_s[...]

    if state_out == 'first':
        _zero_other_layers(cfin_ref, n_sub)
    if state_out:
        nfin_ref[...] = jnp.zeros(nfin_ref.shape, F32)

    def scan_sequence(sub, carry):
        scan_direction(sub, 0)
        scan_direction(sub, 1)
        return carry

    lax.fori_loop(0, n_sub, scan_sequence, 0)

    def phase_e(c, carry):
        r0 = pl.multiple_of(c * CHUNK, CHUNK)
        for hl in range(4):
            c0 = hl * LANES
            hm_ref[pl.ds(r0, CHUNK), c0:c0 + LANES] = _rms(
                hacc_s[pl.ds(r0, CHUNK), c0:c0 + LANES], nw_ref[0, :, c0:c0 + LANES]).astype(hm_ref.dtype)
        return carry

    lax.fori_loop(0, n_chunks, phase_e, 0)


def _mlstm(proj, gates, par, normw, s0, m0, *, row0, n_seq, seq_len, n_sub=1, state_layer=None,
           state_prev=None, depth=1):
    zero_init = s0 is None
    state_out = None if state_layer is None else ('first' if state_layer == 0 else 'next')
    blk_len = n_sub * seq_len
    rb = row0 // blk_len
    nc = blk_len // CHUNK
    in_specs = [
        pl.BlockSpec((blk_len, 256), lambda b, s: (rb + b, COL_Q // 256 + s)),
        pl.BlockSpec((blk_len, 256), lambda b, s: (rb + b, COL_K // 256 + s)),
        pl.BlockSpec((blk_len, 512), lambda b, s: (rb + b, COL_V // 512 + s)),
        pl.BlockSpec((blk_len, LANES), lambda b, s: (rb + b, s)),
        pl.BlockSpec((1, 8, LANES), lambda b, s: (s, 0, 0)),
        pl.BlockSpec((1, 1, 512), lambda b, s: (s, 0, 0)),
    ]
    args = [proj, proj, proj, gates, par, normw]
    if not zero_init:
        in_specs.append(pl.BlockSpec((n_sub, 1, 2, 2, 128, 256), lambda b, s: (b, s, 0, 0, 0, 0)))
        in_specs.append(pl.BlockSpec((n_sub, 1, 1, LANES), lambda b, s: (b, s, 0, 0)))
        args += [s0, m0]
    out_specs = [pl.BlockSpec((blk_len, 512), lambda b, s: (b, s))]
    out_shape = [jax.ShapeDtypeStruct((n_seq * seq_len, 1024), BF16)]
    aliases = {}
    if state_out:
        layers = depth if state_out == 'first' else 1
        layer0 = state_layer
        out_specs.append(pl.BlockSpec((n_sub, layers, 2, 4, ML_DV, ML_DK), lambda b, s: (b, layer0, 0, s, 0, 0)))
        out_shape.append(jax.ShapeDtypeStruct((n_seq, depth, 2, ML_HEADS, ML_DV, ML_DK), F32))
        out_specs.append(pl.BlockSpec((n_sub, 1, 2, 8, LANES), lambda b, s: (b, s, 0, 0, 0)))
        out_shape.append(jax.ShapeDtypeStruct((n_seq, 2, 2, 8, LANES), F32))
        out_specs.append(pl.BlockSpec((n_sub, 1, 2, LANES), lambda b, s: (b, s, 0, 0)))
        out_shape.append(jax.ShapeDtypeStruct((n_seq, 2, 2, LANES), F32))
        if state_out == 'next':
            aliases = {len(args): 1}
            in_specs.append(pl.BlockSpec(memory_space=pl.ANY))
            args.append(state_prev)
    gate_rows = pltpu.VMEM((nc * GROWS, LANES), F32)
    res = pl.pallas_call(
        functools.partial(_mlstm_kernel, blk_len=blk_len, seq_len=seq_len, zero_init=zero_init,
                          state_out=state_out),
        grid=(n_seq // n_sub, 2),
        in_specs=in_specs,
        out_specs=out_specs,
        out_shape=out_shape,
        input_output_aliases=aliases,
        scratch_shapes=[pltpu.VMEM((nc * 4, CHUNK, LANES), BF16),
                        pltpu.VMEM((nc * 4, CHUNK, CHUNK), F32),
                        pltpu.VMEM((nc * 2, LANES, CHUNK), F32),
                        pltpu.VMEM((blk_len, 512), F32),
                        gate_rows, gate_rows, gate_rows, gate_rows, gate_rows, gate_rows,
                        pltpu.VMEM((nc, CHUNK, LANES), F32),
                        pltpu.VMEM((nc, CHUNK, LANES), F32),
                        pltpu.VMEM((2, 128, 256), F32),
                        pltpu.VMEM((1, LANES), F32)],
        compiler_params=_cparams(("parallel", "arbitrary")),
        name="mlstm_scan",
    )(*args)
    return res if state_out else (res[0], None, None, None)


def _outproj_kernel(yc_ref, yl_ref, z_ref, hc_ref, hl_ref, o_ref, gs_ref, gm_ref, ha_ref, hb_ref, wos_ref,
                    wom_ref, wout_ref, sn_ref, nf_ref, g1_ref, sc2_ref, sh2_ref, hout_ref, u2_ref, *,
                    n_ctx_tiles):
    is_ctx = pl.program_id(0) < n_ctx_tiles
    y_in = jnp.where(is_ctx, yc_ref[...], yl_ref[...]).astype(F32)
    hm_in = jnp.where(is_ctx, hc_ref[...], hl_ref[...]).astype(F32)
    y = y_in * _silu(z_ref[...].astype(F32))
    y = _rms(y, sn_ref[...]).astype(BF16)
    y_ssd = _dot(y, wos_ref[...])
    hm = (hm_in * jax.nn.sigmoid(o_ref[...].astype(F32))).astype(BF16)
    y_ml = _dot(hm, wom_ref[...])
    mix = (jax.nn.sigmoid(gs_ref[...].astype(F32)) * y_ssd
           + jax.nn.sigmoid(gm_ref[...].astype(F32)) * y_ml).astype(BF16)
    h = jnp.where(is_ctx, ha_ref[...], hb_ref[...]) + g1_ref[0] * _dot(mix, wout_ref[...])
    hout_ref[...] = h
    u2_ref[...] = (_rms(h, nf_ref[...]) * (1.0 + sc2_ref[0]) + sh2_ref[0]).astype(u2_ref.dtype)


def _outproj(y_ctx, y_lat, hm_ctx, hm_lat, proj, rows, wos, wom, wout, ssd_norm, norm_ffn, mod48, lat_len,
             tm=512):
    t, d = proj.shape[0], wos.shape[0]
    n_ctx_rows = rows.n_ctx_rows
    nct = n_ctx_rows // tm
    mi = functools.partial(_mod_index, tm=tm, n_ctx_rows=n_ctx_rows, lat_len=lat_len)
    row = lambda i: (i, 0)
    ctx_row = lambda i: (jnp.minimum(i, nct - 1), 0)
    lat_row = lambda i: (jnp.maximum(i - nct, 0), 0)
    const = lambda i: (0, 0)
    col = lambda k: (lambda i: (i, k))
    modspec = lambda which: pl.BlockSpec((1, 1, d), lambda i: (mi(i, which=which), 0, 0))
    return pl.pallas_call(
        functools.partial(_outproj_kernel, n_ctx_tiles=nct),
        grid=(t // tm,),
        in_specs=[pl.BlockSpec((tm, d), ctx_row), pl.BlockSpec((tm, d), lat_row),
                  pl.BlockSpec((tm, d), col(COL_Z // d)),
                  pl.BlockSpec((tm, d), ctx_row), pl.BlockSpec((tm, d), lat_row),
                  pl.BlockSpec((tm, d), col(COL_O // d)),
                  pl.BlockSpec((tm, d), col(COL_GS // d)), pl.BlockSpec((tm, d), col(COL_GM // d))]
        + rows.specs(tm, d) + [
                  pl.BlockSpec((d, d), const), pl.BlockSpec((d, d), const), pl.BlockSpec((d, d), const),
                  pl.BlockSpec((1, d), const), pl.BlockSpec((1, d), const),
                  modspec(2), modspec(4), modspec(3)],
        out_specs=[pl.BlockSpec((tm, d), row), pl.BlockSpec((tm, d), row)],
        out_shape=[jax.ShapeDtypeStruct((t, d), F32), jax.ShapeDtypeStruct((t, d), BF16)],
        compiler_params=_cparams(("parallel",)),
        name="merge_outproj",
    )(y_ctx, y_lat, proj, hm_ctx, hm_lat, proj, proj, proj, rows.ctx, rows.lat, wos, wom, wout,
      ssd_norm.reshape(1, d), norm_ffn.reshape(1, d), mod48, mod48, mod48)


FFN_TM = 512
FFN_TF = 256


def _ffn_kernel(u_ref, ut_ref, ub_ref, wup_ref, cw_ref, wd_ref, h_ref, g2_ref, *rest,
                n_ctx_tiles, ctx_len, tiles_per_seq, final):
    if final:
        fw_ref, yc_ref, yl_ref, act_s = rest
    else:
        o_ref, act_s = rest
    i = pl.program_id(0)
    tm, tf = FFN_TM, FFN_TF
    n_chunks = D_FF // tf
    ext = tm + 2 * GRID_W

    def cw(r, j):
        return cw_ref[r:r + 1, j * tf:(j + 1) * tf]

    @pl.when(i < n_ctx_tiles)
    def _():
        u = u_ref[...]
        pos = lax.broadcasted_iota(jnp.int32, (tm, tf), 0) & (ctx_len - 1)
        for j in range(n_chunks):
            g = _dot(u, wup_ref[:, j * tf:(j + 1) * tf])
            val = _dot(u, wup_ref[:, D_FF + j * tf:D_FF + (j + 1) * tf])
            gl = jnp.where(pos != 0, pltpu.roll(g, 1, 0), 0.0)
            gr = jnp.where(pos != ctx_len - 1, pltpu.roll(g, tm - 1, 0), 0.0)
            conv = gl * cw(3, j) + g * cw(4, j) + gr * cw(5, j) + cw(9, j)
            act_s[:, j * tf:(j + 1) * tf] = (_silu(conv) * val).astype(BF16)

    @pl.when(i >= n_ctx_tiles)
    def _():
        ti = (i - n_ctx_tiles) % tiles_per_seq
        u = u_ref[...]
        top = jnp.where(ti > 0, ut_ref[...], jnp.zeros_like(ut_ref[...]))
        bot = jnp.where(ti < tiles_per_seq - 1, ub_ref[...], jnp.zeros_like(ub_ref[...]))
        uext = jnp.concatenate([top, u, bot], axis=0)
        col = lax.broadcasted_iota(jnp.int32, (ext, tf), 0) & (GRID_W - 1)
        for j in range(n_chunks):
            gx = _dot(uext, wup_ref[:, j * tf:(j + 1) * tf])
            val = _dot(u, wup_ref[:, D_FF + j * tf:D_FF + (j + 1) * tf])
            gl = jnp.where(col != 0, pltpu.roll(gx, 1, 0), 0.0)
            gr = jnp.where(col != GRID_W - 1, pltpu.roll(gx, ext - 1, 0), 0.0)

            def taps(r, lo):
                return (gl[lo:lo + tm] * cw(3 * r, j) + gx[lo:lo + tm] * cw(3 * r + 1, j)
                        + gr[lo:lo + tm] * cw(3 * r + 2, j))

            conv = taps(1, GRID_W) + taps(0, 0) + taps(2, 2 * GRID_W) + cw(9, j)
            act_s[:, j * tf:(j + 1) * tf] = (_silu(conv) * val).astype(BF16)

    h_new = h_ref[...] + g2_ref[0] * _dot(act_s[...], wd_ref[...])
    if final:
        y = _rms(h_new, fw_ref[...])

        @pl.when(i < n_ctx_tiles)
        def _():
            yc_ref[...] = y

        @pl.when(i >= n_ctx_tiles)
        def _():
            yl_ref[...] = y
    else:
        o_ref[...] = h_new


def _ffn(u2, wup, cw16, wd, h, mod48, n_ctx_rows, ctx_len, lat_len, final_w=None):
    t, d = h.shape
    tm = FFN_TM
    nct = n_ctx_rows // tm
    hb = tm // GRID_W
    n_hblocks = t // GRID_W
    final = final_w is not None
    mi = functools.partial(_mod_index, tm=tm, n_ctx_rows=n_ctx_rows, lat_len=lat_len)
    resident = lambda shape: pl.BlockSpec(shape, lambda i: (0, 0), pipeline_mode=pl.Buffered(1))
    in_specs = [pl.BlockSpec((tm, d), lambda i: (i, 0)),
                pl.BlockSpec((GRID_W, d), lambda i: (jnp.maximum(i * hb - 1, 0), 0)),
                pl.BlockSpec((GRID_W, d), lambda i: (jnp.minimum((i + 1) * hb, n_hblocks - 1), 0)),
                resident((d, 2 * D_FF)),
                resident((16, D_FF)),
                resident((D_FF, d)),
                pl.BlockSpec((tm, d), lambda i: (i, 0)),
                pl.BlockSpec((1, 1, d), lambda i: (mi(i, which=5), 0, 0))]
    args = [u2, u2, u2, wup, cw16, wd, h, mod48]
    if final:
        in_specs.append(pl.BlockSpec((1, d), lambda i: (0, 0)))
        args.append(final_w.reshape(1, d))
        out_specs = [pl.BlockSpec((tm, d), lambda i: (jnp.minimum(i, nct - 1), 0)),
                     pl.BlockSpec((tm, d), lambda i: (jnp.maximum(i - nct, 0), 0))]
        out_shape = [jax.ShapeDtypeStruct((n_ctx_rows, d), F32), jax.ShapeDtypeStruct((t - n_ctx_rows, d), F32)]
    else:
        out_specs = pl.BlockSpec((tm, d), lambda i: (i, 0))
        out_shape = jax.ShapeDtypeStruct((t, d), F32)
    return pl.pallas_call(
        functools.partial(_ffn_kernel, n_ctx_tiles=nct, ctx_len=ctx_len, tiles_per_seq=lat_len // tm,
                          final=final),
        grid=(t // tm,),
        in_specs=in_specs,
        out_specs=out_specs,
        out_shape=out_shape,
        scratch_shapes=[pltpu.VMEM((tm, D_FF), BF16)],
        compiler_params=_cparams(("arbitrary",) if final else ("parallel",)),
        name="convffn",
    )(*args)


def _arrange_w_in(w):
    z, xbc, dt, q, k, v, o, ig, fg, gates = (
        w[:, 0:1024], w[:, 1024:2560], w[:, 2560:2592], w[:, 2592:3104], w[:, 3104:3616],
        w[:, 3616:4640], w[:, 4640:5664], w[:, 5664:5680], w[:, 5680:5696], w[:, 5696:7744])
    big = jnp.concatenate([z, o, gates, xbc, q, k, v], axis=1).astype(BF16)
    blocks = []
    for s in (0, 1):
        blocks += [dt[:, s * 8:s * 8 + 8], dt[:, 16 + s * 8:16 + s * 8 + 8],
                   ig[:, s * 4:s * 4 + 4], ig[:, 8 + s * 4:8 + s * 4 + 4],
                   fg[:, s * 4:s * 4 + 4], fg[:, 8 + s * 4:8 + s * 4 + 4],
                   jnp.zeros((w.shape[0], LANES - 32), w.dtype)]
    small = jnp.concatenate(blocks, axis=1).astype(BF16)
    return big, small


def _slab_rows(rows, n_layers, n_rows=8):
    ns = 2 * n_layers
    out = []
    for pieces in rows:
        used = sum(p.shape[1] for p in pieces)
        out.append(jnp.concatenate(list(pieces) + [jnp.zeros((ns, LANES - used), F32)], axis=1)[:, None, :])
    out.append(jnp.zeros((ns, n_rows - len(rows), LANES), F32))
    return jnp.concatenate(out, axis=1).reshape(n_layers, 2, n_rows, LANES)


def _ssd_params(dt_bias, a_log, d_skip, conv_w, conv_b):
    nl = dt_bias.shape[0]
    slab = lambda x: x.reshape(2 * nl, 8)
    par = _slab_rows([[slab(dt_bias[:, 0]), slab(dt_bias[:, 1])], [slab(a_log[:, 0]), slab(a_log[:, 1])]], nl)
    dvec = jnp.repeat(d_skip, SSD_HEADDIM, axis=-1).reshape(nl, 2, 1, 512)
    cw = jnp.concatenate([conv_w, conv_b[:, None], jnp.zeros((nl, 4, conv_w.shape[2]), F32)], axis=1)
    cwx = cw[:, :, 0:1024].reshape(nl, 8, 2, 512).transpose(0, 2, 1, 3)
    cwb = cw[:, :, 1024:1280].reshape(nl, 8, 2, LANES).transpose(0, 2, 1, 3)
    cwc = cw[:, :, 1280:1536].reshape(nl, 8, 2, LANES).transpose(0, 2, 1, 3)
    return par, dvec, cwx, cwb, cwc


def _ml_params(i_bias, f_bias, ml_norm):
    nl = i_bias.shape[0]
    slab = lambda x: x.reshape(2 * nl, 4)
    par = _slab_rows([[jnp.zeros((2 * nl, LANE_I), F32), slab(i_bias[:, 0]), slab(i_bias[:, 1]),
                       slab(f_bias[:, 0]), slab(f_bias[:, 1])]], nl)
    return par, ml_norm.reshape(nl, 2, 1, 512)


def _ssd_state_to_kernel(s):
    b, nl = s.shape[0], s.shape[1]
    s = s.reshape(b, nl, 2, 2, 2, 4, 64, 64).transpose(1, 0, 3, 2, 4, 7, 5, 6)
    return s.reshape(nl, b, 2, 2, 128, 256)


def _ml_state_to_kernel(c0, n0, m0):
    b, nl = c0.shape[0], c0.shape[1]
    aug = jnp.concatenate([jnp.swapaxes(c0, -1, -2),
                           jnp.broadcast_to(n0[..., None], n0.shape + (LANES,))], axis=-1)
    aug = aug.reshape(b, nl, 2, 2, 2, 2, 64, 256).transpose(1, 0, 3, 2, 4, 5, 6, 7).reshape(nl, b, 2, 2, 2, 128, 256)
    m = m0.reshape(b, nl, 2, 2, 4).transpose(1, 0, 3, 2, 4).reshape(nl, b, 2, 1, 8)
    mk = jnp.concatenate([m, jnp.zeros((nl, b, 2, 1, LANES - 8), F32)], axis=-1)
    return aug, mk


def _ml_nm_from_kernel(nfin, mfin):
    nl, b = nfin.shape[0], nfin.shape[1]
    n = nfin[:, :, :, :, 0:2, :].reshape(nl, b, 2, 2, 2, 2, ML_DK).transpose(1, 0, 3, 2, 4, 5, 6)
    m = jnp.stack([mfin[:, :, :, 0, 0:4], mfin[:, :, :, 1, 4:8]], axis=2).transpose(1, 0, 2, 3, 4)
    return n.reshape(b, nl, 2, ML_HEADS, ML_DK), m.reshape(b, nl, 2, ML_HEADS)


def kernel(x_prompt, x_sample, state_ssd, state_mlstm_C, state_mlstm_n, state_mlstm_m, c, c_ctx, w_ada, b_ada,
           norm_mix, w_in, ssd_conv_w, ssd_conv_b, ssd_dt_bias, ssd_a_log, ssd_d, ssd_norm, w_o_ssd, ml_i_bias,
           ml_f_bias, ml_norm, w_o_ml, w_out, norm_ffn, w_up, ffn_conv_w, ffn_conv_b, w_down, final_norm):
    nb, ctx_len, d = x_prompt.shape
    nlat, lat_len, _ = x_sample.shape
    depth = w_in.shape[0]
    n_ctx_rows = nb * ctx_len
    n_lat_rows = nlat * lat_len

    n_rows = n_ctx_rows + n_lat_rows
    rows = _Rows(x_prompt.reshape(n_ctx_rows, d), x_sample.reshape(n_lat_rows, d), n_ctx_rows)
    cond8 = jnp.concatenate([c_ctx[None], c, jnp.zeros((8 - 1 - nlat, d), F32)], axis=0)
    mod = _ada(cond8, w_ada, b_ada)

    mod48 = mod.reshape(depth, 48, 1, d)
    w_big, w_small = zip(*[_arrange_w_in(w_in[l]) for l in range(depth)])
    par_s, dvec, cwx, cwb, cwc = _ssd_params(ssd_dt_bias, ssd_a_log, ssd_d, ssd_conv_w, ssd_conv_b)
    par_m, normw = _ml_params(ml_i_bias, ml_f_bias, ml_norm)
    s0s = _ssd_state_to_kernel(state_ssd)
    s0m, m0m = _ml_state_to_kernel(state_mlstm_C, state_mlstm_n, state_mlstm_m)
    cw16 = jnp.concatenate([ffn_conv_w.reshape(depth, 9, D_FF), ffn_conv_b[:, None],
                            jnp.zeros((depth, 6, D_FF), F32)], axis=1)
    wos, wom, wout = w_o_ssd.astype(BF16), w_o_ml.astype(BF16), w_out.astype(BF16)
    wup, wdn = w_up.astype(BF16), w_down.astype(BF16)

    new_ssd, new_c, n_l, m_l = None, None, [], []
    for l in range(depth):
        proj, gates = _in_proj(rows, norm_mix[l], mod48[l], w_big[l], w_small[l], n_rows, lat_len)

        ssd_par = (par_s[l], dvec[l], cwx[l], cwb[l], cwc[l])
        y_ctx, new_ssd = _ssd(proj, gates, *ssd_par, None, row0=0, n_seq=nb, seq_len=ctx_len,
                              n_sub=CTX_SEQS_PER_STEP, state_layer=l, state_prev=new_ssd, depth=depth)
        y_lat, _ = _ssd(proj, gates, *ssd_par, s0s[l], row0=n_ctx_rows, n_seq=nlat, seq_len=lat_len)
        hm_ctx, new_c, n_ctx, m_ctx = _mlstm(proj, gates, par_m[l], normw[l], None, None, row0=0, n_seq=nb,
                                             seq_len=ctx_len, n_sub=CTX_SEQS_PER_STEP, state_layer=l,
                                             state_prev=new_c, depth=depth)
        hm_lat, _, _, _ = _mlstm(proj, gates, par_m[l], normw[l], s0m[l], m0m[l],
                                 row0=n_ctx_rows, n_seq=nlat, seq_len=lat_len)
        h, u2 = _outproj(y_ctx, y_lat, hm_ctx, hm_lat, proj, rows, wos[l], wom[l], wout[l], ssd_norm[l],
                         norm_ffn[l], mod48[l], lat_len)
        n_l.append(n_ctx)
        m_l.append(m_ctx)
        if l < depth - 1:
            h = _ffn(u2, wup[l], cw16[l], wdn[l], h, mod48[l], n_ctx_rows, ctx_len, lat_len)
            rows = _Rows(h, h, n_ctx_rows)
        else:
            y_ctx_rows, y_lat_rows = _ffn(u2, wup[l], cw16[l], wdn[l], h, mod48[l], n_ctx_rows, ctx_len, lat_len,
                                          final_w=final_norm)

    n_new, m_new = _ml_nm_from_kernel(jnp.stack(n_l), jnp.stack(m_l))
    return (y_ctx_rows.reshape(nb, ctx_len, d), y_lat_rows.reshape(nlat, lat_len, d), new_ssd, new_c, n_new, m_new)
```

```python
import functools

import jax
import jax.numpy as jnp
from jax import lax
from jax.experimental import pallas as pl
from jax.experimental.pallas import tpu as pltpu

F32 = jnp.float32
BF16 = jnp.bfloat16

CHUNK = 128
EPS = 1e-6
GRID_W = 64
SSD_HEADS = 16
SSD_HEADDIM = 64
SSD_STATE = 64
ML_HEADS = 8
ML_DV = 128
ML_DK = 64
D_FF = 2816

LANES = 128
BF16_SUBLANES = 16
VMEM_LIMIT = 56 * 1024 * 1024

COL_Z, COL_O, COL_GS, COL_GM = 0, 1024, 2048, 3072
COL_X, COL_B, COL_C = 4096, 5120, 5376
COL_Q, COL_K, COL_V = 5632, 6144, 6656
PROJ_COLS = 7680
GATE_COLS = 256
LANE_I = 16
LANE_F = 24
GROWS = 16
LOG2E = 1.4426950408889634
CTX_SEQS_PER_STEP = 8


def _cparams(sem):
    return pltpu.CompilerParams(dimension_semantics=sem, vmem_limit_bytes=VMEM_LIMIT)


def _silu(x):
    return x * jax.nn.sigmoid(x)


def _softplus(x):
    return jnp.maximum(x, 0.0) + jnp.log1p(jnp.exp(-jnp.abs(x)))


def _dot(a, b):
    return jnp.dot(a, b, preferred_element_type=F32)


def _dot_nt(a, b):
    return lax.dot_general(a, b, (((1,), (1,)), ((), ())), preferred_element_type=F32)


def _tri2():
    s = lax.broadcasted_iota(jnp.int32, (CHUNK, 2 * CHUNK), 0)
    t = lax.broadcasted_iota(jnp.int32, (CHUNK, 2 * CHUNK), 1)
    keep = ((t < CHUNK) & (s <= t)) | ((t >= CHUNK) & (s >= t - CHUNK))
    return jnp.where(keep, 1.0, 0.0).astype(BF16)


def _lane_cumsums(x, tri2):
    hi = x.astype(BF16)
    r1 = x - hi.astype(F32)
    mid = r1.astype(BF16)
    lo = (r1 - mid.astype(F32)).astype(BF16)
    cs = _dot(hi, tri2) + _dot(mid, tri2) + _dot(lo, tri2)
    return cs[:, 0:CHUNK], cs[:, CHUNK:2 * CHUNK]


def _pad_rows_t(x):
    return jnp.concatenate([x, jnp.zeros((CHUNK - x.shape[0], LANES), F32)], axis=0).T


def _ada_kernel(c_ref, w_ref, b_ref, o_ref):
    cond = _silu(c_ref[...]).astype(BF16)
    o_ref[0] = _dot(cond, w_ref[0].astype(BF16)) + b_ref[0]


def _ada(cond8, w_ada, b_ada):
    depth, d, n = w_ada.shape
    tn = 1536
    return pl.pallas_call(
        _ada_kernel,
        grid=(depth, n // tn),
        in_specs=[pl.BlockSpec((8, d), lambda l, j: (0, 0)),
                  pl.BlockSpec((1, d, tn), lambda l, j: (l, 0, j)),
                  pl.BlockSpec((1, 1, tn), lambda l, j: (l, 0, j))],
        out_specs=pl.BlockSpec((1, 8, tn), lambda l, j: (l, 0, j)),
        out_shape=jax.ShapeDtypeStruct((depth, 8, n), F32),
        compiler_params=_cparams(("arbitrary", "arbitrary")),
        name="ada_mod",
    )(cond8, w_ada, b_ada.reshape(depth, 1, n))


def _mod_index(i, tm, n_ctx_rows, lat_len, which):
    n_ctx_tiles = n_ctx_rows // tm
    tiles_per_seq = lat_len // tm
    row = jnp.where(i < n_ctx_tiles, 0, 1 + (i - n_ctx_tiles) // tiles_per_seq)
    return row * 6 + which


def _rms(x, w):
    return (x * lax.rsqrt(jnp.mean(x * x, axis=-1, keepdims=True) + EPS)) * w


class _Rows:
    def __init__(self, ctx, lat, n_ctx_rows):
        self.ctx, self.lat, self.n_ctx_rows = ctx, lat, n_ctx_rows
        self.lat_row0 = n_ctx_rows if lat is ctx else 0

    def specs(self, tm, d):
        nct = self.n_ctx_rows // tm
        off = self.lat_row0 // tm
        return [pl.BlockSpec((tm, d), lambda i, *_: (jnp.minimum(i, nct - 1), 0)),
                pl.BlockSpec((tm, d), lambda i, *_: (jnp.maximum(i - nct, 0) + off, 0))]


IN_TM = 1024
IN_TN = 2560


def _in_proj_kernel(ha_ref, hb_ref, nw_ref, sc_ref, sh_ref, w_ref, wg_ref, proj_ref, gate_ref, u_s, *,
                    n_ctx_tiles, n_col_tiles):
    i = pl.program_id(0)
    j = pl.program_id(1)

    @pl.when(j == 0)
    def _():
        x = jnp.where(i < n_ctx_tiles, ha_ref[...], hb_ref[...])
        u_s[...] = (_rms(x, nw_ref[...]) * (1.0 + sc_ref[0]) + sh_ref[0]).astype(u_s.dtype)

    @pl.when(j < n_col_tiles)
    def _():
        proj_ref[...] = _dot(u_s[...], w_ref[...]).astype(proj_ref.dtype)

    @pl.when(j == n_col_tiles)
    def _():
        gate_ref[...] = _dot(u_s[...], wg_ref[...])


def _in_proj(rows, w, mod48, w_big, w_small, n_rows, lat_len):
    d = w.shape[0]
    tm, tn = IN_TM, IN_TN
    nj = PROJ_COLS // tn
    mi = functools.partial(_mod_index, tm=tm, n_ctx_rows=rows.n_ctx_rows, lat_len=lat_len)
    return pl.pallas_call(
        functools.partial(_in_proj_kernel, n_ctx_tiles=rows.n_ctx_rows // tm, n_col_tiles=nj),
        grid=(n_rows // tm, nj + 1),
        in_specs=rows.specs(tm, d) + [
            pl.BlockSpec((1, d), lambda i, j: (0, 0)),
            pl.BlockSpec((1, 1, d), lambda i, j: (mi(i, which=1), 0, 0)),
            pl.BlockSpec((1, 1, d), lambda i, j: (mi(i, which=0), 0, 0)),
            pl.BlockSpec((d, tn), lambda i, j: (0, jnp.minimum(j, nj - 1))),
            pl.BlockSpec((d, GATE_COLS), lambda i, j: (0, 0))],
        out_specs=[pl.BlockSpec((tm, tn), lambda i, j: (i, jnp.minimum(j, nj - 1))),
                   pl.BlockSpec((tm, GATE_COLS), lambda i, j: (i, 0))],
        out_shape=[jax.ShapeDtypeStruct((n_rows, PROJ_COLS), BF16),
                   jax.ShapeDtypeStruct((n_rows, GATE_COLS), F32)],
        scratch_shapes=[pltpu.VMEM((tm, d), BF16)],
        compiler_params=_cparams(("parallel", "arbitrary")),
        name="in_proj",
    )(rows.ctx, rows.lat, w.reshape(1, d), mod48, mod48, w_big, w_small)


def _conv_silu_chunk(src_ref, cw_ref, c, cps, blk_len):
    r0 = pl.multiple_of(c * CHUNK, CHUNK)
    cur = src_ref[pl.ds(r0, CHUNK), :].astype(F32)
    p0 = pl.multiple_of(jnp.maximum(r0 - BF16_SUBLANES, 0), BF16_SUBLANES)
    n0 = pl.multiple_of(jnp.minimum(r0 + CHUNK, blk_len - BF16_SUBLANES), BF16_SUBLANES)
    prv = src_ref[pl.ds(p0, BF16_SUBLANES), :].astype(F32)[BF16_SUBLANES - 1:BF16_SUBLANES, :]
    nxt = src_ref[pl.ds(n0, BF16_SUBLANES), :].astype(F32)[0:1, :]
    cs = c % cps
    prv = jnp.where(cs > 0, prv, 0.0)
    nxt = jnp.where(cs < cps - 1, nxt, 0.0)
    row = lax.broadcasted_iota(jnp.int32, cur.shape, 0)
    up = jnp.where(row == 0, prv, pltpu.roll(cur, 1, 0))
    dn = jnp.where(row == CHUNK - 1, nxt, pltpu.roll(cur, CHUNK - 1, 0))
    y = up * cw_ref[0, 0:1, :] + cur * cw_ref[0, 1:2, :] + dn * cw_ref[0, 2:3, :] + cw_ref[0, 3:4, :]
    return _silu(y)


def _zero_other_layers(ref, n_sub):
    z = jnp.zeros(ref.shape[-2:], F32)
    for sub in range(n_sub):
        for layer in range(1, ref.shape[1]):
            for d in range(ref.shape[2]):
                for h in range(ref.shape[3]):
                    ref[sub, layer, d, h] = z


def _ssd_kernel(*refs, blk_len, seq_len, zero_init, state_out):
    it = iter(refs)
    x_ref, b_ref, c_ref, g_ref = next(it), next(it), next(it), next(it)
    par_ref, d_ref, cwx_ref, cwb_ref, cwc_ref = next(it), next(it), next(it), next(it), next(it)
    s0_ref = None if zero_init else next(it)
    if state_out == 'next':
        next(it)
    y_ref = next(it)
    sfin_ref = next(it) if state_out else None
    (xbd_s, cm_s, g_s, bt_s, yacc_s, cumc_s, at_s, dtt_s, cumt_s, dtet_s, cdt_s, st_s) = (
        next(it) for _ in range(12))

    n_chunks = blk_len // CHUNK
    cps = seq_len // CHUNK
    n_sub = blk_len // seq_len
    lane = lax.broadcasted_iota(jnp.int32, (CHUNK, LANES), 1)
    lane_lo = lane < 64
    lane_row_lo = lax.broadcasted_iota(jnp.int32, (1, LANES), 1) < 64
    trow = lax.broadcasted_iota(jnp.int32, (CHUNK, CHUNK), 0)
    scol = lax.broadcasted_iota(jnp.int32, (CHUNK, CHUNK), 1)
    bias_row = par_ref[0, 0:1, :]
    a_row = -jnp.exp(par_ref[0, 1:2, :])

    def phase_a(c, carry):
        r0 = pl.multiple_of(c * CHUNK, CHUNK)
        xck = _conv_silu_chunk(x_ref, cwx_ref, c, cps, blk_len)
        bck = _conv_silu_chunk(b_ref, cwb_ref, c, cps, blk_len)
        cck = _conv_silu_chunk(c_ref, cwc_ref, c, cps, blk_len)
        yacc_s[pl.ds(r0, CHUNK), :] = d_ref[0] * xck
        for p in range(4):
            xp = xck[:, p * LANES:(p + 1) * LANES]
            xbd_s[c * 4 + p] = jnp.concatenate(
                [jnp.where(lane_lo, xp, 0.0), jnp.where(lane_lo, 0.0, xp)], axis=0).astype(BF16)
        bbf = bck.astype(BF16)
        for g2 in (0, 1):
            in_g = lane_lo if g2 == 0 else jnp.logical_not(lane_lo)
            cm = jnp.where(in_g, cck, 0.0).astype(BF16)
            cm_s[c * 2 + g2] = cm
            g_s[c * 2 + g2] = _dot_nt(cm, bbf)
        bt_s[c] = bck.T
        dt = _softplus(g_ref[pl.ds(r0, CHUNK), :] + bias_row)
        g0 = pl.multiple_of(c * GROWS, GROWS)
        at_s[pl.ds(g0, GROWS), :] = (dt * a_row).T[0:GROWS, :]
        dtt_s[pl.ds(g0, GROWS), :] = dt.T[0:GROWS, :]
        return carry

    lax.fori_loop(0, n_chunks, phase_a, 0)

    cf, cr = _lane_cumsums(at_s[...], _tri2())
    fwd_row = (lax.broadcasted_iota(jnp.int32, (n_chunks * GROWS, 1), 0) & (GROWS - 1)) < 8
    cum_t = jnp.where(fwd_row, cf, cr)
    tot_t = jnp.where(fwd_row, cum_t[:, CHUNK - 1:CHUNK], cum_t[:, 0:1])
    cumt_s[...] = cum_t * LOG2E
    dtet_s[...] = dtt_s[...] * jnp.exp(tot_t - cum_t)
    cdt_s[...] = jnp.broadcast_to(jnp.exp(tot_t), cum_t.shape)

    def phase_c(c, carry):
        g0 = pl.multiple_of(c * GROWS, GROWS)
        cumc_s[c] = _pad_rows_t(cumt_s[pl.ds(g0, GROWS), :])
        return carry

    lax.fori_loop(0, n_chunks, phase_c, 0)

    def scan_direction(sub, d):
        mask = (scol <= trow) if d == 0 else (scol >= trow)
        if zero_init:
            st_s[...] = jnp.zeros(st_s.shape, F32)
        else:
            st_s[...] = s0_ref[sub, 0, d]

        def body(ci, carry):
            c = sub * cps + (ci if d == 0 else cps - 1 - ci)
            r0 = pl.multiple_of(c * CHUNK, CHUNK)
            cumc = cumc_s[c]
            st = st_s[...]
            st_bf = st.astype(BF16)

            def grow(ref, j):
                return ref[pl.ds(c * GROWS + j, 1), :]

            for g2 in (0, 1):
                yoff = _dot(cm_s[c * 2 + g2], st_bf)
                gmat = g_s[c * 2 + g2]
                bt_g = bt_s[c, g2 * 64:(g2 + 1) * 64, :]
                for pr in (0, 1):
                    p = g2 * 2 + pr
                    j0 = d * 8 + g2 * 4 + pr * 2
                    j1 = j0 + 1
                    cb0 = jnp.broadcast_to(cumc[:, j0:j0 + 1], (CHUNK, LANES))
                    cb1 = jnp.broadcast_to(cumc[:, j1:j1 + 1], (CHUNK, LANES))
                    m0 = gmat * jnp.exp2(jnp.where(mask, cb0 - grow(cumt_s, j0), -jnp.inf))
                    m1 = gmat * jnp.exp2(jnp.where(mask, cb1 - grow(cumt_s, j1), -jnp.inf))
                    mcat = jnp.concatenate([(m0 * grow(dtt_s, j0)).astype(BF16),
                                            (m1 * grow(dtt_s, j1)).astype(BF16)], axis=1)
                    xbd = xbd_s[c * 4 + p]
                    yp = _dot(mcat, xbd) + (jnp.exp2(jnp.where(lane_lo, cb0, cb1))
                                            * yoff[:, pr * LANES:(pr + 1) * LANES])
                    c0 = p * LANES
                    if d == 0:
                        yacc_s[pl.ds(r0, CHUNK), c0:c0 + LANES] += yp
                    else:
                        y_ref[pl.ds(r0, CHUNK), c0:c0 + LANES] = (
                            yacc_s[pl.ds(r0, CHUNK), c0:c0 + LANES] + yp).astype(y_ref.dtype)
                    lhs = jnp.concatenate([(bt_g * grow(dtet_s, j0)).astype(BF16),
                                           (bt_g * grow(dtet_s, j1)).astype(BF16)], axis=1)
                    cd = jnp.where(lane_row_lo, grow(cdt_s, j0), grow(cdt_s, j1))
                    st_s[g2 * 64:(g2 + 1) * 64, pr * LANES:(pr + 1) * LANES] = (
                        cd * st[g2 * 64:(g2 + 1) * 64, pr * LANES:(pr + 1) * LANES] + _dot(lhs, xbd))
            return carry

        lax.fori_loop(0, cps, body, 0, unroll=min(4, cps))
        if state_out:
            for half in (0, 1):
                t = st_s[:, half * LANES:(half + 1) * LANES].T
                t_hi = pltpu.roll(t, 64, 1)
                for hloc in (0, 1):
                    hl = half * 2 + hloc
                    sfin_ref[sub, 0, d, hl] = t[hloc * 64:(hloc + 1) * 64, 0:64]
                    sfin_ref[sub, 0, d, 4 + hl] = t_hi[hloc * 64:(hloc + 1) * 64, 0:64]

    if state_out == 'first':
        _zero_other_layers(sfin_ref, n_sub)

    def scan_sequence(sub, carry):
        scan_direction(sub, 0)
        scan_direction(sub, 1)
        return carry

    lax.fori_loop(0, n_sub, scan_sequence, 0)


def _ssd(proj, gates, par, dvec, cwx, cwb, cwc, s0, *, row0, n_seq, seq_len, n_sub=1, state_layer=None,
         state_prev=None, depth=1):
    zero_init = s0 is None
    state_out = None if state_layer is None else ('first' if state_layer == 0 else 'next')
    blk_len = n_sub * seq_len
    rb = row0 // blk_len
    nc = blk_len // CHUNK
    in_specs = [
        pl.BlockSpec((blk_len, 512), lambda b, s: (rb + b, COL_X // 512 + s)),
        pl.BlockSpec((blk_len, LANES), lambda b, s: (rb + b, COL_B // LANES + s)),
        pl.BlockSpec((blk_len, LANES), lambda b, s: (rb + b, COL_C // LANES + s)),
        pl.BlockSpec((blk_len, LANES), lambda b, s: (rb + b, s)),
        pl.BlockSpec((1, 8, LANES), lambda b, s: (s, 0, 0)),
        pl.BlockSpec((1, 1, 512), lambda b, s: (s, 0, 0)),
        pl.BlockSpec((1, 8, 512), lambda b, s: (s, 0, 0)),
        pl.BlockSpec((1, 8, LANES), lambda b, s: (s, 0, 0)),
        pl.BlockSpec((1, 8, LANES), lambda b, s: (s, 0, 0)),
    ]
    args = [proj, proj, proj, gates, par, dvec, cwx, cwb, cwc]
    if not zero_init:
        in_specs.append(pl.BlockSpec((n_sub, 1, 2, 128, 256), lambda b, s: (b, s, 0, 0, 0)))
        args.append(s0)
    out_specs = [pl.BlockSpec((blk_len, 512), lambda b, s: (b, s))]
    out_shape = [jax.ShapeDtypeStruct((n_seq * seq_len, 1024), BF16)]
    aliases = {}
    if state_out:
        layers = depth if state_out == 'first' else 1
        layer0 = state_layer
        out_specs.append(pl.BlockSpec((n_sub, layers, 2, 8, SSD_HEADDIM, SSD_STATE),
                                      lambda b, s: (b, layer0, 0, s, 0, 0)))
        out_shape.append(jax.ShapeDtypeStruct((n_seq, depth, 2, SSD_HEADS, SSD_HEADDIM, SSD_STATE), F32))
        if state_out == 'next':
            aliases = {len(args): 1}
            in_specs.append(pl.BlockSpec(memory_space=pl.ANY))
            args.append(state_prev)
    gate_rows = pltpu.VMEM((nc * GROWS, LANES), F32)
    res = pl.pallas_call(
        functools.partial(_ssd_kernel, blk_len=blk_len, seq_len=seq_len, zero_init=zero_init,
                          state_out=state_out),
        grid=(n_seq // n_sub, 2),
        in_specs=in_specs,
        out_specs=out_specs,
        out_shape=out_shape,
        input_output_aliases=aliases,
        scratch_shapes=[pltpu.VMEM((nc * 4, 2 * CHUNK, LANES), BF16),
                        pltpu.VMEM((nc * 2, CHUNK, LANES), BF16),
                        pltpu.VMEM((nc * 2, CHUNK, CHUNK), F32),
                        pltpu.VMEM((nc, LANES, CHUNK), F32),
                        pltpu.VMEM((blk_len, 512), F32),
                        pltpu.VMEM((nc, CHUNK, LANES), F32),
                        gate_rows, gate_rows, gate_rows, gate_rows, gate_rows,
                        pltpu.VMEM((128, 256), F32)],
        compiler_params=_cparams(("parallel", "arbitrary")),
        name="ssd_scan",
    )(*args)
    return res if state_out else (res[0], None)


def _mlstm_kernel(*refs, blk_len, seq_len, zero_init, state_out):
    it = iter(refs)
    q_ref, k_ref, v_ref, g_ref, par_ref, nw_ref = (next(it) for _ in range(6))
    s0_ref, m0_ref = (None, None) if zero_init else (next(it), next(it))
    if state_out == 'next':
        next(it)
    hm_ref = next(it)
    cfin_ref, nfin_ref, mfin_ref = (next(it), next(it), next(it)) if state_out else (None, None, None)
    (qm_s, s_s, kt_s, hacc_s, gt_s, cumt_s, et_s, wendt_s, pmaxt_s, mloct_s, cumc_s, pmaxc_s, st_s, m_s) = (
        next(it) for _ in range(14))

    n_chunks = blk_len // CHUNK
    cps = seq_len // CHUNK
    n_sub = blk_len // seq_len
    n_rows = n_chunks * GROWS
    lane = lax.broadcasted_iota(jnp.int32, (CHUNK, LANES), 1)
    lane_lo = lane < 64
    trow = lax.broadcasted_iota(jnp.int32, (CHUNK, CHUNK), 0)
    scol = lax.broadcasted_iota(jnp.int32, (CHUNK, CHUNK), 1)
    ones_blk = jnp.ones((CHUNK, LANES), BF16)
    bias_row = par_ref[0, 0:1, :]
    qscale = ML_DK ** -0.5

    def phase_a(c, carry):
        r0 = pl.multiple_of(c * CHUNK, CHUNK)
        for pr in (0, 1):
            qp = q_ref[pl.ds(r0, CHUNK), pr * LANES:(pr + 1) * LANES].astype(F32)
            kp = k_ref[pl.ds(r0, CHUNK), pr * LANES:(pr + 1) * LANES]
            kt_s[c * 2 + pr] = kp.astype(F32).T
            for hh in (0, 1):
                in_h = lane_lo if hh == 0 else jnp.logical_not(lane_lo)
                qm = jnp.where(in_h, qp * qscale, 0.0).astype(BF16)
                qm_s[c * 4 + pr * 2 + hh] = qm
                s_s[c * 4 + pr * 2 + hh] = _dot_nt(qm, kp)
        gk = g_ref[pl.ds(r0, CHUNK), :] + bias_row
        comb = jnp.where(lane < LANE_F, gk, -_softplus(-gk))
        g0 = pl.multiple_of(c * GROWS, GROWS)
        gt_s[pl.ds(g0, GROWS), :] = comb.T[LANE_I:LANE_I + GROWS, :]
        return carry

    lax.fori_loop(0, n_chunks, phase_a, 0)

    g_all = gt_s[...]
    cf, cr = _lane_cumsums(g_all, _tri2())
    rr = lax.broadcasted_iota(jnp.int32, (n_rows, 1), 0) & (GROWS - 1)
    fwd_row = rr < 12
    cum_t = jnp.where(fwd_row, cf, cr)
    tot_t = jnp.where(fwd_row, cum_t[:, CHUNK - 1:CHUNK], cum_t[:, 0:1])
    e_t = pltpu.roll(g_all, 8, 0) - cum_t
    mloc_t = jnp.max(e_t, axis=1, keepdims=True) + tot_t
    lane_b = lax.broadcasted_iota(jnp.int32, (n_rows, LANES), 1)
    pf, pb = e_t, e_t
    k = 1
    while k < CHUNK:
        pf = jnp.maximum(pf, jnp.where(lane_b >= k, pltpu.roll(pf, k, 1), -jnp.inf))
        pb = jnp.maximum(pb, jnp.where(lane_b < CHUNK - k, pltpu.roll(pb, CHUNK - k, 1), -jnp.inf))
        k *= 2
    cumt_s[...] = cum_t
    et_s[...] = e_t * LOG2E
    wendt_s[...] = jnp.exp(e_t + tot_t - mloc_t)
    pmaxt_s[...] = jnp.where(fwd_row, pf, pb) * LOG2E
    mloct_s[...] = jnp.broadcast_to(mloc_t, e_t.shape)

    def phase_c(c, carry):
        g8 = pl.multiple_of(c * GROWS + 8, 8)
        cumc_s[c] = _pad_rows_t(jnp.concatenate([cumt_s[pl.ds(g8, 8), :], mloct_s[pl.ds(g8, 8), :]], axis=0))
        pmaxc_s[c] = _pad_rows_t(pmaxt_s[pl.ds(g8, 8), :])
        return carry

    lax.fori_loop(0, n_chunks, phase_c, 0)

    def scan_direction(sub, d):
        mask = (scol <= trow) if d == 0 else (scol >= trow)
        if zero_init:
            st_s[...] = jnp.zeros(st_s.shape, F32)
            m_s[...] = jnp.zeros(m_s.shape, F32)
        else:
            st_s[...] = s0_ref[sub, 0, d]
            m_s[...] = m0_ref[sub, 0]

        def body(ci, carry):
            c = sub * cps + (ci if d == 0 else cps - 1 - ci)
            r0 = pl.multiple_of(c * CHUNK, CHUNK)
            cumc = cumc_s[c]
            m_prev = m_s[...]
            m_prev2 = m_prev * LOG2E
            mx = jnp.maximum(m_prev2, pmaxc_s[c])
            negmt = -(cumc * LOG2E + mx)
            tot = cumc[CHUNK - 1:CHUNK, :] if d == 0 else cumc[0:1, :]
            mloc = pltpu.roll(cumc[0:1, :], LANES - 8, 1)
            m_new = jnp.maximum(tot + m_prev, mloc)
            a_old = jnp.exp(tot + m_prev - m_new)
            a_loc = jnp.exp(mloc - m_new)
            for pr in (0, 1):
                stp = st_s[pr]
                stp_bf = stp.astype(BF16)
                kt = kt_s[c * 2 + pr]
                for hh in (0, 1):
                    hl = pr * 2 + hh
                    j = d * 4 + hl
                    c0 = hl * LANES
                    e_row = et_s[pl.ds(c * GROWS + 8 + j, 1), :]
                    wend_row = wendt_s[pl.ds(c * GROWS + 8 + j, 1), :]
                    mxb = jnp.broadcast_to(mx[:, j:j + 1], (CHUNK, LANES))
                    sqk = (s_s[c * 4 + hl] * jnp.exp2(jnp.where(mask, e_row - mxb, -jnp.inf))).astype(BF16)
                    qw = qm_s[c * 4 + hl] * jnp.exp2(m_prev2[:, j:j + 1] - mxb).astype(BF16)
                    vaug = jnp.concatenate([v_ref[pl.ds(r0, CHUNK), c0:c0 + LANES], ones_blk], axis=1)
                    nd = _dot(jnp.concatenate([sqk, qw], axis=1), jnp.concatenate([vaug, stp_bf], axis=0))
                    emt = jnp.exp2(jnp.broadcast_to(negmt[:, j:j + 1], (CHUNK, LANES)))
                    den = jnp.maximum(jnp.abs(nd[:, LANES:2 * LANES]), emt)
                    hout = nd[:, 0:LANES] / den
                    if d == 0:
                        hacc_s[pl.ds(r0, CHUNK), c0:c0 + LANES] = hout
                    else:
                        hacc_s[pl.ds(r0, CHUNK), c0:c0 + LANES] += hout
                    ktw = (kt[hh * 64:(hh + 1) * 64, :] * (wend_row * a_loc[:, j:j + 1])).astype(BF16)
                    st_s[pr, hh * 64:(hh + 1) * 64, :] = (a_old[:, j:j + 1] * stp[hh * 64:(hh + 1) * 64, :]
                                                          + _dot(ktw, vaug))
            m_s[...] = m_new
            return carry

        lax.fori_loop(0, cps, body, 0, unroll=min(4, cps))
        if state_out:
            for pr in (0, 1):
                t = st_s[pr, :, 0:LANES].T
                t_hi = pltpu.roll(t, 64, 1)
                cfin_ref[sub, 0, d, pr * 2] = t[:, 0:64]
                cfin_ref[sub, 0, d, pr * 2 + 1] = t_hi[:, 0:64]
                nfin_ref[sub, 0, d, pr:pr + 1, :] = st_s[pr, :, LANES:2 * LANES].T[0:1, :]
            mfin_ref[sub, 0, d:d + 1, :] = m_s[...]

    if state_out == 'first':
        _zero_other_layers(cfin_ref, n_sub)
    if state_out:
        nfin_ref[...] = jnp.zeros(nfin_ref.shape, F32)

    def scan_sequence(sub, carry):
        scan_direction(sub, 0)
        scan_direction(sub, 1)
        return carry

    lax.fori_loop(0, n_sub, scan_sequence, 0)

    def phase_e(c, carry):
        r0 = pl.multiple_of(c * CHUNK, CHUNK)
        for hl in range(4):
            c0 = hl * LANES
            hm_ref[pl.ds(r0, CHUNK), c0:c0 + LANES] = _rms(
                hacc_s[pl.ds(r0, CHUNK), c0:c0 + LANES], nw_ref[0, :, c0:c0 + LANES]).astype(hm_ref.dtype)
        return carry

    lax.fori_loop(0, n_chunks, phase_e, 0)


def _mlstm(proj, gates, par, normw, s0, m0, *, row0, n_seq, seq_len, n_sub=1, state_layer=None,
           state_prev=None, depth=1):
    zero_init = s0 is None
    state_out = None if state_layer is None else ('first' if state_layer == 0 else 'next')
    blk_len = n_sub * seq_len
    rb = row0 // blk_len
    nc = blk_len // CHUNK
    in_specs = [
        pl.BlockSpec((blk_len, 256), lambda b, s: (rb + b, COL_Q // 256 + s)),
        pl.BlockSpec((blk_len, 256), lambda b, s: (rb + b, COL_K // 256 + s)),
        pl.BlockSpec((blk_len, 512), lambda b, s: (rb + b, COL_V // 512 + s)),
        pl.BlockSpec((blk_len, LANES), lambda b, s: (rb + b, s)),
        pl.BlockSpec((1, 8, LANES), lambda b, s: (s, 0, 0)),
        pl.BlockSpec((1, 1, 512), lambda b, s: (s, 0, 0)),
    ]
    args = [proj, proj, proj, gates, par, normw]
    if not zero_init:
        in_specs.append(pl.BlockSpec((n_sub, 1, 2, 2, 128, 256), lambda b, s: (b, s, 0, 0, 0, 0)))
        in_specs.append(pl.BlockSpec((n_sub, 1, 1, LANES), lambda b, s: (b, s, 0, 0)))
        args += [s0, m0]
    out_specs = [pl.BlockSpec((blk_len, 512), lambda b, s: (b, s))]
    out_shape = [jax.ShapeDtypeStruct((n_seq * seq_len, 1024), BF16)]
    aliases = {}
    if state_out:
        layers = depth if state_out == 'first' else 1
        layer0 = state_layer
        out_specs.append(pl.BlockSpec((n_sub, layers, 2, 4, ML_DV, ML_DK), lambda b, s: (b, layer0, 0, s, 0, 0)))
        out_shape.append(jax.ShapeDtypeStruct((n_seq, depth, 2, ML_HEADS, ML_DV, ML_DK), F32))
        out_specs.append(pl.BlockSpec((n_sub, 1, 2, 8, LANES), lambda b, s: (b, s, 0, 0, 0)))
        out_shape.append(jax.ShapeDtypeStruct((n_seq, 2, 2, 8, LANES), F32))
        out_specs.append(pl.BlockSpec((n_sub, 1, 2, LANES), lambda b, s: (b, s, 0, 0)))
        out_shape.append(jax.ShapeDtypeStruct((n_seq, 2, 2, LANES), F32))
        if state_out == 'next':
            aliases = {len(args): 1}
            in_specs.append(pl.BlockSpec(memory_space=pl.ANY))
            args.append(state_prev)
    gate_rows = pltpu.VMEM((nc * GROWS, LANES), F32)
    res = pl.pallas_call(
        functools.partial(_mlstm_kernel, blk_len=blk_len, seq_len=seq_len, zero_init=zero_init,
                          state_out=state_out),
        grid=(n_seq // n_sub, 2),
        in_specs=in_specs,
        out_specs=out_specs,
        out_shape=out_shape,
        input_output_aliases=aliases,
        scratch_shapes=[pltpu.VMEM((nc * 4, CHUNK, LANES), BF16),
                        pltpu.VMEM((nc * 4, CHUNK, CHUNK), F32),
                        pltpu.VMEM((nc * 2, LANES, CHUNK), F32),
                        pltpu.VMEM((blk_len, 512), F32),
                        gate_rows, gate_rows, gate_rows, gate_rows, gate_rows, gate_rows,
                        pltpu.VMEM((nc, CHUNK, LANES), F32),
                        pltpu.VMEM((nc, CHUNK, LANES), F32),
                        pltpu.VMEM((2, 128, 256), F32),
                        pltpu.VMEM((1, LANES), F32)],
        compiler_params=_cparams(("parallel", "arbitrary")),
        name="mlstm_scan",
    )(*args)
    return res if state_out else (res[0], None, None, None)


def _outproj_kernel(yc_ref, yl_ref, z_ref, hc_ref, hl_ref, o_ref, gs_ref, gm_ref, ha_ref, hb_ref, wos_ref,
                    wom_ref, wout_ref, sn_ref, nf_ref, g1_ref, sc2_ref, sh2_ref, hout_ref, u2_ref, *,
                    n_ctx_tiles):
    is_ctx = pl.program_id(0) < n_ctx_tiles
    y_in = jnp.where(is_ctx, yc_ref[...], yl_ref[...]).astype(F32)
    hm_in = jnp.where(is_ctx, hc_ref[...], hl_ref[...]).astype(F32)
    y = y_in * _silu(z_ref[...].astype(F32))
    y = _rms(y, sn_ref[...]).astype(BF16)
    y_ssd = _dot(y, wos_ref[...])
    hm = (hm_in * jax.nn.sigmoid(o_ref[...].astype(F32))).astype(BF16)
    y_ml = _dot(hm, wom_ref[...])
    mix = (jax.nn.sigmoid(gs_ref[...].astype(F32)) * y_ssd
           + jax.nn.sigmoid(gm_ref[...].astype(F32)) * y_ml).astype(BF16)
    h = jnp.where(is_ctx, ha_ref[...], hb_ref[...]) + g1_ref[0] * _dot(mix, wout_ref[...])
    hout_ref[...] = h
    u2_ref[...] = (_rms(h, nf_ref[...]) * (1.0 + sc2_ref[0]) + sh2_ref[0]).astype(u2_ref.dtype)


def _outproj(y_ctx, y_lat, hm_ctx, hm_lat, proj, rows, wos, wom, wout, ssd_norm, norm_ffn, mod48, lat_len,
             tm=512):
    t, d = proj.shape[0], wos.shape[0]
    n_ctx_rows = rows.n_ctx_rows
    nct = n_ctx_rows // tm
    mi = functools.partial(_mod_index, tm=tm, n_ctx_rows=n_ctx_rows, lat_len=lat_len)
    row = lambda i: (i, 0)
    ctx_row = lambda i: (jnp.minimum(i, nct - 1), 0)
    lat_row = lambda i: (jnp.maximum(i - nct, 0), 0)
    const = lambda i: (0, 0)
    col = lambda k: (lambda i: (i, k))
    modspec = lambda which: pl.BlockSpec((1, 1, d), lambda i: (mi(i, which=which), 0, 0))
    return pl.pallas_call(
        functools.partial(_outproj_kernel, n_ctx_tiles=nct),
        grid=(t // tm,),
        in_specs=[pl.BlockSpec((tm, d), ctx_row), pl.BlockSpec((tm, d), lat_row),
                  pl.BlockSpec((tm, d), col(COL_Z // d)),
                  pl.BlockSpec((tm, d), ctx_row), pl.BlockSpec((tm, d), lat_row),
                  pl.BlockSpec((tm, d), col(COL_O // d)),
                  pl.BlockSpec((tm, d), col(COL_GS // d)), pl.BlockSpec((tm, d), col(COL_GM // d))]
        + rows.specs(tm, d) + [
                  pl.BlockSpec((d, d), const), pl.BlockSpec((d, d), const), pl.BlockSpec((d, d), const),
                  pl.BlockSpec((1, d), const), pl.BlockSpec((1, d), const),
                  modspec(2), modspec(4), modspec(3)],
        out_specs=[pl.BlockSpec((tm, d), row), pl.BlockSpec((tm, d), row)],
        out_shape=[jax.ShapeDtypeStruct((t, d), F32), jax.ShapeDtypeStruct((t, d), BF16)],
        compiler_params=_cparams(("parallel",)),
        name="merge_outproj",
    )(y_ctx, y_lat, proj, hm_ctx, hm_lat, proj, proj, proj, rows.ctx, rows.lat, wos, wom, wout,
      ssd_norm.reshape(1, d), norm_ffn.reshape(1, d), mod48, mod48, mod48)


FFN_TM = 512
FFN_TF = 256


def _ffn_kernel(u_ref, ut_ref, ub_ref, wup_ref, cw_ref, wd_ref, h_ref, g2_ref, *rest,
                n_ctx_tiles, ctx_len, tiles_per_seq, final):
    if final:
        fw_ref, yc_ref, yl_ref, act_s = rest
    else:
        o_ref, act_s = rest
    i = pl.program_id(0)
    tm, tf = FFN_TM, FFN_TF
    n_chunks = D_FF // tf
    ext = tm + 2 * GRID_W

    def cw(r, j):
        return cw_ref[r:r + 1, j * tf:(j + 1) * tf]

    @pl.when(i < n_ctx_tiles)
    def _():
        u = u_ref[...]
        pos = lax.broadcasted_iota(jnp.int32, (tm, tf), 0) & (ctx_len - 1)
        for j in range(n_chunks):
            g = _dot(u, wup_ref[:, j * tf:(j + 1) * tf])
            val = _dot(u, wup_ref[:, D_FF + j * tf:D_FF + (j + 1) * tf])
            gl = jnp.where(pos != 0, pltpu.roll(g, 1, 0), 0.0)
            gr = jnp.where(pos != ctx_len - 1, pltpu.roll(g, tm - 1, 0), 0.0)
            conv = gl * cw(3, j) + g * cw(4, j) + gr * cw(5, j) + cw(9, j)
            act_s[:, j * tf:(j + 1) * tf] = (_silu(conv) * val).astype(BF16)

    @pl.when(i >= n_ctx_tiles)
    def _():
        ti = (i - n_ctx_tiles) % tiles_per_seq
        u = u_ref[...]
        top = jnp.where(ti > 0, ut_ref[...], jnp.zeros_like(ut_ref[...]))
        bot = jnp.where(ti < tiles_per_seq - 1, ub_ref[...], jnp.zeros_like(ub_ref[...]))
        uext = jnp.concatenate([top, u, bot], axis=0)
        col = lax.broadcasted_iota(jnp.int32, (ext, tf), 0) & (GRID_W - 1)
        for j in range(n_chunks):
            gx = _dot(uext, wup_ref[:, j * tf:(j + 1) * tf])
            val = _dot(u, wup_ref[:, D_FF + j * tf:D_FF + (j + 1) * tf])
            gl = jnp.where(col != 0, pltpu.roll(gx, 1, 0), 0.0)
            gr = jnp.where(col != GRID_W - 1, pltpu.roll(gx, ext - 1, 0), 0.0)

            def taps(r, lo):
                return (gl[lo:lo + tm] * cw(3 * r, j) + gx[lo:lo + tm] * cw(3 * r + 1, j)
                        + gr[lo:lo + tm] * cw(3 * r + 2, j))

            conv = taps(1, GRID_W) + taps(0, 0) + taps(2, 2 * GRID_W) + cw(9, j)
            act_s[:, j * tf:(j + 1) * tf] = (_silu(conv) * val).astype(BF16)

    h_new = h_ref[...] + g2_ref[0] * _dot(act_s[...], wd_ref[...])
    if final:
        y = _rms(h_new, fw_ref[...])

        @pl.when(i < n_ctx_tiles)
        def _():
            yc_ref[...] = y

        @pl.when(i >= n_ctx_tiles)
        def _():
            yl_ref[...] = y
    else:
        o_ref[...] = h_new


def _ffn(u2, wup, cw16, wd, h, mod48, n_ctx_rows, ctx_len, lat_len, final_w=None):
    t, d = h.shape
    tm = FFN_TM
    nct = n_ctx_rows // tm
    hb = tm // GRID_W
    n_hblocks = t // GRID_W
    final = final_w is not None
    mi = functools.partial(_mod_index, tm=tm, n_ctx_rows=n_ctx_rows, lat_len=lat_len)
    resident = lambda shape: pl.BlockSpec(shape, lambda i: (0, 0), pipeline_mode=pl.Buffered(1))
    in_specs = [pl.BlockSpec((tm, d), lambda i: (i, 0)),
                pl.BlockSpec((GRID_W, d), lambda i: (jnp.maximum(i * hb - 1, 0), 0)),
                pl.BlockSpec((GRID_W, d), lambda i: (jnp.minimum((i + 1) * hb, n_hblocks - 1), 0)),
                resident((d, 2 * D_FF)),
                resident((16, D_FF)),
                resident((D_FF, d)),
                pl.BlockSpec((tm, d), lambda i: (i, 0)),
                pl.BlockSpec((1, 1, d), lambda i: (mi(i, which=5), 0, 0))]
    args = [u2, u2, u2, wup, cw16, wd, h, mod48]
    if final:
        in_specs.append(pl.BlockSpec((1, d), lambda i: (0, 0)))
        args.append(final_w.reshape(1, d))
        out_specs = [pl.BlockSpec((tm, d), lambda i: (jnp.minimum(i, nct - 1), 0)),
                     pl.BlockSpec((tm, d), lambda i: (jnp.maximum(i - nct, 0), 0))]
        out_shape = [jax.ShapeDtypeStruct((n_ctx_rows, d), F32), jax.ShapeDtypeStruct((t - n_ctx_rows, d), F32)]
    else:
        out_specs = pl.BlockSpec((tm, d), lambda i: (i, 0))
        out_shape = jax.ShapeDtypeStruct((t, d), F32)
    return pl.pallas_call(
        functools.partial(_ffn_kernel, n_ctx_tiles=nct, ctx_len=ctx_len, tiles_per_seq=lat_len // tm,
                          final=final),
        grid=(t // tm,),
        in_specs=in_specs,
        out_specs=out_specs,
        out_shape=out_shape,
        scratch_shapes=[pltpu.VMEM((tm, D_FF), BF16)],
        compiler_params=_cparams(("arbitrary",) if final else ("parallel",)),
        name="convffn",
    )(*args)


def _arrange_w_in(w):
    z, xbc, dt, q, k, v, o, ig, fg, gates = (
        w[:, 0:1024], w[:, 1024:2560], w[:, 2560:2592], w[:, 2592:3104], w[:, 3104:3616],
        w[:, 3616:4640], w[:, 4640:5664], w[:, 5664:5680], w[:, 5680:5696], w[:, 5696:7744])
    big = jnp.concatenate([z, o, gates, xbc, q, k, v], axis=1).astype(BF16)
    blocks = []
    for s in (0, 1):
        blocks += [dt[:, s * 8:s * 8 + 8], dt[:, 16 + s * 8:16 + s * 8 + 8],
                   ig[:, s * 4:s * 4 + 4], ig[:, 8 + s * 4:8 + s * 4 + 4],
                   fg[:, s * 4:s * 4 + 4], fg[:, 8 + s * 4:8 + s * 4 + 4],
                   jnp.zeros((w.shape[0], LANES - 32), w.dtype)]
    small = jnp.concatenate(blocks, axis=1).astype(BF16)
    return big, small


def _slab_rows(rows, n_layers, n_rows=8):
    ns = 2 * n_layers
    out = []
    for pieces in rows:
        used = sum(p.shape[1] for p in pieces)
        out.append(jnp.concatenate(list(pieces) + [jnp.zeros((ns, LANES - used), F32)], axis=1)[:, None, :])
    out.append(jnp.zeros((ns, n_rows - len(rows), LANES), F32))
    return jnp.concatenate(out, axis=1).reshape(n_layers, 2, n_rows, LANES)


def _ssd_params(dt_bias, a_log, d_skip, conv_w, conv_b):
    nl = dt_bias.shape[0]
    slab = lambda x: x.reshape(2 * nl, 8)
    par = _slab_rows([[slab(dt_bias[:, 0]), slab(dt_bias[:, 1])], [slab(a_log[:, 0]), slab(a_log[:, 1])]], nl)
    dvec = jnp.repeat(d_skip, SSD_HEADDIM, axis=-1).reshape(nl, 2, 1, 512)
    cw = jnp.concatenate([conv_w, conv_b[:, None], jnp.zeros((nl, 4, conv_w.shape[2]), F32)], axis=1)
    cwx = cw[:, :, 0:1024].reshape(nl, 8, 2, 512).transpose(0, 2, 1, 3)
    cwb = cw[:, :, 1024:1280].reshape(nl, 8, 2, LANES).transpose(0, 2, 1, 3)
    cwc = cw[:, :, 1280:1536].reshape(nl, 8, 2, LANES).transpose(0, 2, 1, 3)
    return par, dvec, cwx, cwb, cwc


def _ml_params(i_bias, f_bias, ml_norm):
    nl = i_bias.shape[0]
    slab = lambda x: x.reshape(2 * nl, 4)
    par = _slab_rows([[jnp.zeros((2 * nl, LANE_I), F32), slab(i_bias[:, 0]), slab(i_bias[:, 1]),
                       slab(f_bias[:, 0]), slab(f_bias[:, 1])]], nl)
    return par, ml_norm.reshape(nl, 2, 1, 512)


def _ssd_state_to_kernel(s):
    b, nl = s.shape[0], s.shape[1]
    s = s.reshape(b, nl, 2, 2, 2, 4, 64, 64).transpose(1, 0, 3, 2, 4, 7, 5, 6)
    return s.reshape(nl, b, 2, 2, 128, 256)


def _ml_state_to_kernel(c0, n0, m0):
    b, nl = c0.shape[0], c0.shape[1]
    aug = jnp.concatenate([jnp.swapaxes(c0, -1, -2),
                           jnp.broadcast_to(n0[..., None], n0.shape + (LANES,))], axis=-1)
    aug = aug.reshape(b, nl, 2, 2, 2, 2, 64, 256).transpose(1, 0, 3, 2, 4, 5, 6, 7).reshape(nl, b, 2, 2, 2, 128, 256)
    m = m0.reshape(b, nl, 2, 2, 4).transpose(1, 0, 3, 2, 4).reshape(nl, b, 2, 1, 8)
    mk = jnp.concatenate([m, jnp.zeros((nl, b, 2, 1, LANES - 8), F32)], axis=-1)
    return aug, mk


def _ml_nm_from_kernel(nfin, mfin):
    nl, b = nfin.shape[0], nfin.shape[1]
    n = nfin[:, :, :, :, 0:2, :].reshape(nl, b, 2, 2, 2, 2, ML_DK).transpose(1, 0, 3, 2, 4, 5, 6)
    m = jnp.stack([mfin[:, :, :, 0, 0:4], mfin[:, :, :, 1, 4:8]], axis=2).transpose(1, 0, 2, 3, 4)
    return n.reshape(b, nl, 2, ML_HEADS, ML_DK), m.reshape(b, nl, 2, ML_HEADS)


def kernel(x_prompt, x_sample, state_ssd, state_mlstm_C, state_mlstm_n, state_mlstm_m, c, c_ctx, w_ada, b_ada,
           norm_mix, w_in, ssd_conv_w, ssd_conv_b, ssd_dt_bias, ssd_a_log, ssd_d, ssd_norm, w_o_ssd, ml_i_bias,
           ml_f_bias, ml_norm, w_o_ml, w_out, norm_ffn, w_up, ffn_conv_w, ffn_conv_b, w_down, final_norm):
    nb, ctx_len, d = x_prompt.shape
    nlat, lat_len, _ = x_sample.shape
    depth = w_in.shape[0]
    n_ctx_rows = nb * ctx_len
    n_lat_rows = nlat * lat_len

    n_rows = n_ctx_rows + n_lat_rows
    rows = _Rows(x_prompt.reshape(n_ctx_rows, d), x_sample.reshape(n_lat_rows, d), n_ctx_rows)
    cond8 = jnp.concatenate([c_ctx[None], c, jnp.zeros((8 - 1 - nlat, d), F32)], axis=0)
    mod = _ada(cond8, w_ada, b_ada)

    mod48 = mod.reshape(depth, 48, 1, d)
    w_big, w_small = zip(*[_arrange_w_in(w_in[l]) for l in range(depth)])
    par_s, dvec, cwx, cwb, cwc = _ssd_params(ssd_dt_bias, ssd_a_log, ssd_d, ssd_conv_w, ssd_conv_b)
    par_m, normw = _ml_params(ml_i_bias, ml_f_bias, ml_norm)
    s0s = _ssd_state_to_kernel(state_ssd)
    s0m, m0m = _ml_state_to_kernel(state_mlstm_C, state_mlstm_n, state_mlstm_m)
    cw16 = jnp.concatenate([ffn_conv_w.reshape(depth, 9, D_FF), ffn_conv_b[:, None],
                            jnp.zeros((depth, 6, D_FF), F32)], axis=1)
    wos, wom, wout = w_o_ssd.astype(BF16), w_o_ml.astype(BF16), w_out.astype(BF16)
    wup, wdn = w_up.astype(BF16), w_down.astype(BF16)

    new_ssd, new_c, n_l, m_l = None, None, [], []
    for l in range(depth):
        proj, gates = _in_proj(rows, norm_mix[l], mod48[l], w_big[l], w_small[l], n_rows, lat_len)

        ssd_par = (par_s[l], dvec[l], cwx[l], cwb[l], cwc[l])
        y_ctx, new_ssd = _ssd(proj, gates, *ssd_par, None, row0=0, n_seq=nb, seq_len=ctx_len,
                              n_sub=CTX_SEQS_PER_STEP, state_layer=l, state_prev=new_ssd, depth=depth)
        y_lat, _ = _ssd(proj, gates, *ssd_par, s0s[l], row0=n_ctx_rows, n_seq=nlat, seq_len=lat_len)
        hm_ctx, new_c, n_ctx, m_ctx = _mlstm(proj, gates, par_m[l], normw[l], None, None, row0=0, n_seq=nb,
                                             seq_len=ctx_len, n_sub=CTX_SEQS_PER_STEP, state_layer=l,
                                             state_prev=new_c, depth=depth)
        hm_lat, _, _, _ = _mlstm(proj, gates, par_m[l], normw[l], s0m[l], m0m[l],
                                 row0=n_ctx_rows, n_seq=nlat, seq_len=lat_len)
        h, u2 = _outproj(y_ctx, y_lat, hm_ctx, hm_lat, proj, rows, wos[l], wom[l], wout[l], ssd_norm[l],
                         norm_ffn[l], mod48[l], lat_len)
        n_l.append(n_ctx)
        m_l.append(m_ctx)
        if l < depth - 1:
            h = _ffn(u2, wup[l], cw16[l], wdn[l], h, mod48[l], n_ctx_rows, ctx_len, lat_len)
            rows = _Rows(h, h, n_ctx_rows)
        else:
            y_ctx_rows, y_lat_rows = _ffn(u2, wup[l], cw16[l], wdn[l], h, mod48[l], n_ctx_rows, ctx_len, lat_len,
                                          final_w=final_norm)

    n_new, m_new = _ml_nm_from_kernel(jnp.stack(n_l), jnp.stack(m_l))
    return (y_ctx_rows.reshape(nb, ctx_len, d), y_lat_rows.reshape(nlat, lat_len, d), new_ssd, new_c, n_new, m_new)
```
